```python
import jax, jax.numpy as jnp
from jax import lax
import numpy as np

D_MODEL = 2048
BATCH = 1
SEQ = 8192
DEPTH = 1

D_MIX = D_MODEL
D_SSD = D_MIX // 2
SSD_HEAD_DIM = 64
SSD_HEADS = D_SSD // SSD_HEAD_DIM
SSD_GROUPS = 2
SSD_HEADS_PER_GROUP = SSD_HEADS // SSD_GROUPS
SSD_STATE = 128
SSD_CONV = 5
SSD_CHUNK = 128
SSD_CONV_DIM = D_SSD + 2 * SSD_GROUPS * SSD_STATE
D_POOL = D_MIX - D_SSD
POOL_WINDOWS = (2, 4, 8, 16)
POOL_GROUPS = len(POOL_WINDOWS)
POOL_GROUP_DIM = D_POOL // POOL_GROUPS
D_IN_PROJ = D_SSD + SSD_CONV_DIM + 2 * SSD_HEADS + D_POOL
D_FF = 5632
FFN_CONV = 3
D_PLE = 256
EPS = 1e-6

kernel_name = "hybrid_ssd_pool_convglu_encoder_block"

F32 = jnp.float32


def rmsnorm(x, g):
    xf = x.astype(F32)
    y = xf * lax.rsqrt(jnp.mean(xf * xf, axis=-1, keepdims=True) + EPS)
    return (y * g.astype(F32)).astype(x.dtype)


def dwconv_centred(u, w, bias):
    k = w.shape[0]
    L = u.shape[1]
    half = k // 2
    up = jnp.pad(u, ((0, 0), (half, half), (0, 0)))
    wf = w.astype(u.dtype)
    y = up[:, 0:L] * wf[0]
    for j in range(1, k):
        y = y + up[:, j:j + L] * wf[j]
    return y + bias.astype(u.dtype)


def ssd_scan(xs, dt, a, bm, cm):
    b, L, g, e, pdim = xs.shape
    n = bm.shape[-1]
    c = L // SSD_CHUNK
    q = SSD_CHUNK
    xdt = (xs.astype(F32) * dt[..., None]).reshape(b, c, q, g, e, pdim)
    da = jnp.moveaxis((dt * a).reshape(b, c, q, g, e), 2, -1)
    acum = jnp.cumsum(da, axis=-1)
    bmc = bm.astype(F32).reshape(b, c, q, g, n)
    cmc = cm.astype(F32).reshape(b, c, q, g, n)
    causal = jnp.tril(jnp.ones((q, q), dtype=bool))
    seg = acum[..., :, None] - acum[..., None, :]
    decay = jnp.exp(jnp.where(causal, seg, -jnp.inf))
    cb = jnp.einsum("bclgn,bcsgn->bcgls", cmc, bmc)
    y_diag = jnp.einsum("bcgls,bcgels,bcsgep->bclgep", cb, decay, xdt)
    decay_to_end = jnp.exp(acum[..., -1:] - acum)
    states = jnp.einsum("bclgn,bcgel,bclgep->bcgepn", bmc, decay_to_end, xdt)
    chunk_decay = jnp.exp(acum[..., -1])

    def step(h, inp):
        s, d = inp
        return h * d[..., None, None] + s, h

    h0 = jnp.zeros((b, g, e, pdim, n), F32)
    _, h_in = lax.scan(step, h0, (jnp.moveaxis(states, 1, 0), jnp.moveaxis(chunk_decay, 1, 0)))
    h_in = jnp.moveaxis(h_in, 0, 1)
    y_off = jnp.einsum("bclgn,bcgepn,bcgel->bclgep", cmc, h_in, jnp.exp(acum))
    return (y_diag + y_off).reshape(b, L, g, e, pdim)


def ssd_mixer(z, xbc, dt_raw, conv_w, conv_b, dt_bias, a_log, d_skip, norm_w):
    b, L, _ = z.shape
    G, E = SSD_GROUPS, SSD_HEADS_PER_GROUP
    xbc = jax.nn.silu(dwconv_centred(xbc, conv_w, conv_b))
    xs, bm, cm = jnp.split(xbc, [D_SSD, D_SSD + G * SSD_STATE], axis=-1)
    xs = xs.reshape(b, L, G, E, SSD_HEAD_DIM)
    bm = bm.reshape(b, L, G, SSD_STATE)
    cm = cm.reshape(b, L, G, SSD_STATE)
    dt = jax.nn.softplus(dt_raw.astype(F32).reshape(b, L, 2, SSD_HEADS) + dt_bias.astype(F32))
    dt = dt.reshape(b, L, 2, G, E)
    a = (-jnp.exp(a_log.astype(F32))).reshape(2, G, E)
    flip = lambda t: jnp.flip(t, axis=1)
    y_fwd = ssd_scan(xs, dt[:, :, 0], a[0], bm, cm)
    y_bwd = flip(ssd_scan(flip(xs), flip(dt[:, :, 1]), a[1], flip(bm), flip(cm)))
    y = y_fwd + y_bwd + xs.astype(F32) * d_skip.astype(F32).reshape(G, E)[:, :, None]
    y = y.reshape(b, L, D_SSD) * jax.nn.silu(z.astype(F32))
    y = y.reshape(b, L, G, D_SSD // G)
    y = y * lax.rsqrt(jnp.mean(y * y, axis=-1, keepdims=True) + EPS)
    return (y.reshape(b, L, D_SSD) * norm_w.astype(F32)).astype(z.dtype)


def pool_mixer(u, w, scale):
    b, L, _ = u.shape
    uf = u.astype(F32).reshape(b, L, POOL_GROUPS, POOL_GROUP_DIM)
    cs = jnp.concatenate([jnp.zeros((b, 1, POOL_GROUPS, POOL_GROUP_DIM), F32),
                          jnp.cumsum(uf, axis=1)], axis=1)
    t = jnp.arange(L)
    means = []
    for gi, k in enumerate(POOL_WINDOWS):
        lo = jnp.clip(t - k // 2, 0, L)
        hi = jnp.clip(t + (k - k // 2), 0, L)
        csg = cs[:, :, gi]
        cnt = (hi - lo).astype(F32)[None, :, None]
        means.append((csg[:, hi] - csg[:, lo]) / cnt)
    mixed = jnp.stack(means, axis=2) - uf
    y = jnp.einsum("blgc,gcd->blgd", mixed, w.astype(F32))
    return (y.reshape(b, L, D_POOL) * scale.astype(F32)).astype(u.dtype)


def conv_glu(hn, w_up, conv_w, conv_b, w_down):
    up = hn @ w_up
    gate, val = jnp.split(up, 2, axis=-1)
    gate = dwconv_centred(gate, conv_w, conv_b)
    return (jax.nn.gelu(gate, approximate=True) * val) @ w_down


def setup_inputs(seed: int = 0) -> dict:
    key = jax.random.key(seed)
    ks = jax.random.split(key, 24)
    nrm = lambda k, shape, s: jax.random.normal(k, shape, F32) * s
    gain = lambda k: 1.0 + 0.02 * jax.random.normal(k, (DEPTH, D_MODEL), F32)
    dt0 = jnp.exp(jax.random.uniform(ks[6], (DEPTH, 2, SSD_HEADS), F32,
                                     np.log(1e-3).astype(np.float32), np.log(1e-1).astype(np.float32)))
    dt_bias = dt0 + jnp.log(-jnp.expm1(-dt0))
    return {
        "x": jax.random.normal(ks[0], (BATCH, SEQ, D_MODEL), F32),
        "p": jax.random.normal(ks[1], (DEPTH, BATCH, SEQ, D_PLE), F32),
        "mix_norm_pre": gain(ks[2]),
        "mix_norm_post": gain(ks[3]),
        "w_in": nrm(ks[4], (DEPTH, D_MODEL, D_IN_PROJ), D_MODEL ** -0.5),
        "ssd_conv_w": nrm(ks[5], (DEPTH, SSD_CONV, SSD_CONV_DIM), SSD_CONV ** -0.5),
        "ssd_conv_b": nrm(ks[7], (DEPTH, SSD_CONV_DIM), 0.02),
        "ssd_dt_bias": dt_bias,
        "ssd_a_log": jnp.log(jax.random.uniform(ks[8], (DEPTH, 2, SSD_HEADS), F32, 1.0, 16.0)),
        "ssd_d": 1.0 + 0.1 * jax.random.normal(ks[9], (DEPTH, SSD_HEADS), F32),
        "ssd_norm": 1.0 + 0.02 * jax.random.normal(ks[10], (DEPTH, D_SSD), F32),
        "pool_w": nrm(ks[11], (DEPTH, POOL_GROUPS, POOL_GROUP_DIM, POOL_GROUP_DIM), POOL_GROUP_DIM ** -0.5),
        "pool_scale": 1.0 + 0.1 * jax.random.normal(ks[12], (DEPTH, D_POOL), F32),
        "w_out": nrm(ks[13], (DEPTH, D_MIX, D_MODEL), D_MIX ** -0.5),
        "ffn_norm_pre": gain(ks[14]),
        "ffn_norm_post": gain(ks[15]),
        "w_ffn_up": nrm(ks[16], (DEPTH, D_MODEL, 2 * D_FF), D_MODEL ** -0.5),
        "ffn_conv_w": nrm(ks[17], (DEPTH, FFN_CONV, D_FF), FFN_CONV ** -0.5),
        "ffn_conv_b": nrm(ks[18], (DEPTH, D_FF), 0.02),
        "w_ffn_down": nrm(ks[19], (DEPTH, D_FF, D_MODEL), D_FF ** -0.5),
        "ple_norm_pre": gain(ks[20]),
        "w_ple_gate": nrm(ks[21], (DEPTH, D_MODEL, D_MODEL), D_MODEL ** -0.5),
        "w_ple": nrm(ks[22], (DEPTH, D_PLE, D_MODEL), D_PLE ** -0.5),
        "ple_norm_post": gain(ks[23]),
    }


def reference(x, p, mix_norm_pre, mix_norm_post, w_in, ssd_conv_w, ssd_conv_b, ssd_dt_bias,
              ssd_a_log, ssd_d, ssd_norm, pool_w, pool_scale, w_out, ffn_norm_pre, ffn_norm_post,
              w_ffn_up, ffn_conv_w, ffn_conv_b, w_ffn_down, ple_norm_pre, w_ple_gate, w_ple,
              ple_norm_post):
    h = x
    split_at = [D_SSD, D_SSD + SSD_CONV_DIM, D_SSD + SSD_CONV_DIM + 2 * SSD_HEADS]
    for i in range(DEPTH):
        hn = rmsnorm(h, mix_norm_pre[i])
        proj = hn @ w_in[i]
        z, xbc, dt_raw, u = jnp.split(proj, split_at, axis=-1)
        y_ssd = ssd_mixer(z, xbc, dt_raw, ssd_conv_w[i], ssd_conv_b[i], ssd_dt_bias[i],
                          ssd_a_log[i], ssd_d[i], ssd_norm[i])
        y_pool = pool_mixer(u, pool_w[i], pool_scale[i])
        mix = jnp.concatenate([y_ssd, y_pool], axis=-1) @ w_out[i]
        h = h + rmsnorm(mix, mix_norm_post[i])
        hn = rmsnorm(h, ffn_norm_pre[i])
        ff = conv_glu(hn, w_ffn_up[i], ffn_conv_w[i], ffn_conv_b[i], w_ffn_down[i])
        h = h + rmsnorm(ff, ffn_norm_post[i])
        gate = jax.nn.sigmoid(rmsnorm(h, ple_norm_pre[i]) @ w_ple_gate[i])
        h = h + rmsnorm(gate * (p[i] @ w_ple[i]), ple_norm_post[i])
    return h
```

```python
import functools

import jax
import jax.numpy as jnp
from jax import lax
from jax.experimental import pallas as pl
from jax.experimental.pallas import tpu as pltpu

F32 = jnp.float32
BF16 = jnp.bfloat16
EPS = 1e-6

V7X_VMEM_BYTES = 64 * 1024 * 1024
V7X_LANES = 128
V7X_SUBLANES = 8

SSD_HEAD_DIM = 64
SSD_HEADS = 16
SSD_GROUPS = 2
SSD_HEADS_PER_GROUP = SSD_HEADS // SSD_GROUPS
SSD_STATE = 128
SSD_CONV = 5
SSD_CHUNK = 128
POOL_WINDOWS = (2, 4, 8, 16)
FFN_CONV = 3
NEG_BIG = -1e30


def _vmem_limit(nbytes):
    return int(min(nbytes, V7X_VMEM_BYTES - 6 * 1024 * 1024))


def _rms(x):
    return x * lax.rsqrt(jnp.mean(x * x, axis=-1, keepdims=True) + EPS)


def _split3(x):
    hi = x.astype(BF16)
    r1 = x - hi.astype(F32)
    mid = r1.astype(BF16)
    r2 = r1 - mid.astype(F32)
    lo = r2.astype(BF16)
    return hi, mid, lo


def _const_spec(shape):
    nd = len(shape)
    return pl.BlockSpec(shape, lambda *_: (0,) * nd, pipeline_mode=pl.Buffered(1))


def _inproj_kernel(x_ref, g_ref, wz_ref, wxs_ref, wbc_ref, wu_ref, wdt_ref,
                   z_ref, xs_ref, bc_ref, u_ref, dt_ref):
    hn = (_rms(x_ref[...]) * g_ref[...]).astype(BF16)
    for w_ref, o_ref in ((wz_ref, z_ref), (wxs_ref, xs_ref), (wbc_ref, bc_ref),
                         (wu_ref, u_ref), (wdt_ref, dt_ref)):
        o_ref[...] = jnp.dot(hn, w_ref[...], preferred_element_type=F32).astype(o_ref.dtype)


def _inproj(x, g, wz, wxs, wbc, wu, wdt, *, tm):
    L, D = x.shape
    outs = [(wz.shape[1], F32), (wxs.shape[1], F32), (wbc.shape[1], F32), (wu.shape[1], F32), (wdt.shape[1], F32)]
    row = lambda n: pl.BlockSpec((tm, n), lambda i: (i, 0))
    ncols = sum(n for n, _ in outs)
    est = 2 * tm * D * 4 + 2 * D * ncols * 2 + 2 * tm * ncols * 4 + tm * D * 8 + tm * 1024 * 8
    return pl.pallas_call(
        _inproj_kernel,
        grid=(L // tm,),
        in_specs=[row(D), _const_spec(g.shape)] + [_const_spec(w.shape) for w in (wz, wxs, wbc, wu, wdt)],
        out_specs=[row(n) for n, _ in outs],
        out_shape=[jax.ShapeDtypeStruct((L, n), dt) for n, dt in outs],
        compiler_params=pltpu.CompilerParams(dimension_semantics=("arbitrary",), vmem_limit_bytes=_vmem_limit(est)),
        name="inproj",
    )(x, g, wz, wxs, wbc, wu, wdt)


def _ssd_kernel(xsp_ref, xsm_ref, xsn_ref, bcp_ref, bcm_ref, bcn_ref, dt_ref, z_ref,
                cwx_ref, cbx_ref, cwb_ref, cbb_ref, dtb_ref, alog_ref, dexp_ref, nw_ref,
                t_ref, nm_ref, r3_ref, ecol_ref,
                o_ref,
                yb_ref, h_ref, extx_ref, extb_ref):
    Q = SSD_CHUNK
    P = SSD_HEAD_DIM
    E = SSD_HEADS_PER_GROUP
    GW = E * P
    s = pl.program_id(0)
    i = pl.program_id(1)
    nc = pl.num_programs(1)
    c = i + (1 - s) * (nc - 1 - 2 * i)
    is_fwd = s == 1

    @pl.when(i == 0)
    def _():
        h_ref[...] = jnp.zeros_like(h_ref)

    def conv_silu(p_ref, m_ref, n_ref, ext_ref, w_ref, b_ref):
        ext_ref[0:8, :] = jnp.where(c > 0, p_ref[...], 0.0)
        ext_ref[8:8 + Q, :] = m_ref[...]
        ext_ref[8 + Q:16 + Q, :] = jnp.where(c < nc - 1, n_ref[...], 0.0)
        half = SSD_CONV // 2
        acc = w_ref[0:1, :] * ext_ref[8 - half:8 - half + Q, :]
        for j in range(1, SSD_CONV):
            acc = acc + w_ref[j:j + 1, :] * ext_ref[8 - half + j:8 - half + j + Q, :]
        acc = acc + b_ref[...]
        return acc * jax.nn.sigmoid(acc)

    xs = conv_silu(xsp_ref, xsm_ref, xsn_ref, extx_ref, cwx_ref, cbx_ref)
    bc = conv_silu(bcp_ref, bcm_ref, bcn_ref, extb_ref, cwb_ref, cbb_ref)

    dt = jax.nn.softplus(dt_ref[...] + dtb_ref[...])
    a = -jnp.exp(alog_ref[...])
    da = dt * a
    da3 = jnp.concatenate(_split3(da), axis=0)
    acum = jnp.dot(t_ref[...], da3, preferred_element_type=F32)
    tot = jnp.where(is_fwd, acum[Q - 1:Q, :], acum[0:1, :])
    acum_t = acum.T
    e_a = jnp.exp(acum)
    w_end = jnp.exp(tot - acum)

    lhs = jnp.concatenate([jnp.concatenate(_split3(v), axis=1) for v in (dt, dt * w_end, e_a)], axis=0)
    exp3 = jnp.dot(lhs, r3_ref[...], preferred_element_type=F32)
    dt_x = exp3[0:Q]
    dtw_x = exp3[Q:2 * Q]
    ea_x = exp3[2 * Q:3 * Q]
    cd_x = jnp.where(is_fwd, ea_x[Q - 1:Q, :], ea_x[0:1, :])
    acol = jnp.dot(jnp.concatenate(_split3(acum), axis=1), ecol_ref[...], preferred_element_type=F32)

    xdt = (xs * dt_x).astype(BF16)
    xw = (xs * dtw_x).astype(BF16)
    negmask = nm_ref[...]
    lane = lax.broadcasted_iota(jnp.int32, (Q, 2 * P), 1)
    first_head = lane < P

    y_groups = []
    for g in range(SSD_GROUPS):
        bm = bc[:, g * SSD_STATE:(g + 1) * SSD_STATE].astype(BF16)
        cm = bc[:, (SSD_GROUPS + g) * SSD_STATE:(SSD_GROUPS + g + 1) * SSD_STATE].astype(BF16)
        cb = lax.dot_general(cm, bm, (((1,), (1,)), ((), ())), preferred_element_type=F32)
        h_in = h_ref[g]
        y_off = jnp.dot(cm, h_in.astype(BF16), preferred_element_type=F32)
        y_g = y_off * ea_x[:, g * GW:(g + 1) * GW]
        st = lax.dot_general(bm, xw[:, g * GW:(g + 1) * GW], (((0,), (0,)), ((), ())),
                             preferred_element_type=F32)
        h_ref[g] = h_in * cd_x[:, g * GW:(g + 1) * GW] + st
        pairs = []
        for hp in range(E // 2):
            ms = []
            for k in range(2):
                h = g * E + hp * 2 + k
                seg = acol[:, h * Q:(h + 1) * Q] - acum_t[h:h + 1, :] + negmask
                ms.append((cb * jnp.exp(seg)).astype(BF16))
            m2 = jnp.concatenate(ms, axis=1)
            xp = xdt[:, (g * E + hp * 2) * P:(g * E + hp * 2 + 2) * P]
            zero = jnp.zeros_like(xp)
            rhs = jnp.concatenate([jnp.where(first_head, xp, zero), jnp.where(first_head, zero, xp)], axis=0)
            pairs.append(jnp.dot(m2, rhs, preferred_element_type=F32))
        y_groups.append(y_g + jnp.concatenate(pairs, axis=1))
    y = jnp.concatenate(y_groups, axis=1)

    row0 = pl.multiple_of(c * Q, Q)

    @pl.when(s == 0)
    def _():
        yb_ref[pl.ds(row0, Q), :] = y

    @pl.when(s == 1)
    def _():
        yy = y + yb_ref[pl.ds(row0, Q), :] + xs * dexp_ref[...]
        z = z_ref[...]
        yy = yy * (z * jax.nn.sigmoid(z))
        outs = []
        for g in range(SSD_GROUPS):
            outs.append(_rms(yy[:, g * GW:(g + 1) * GW]))
        o_ref[...] = (jnp.concatenate(outs, axis=1) * nw_ref[...]).astype(o_ref.dtype)


def _ssd(xs_raw, bc_raw, dt_raw, z, cwx, cbx, cwb, cbb, dtb, alog, dexp, nw, tcat, negmask, r3, ecol):
    L, DS = xs_raw.shape
    Q = SSD_CHUNK
    nc = L // Q
    hb = Q // V7X_SUBLANES
    nhb = L // V7X_SUBLANES

    def cidx(s, i):
        return i + (1 - s) * (nc - 1 - 2 * i)

    def main(n):
        return pl.BlockSpec((Q, n), lambda s, i: (cidx(s, i), 0))

    def prev(n):
        return pl.BlockSpec((V7X_SUBLANES, n), lambda s, i: (jnp.maximum(cidx(s, i) * hb - 1, 0), 0))

    def nxt(n):
        return pl.BlockSpec((V7X_SUBLANES, n), lambda s, i: (jnp.minimum(cidx(s, i) * hb + hb, nhb - 1), 0))

    nbc = bc_raw.shape[1]
    in_specs = [
        prev(DS), main(DS), nxt(DS), prev(nbc), main(nbc), nxt(nbc),
        pl.BlockSpec((Q, V7X_LANES), lambda s, i: (cidx(s, i), 1 - s)),
        pl.BlockSpec((Q, DS), lambda s, i: (s * i, 0)),
        _const_spec(cwx.shape), _const_spec(cbx.shape), _const_spec(cwb.shape), _const_spec(cbb.shape),
        pl.BlockSpec((None, 1, V7X_LANES), lambda s, i: (1 - s, 0, 0)),
        pl.BlockSpec((None, 1, V7X_LANES), lambda s, i: (1 - s, 0, 0)),
        _const_spec(dexp.shape), _const_spec(nw.shape),
        pl.BlockSpec((None, Q, 3 * Q), lambda s, i: (1 - s, 0, 0)),
        pl.BlockSpec((None, Q, Q), lambda s, i: (1 - s, 0, 0)),
        _const_spec(r3.shape), _const_spec(ecol.shape),
    ]
    est = (L * DS * 4 + 2 * SSD_GROUPS * SSD_STATE * DS * 2 + 4 * (Q + 16) * (DS + nbc) * 4 * 2
           + 2 * (r3.size + ecol.size) * 2 + 2 * Q * DS * 4 * 2 + 16 * Q * DS * 4)
    return pl.pallas_call(
        _ssd_kernel,
        grid=(2, nc),
        in_specs=in_specs,
        out_specs=pl.BlockSpec((Q, DS), lambda s, i: (s * i, 0)),
        out_shape=jax.ShapeDtypeStruct((L, DS), BF16),
        scratch_shapes=[
            pltpu.VMEM((L, DS), F32),
            pltpu.VMEM((SSD_GROUPS, SSD_STATE, DS // SSD_GROUPS), F32),
            pltpu.VMEM((Q + 16, DS), F32),
            pltpu.VMEM((Q + 16, nbc), F32),
        ],
        compiler_params=pltpu.CompilerParams(dimension_semantics=("arbitrary", "arbitrary"),
                                             vmem_limit_bytes=_vmem_limit(est)),
        name="ssd",
    )(xs_raw, xs_raw, xs_raw, bc_raw, bc_raw, bc_raw, dt_raw, z, cwx, cbx, cwb, cbb, dtb, alog, dexp, nw,
      tcat, negmask, r3, ecol)


def _mixout_kernel(up_ref, um_ref, un_ref, ys_ref, x_ref, pw_ref, ps_ref, wo_ref, g_ref, o_ref, ext_ref, *, seq):
    i = pl.program_id(0)
    n = pl.num_programs(0)
    tm = um_ref.shape[0]
    cg = um_ref.shape[1] // len(POOL_WINDOWS)
    ext_ref[0:8, :] = jnp.where(i > 0, up_ref[...], 0.0)
    ext_ref[8:8 + tm, :] = um_ref[...]
    ext_ref[8 + tm:16 + tm, :] = jnp.where(i < n - 1, un_ref[...], 0.0)
    t = i * tm + lax.broadcasted_iota(jnp.int32, (tm, cg), 0)
    pooled = []
    for gi, k in enumerate(POOL_WINDOWS):
        cols = slice(gi * cg, (gi + 1) * cg)
        lo_off = 8 - k // 2
        acc = ext_ref[lo_off:lo_off + tm, cols]
        for j in range(1, k):
            acc = acc + ext_ref[lo_off + j:lo_off + j + tm, cols]
        cnt = (jnp.minimum(t + (k - k // 2), seq) - jnp.maximum(t - k // 2, 0)).astype(F32)
        mixed = acc / cnt - um_ref[:, cols]
        yp = jnp.dot(mixed.astype(BF16), pw_ref[gi], preferred_element_type=F32) * ps_ref[:, cols]
        pooled.append(yp.astype(BF16))
    ypool = jnp.concatenate(pooled, axis=1)
    ds = ys_ref.shape[1]
    mix = jnp.dot(ys_ref[...], wo_ref[0:ds, :], preferred_element_type=F32)
    mix = mix + jnp.dot(ypool, wo_ref[ds:, :], preferred_element_type=F32)
    o_ref[...] = x_ref[...] + _rms(mix) * g_ref[...]


def _mixout(u, y_ssd, x, pool_w, pool_scale, w_out, g, *, tm):
    L, DP = u.shape
    D = x.shape[1]
    hb = tm // V7X_SUBLANES
    nhb = L // V7X_SUBLANES
    row = lambda n_: pl.BlockSpec((tm, n_), lambda i: (i, 0))
    in_specs = [
        pl.BlockSpec((V7X_SUBLANES, DP), lambda i: (jnp.maximum(i * hb - 1, 0), 0)),
        row(DP),
        pl.BlockSpec((V7X_SUBLANES, DP), lambda i: (jnp.minimum(i * hb + hb, nhb - 1), 0)),
        row(y_ssd.shape[1]), row(D),
        _const_spec(pool_w.shape), _const_spec(pool_scale.shape), _const_spec(w_out.shape), _const_spec(g.shape),
    ]
    est = (2 * tm * (DP * 4 + y_ssd.shape[1] * 2 + D * 4 + D * 4) + 2 * (w_out.size + pool_w.size) * 2
           + (tm + 16) * DP * 4 + 6 * tm * D * 4)
    return pl.pallas_call(
        functools.partial(_mixout_kernel, seq=L),
        grid=(L // tm,),
        in_specs=in_specs,
        out_specs=row(D),
        out_shape=jax.ShapeDtypeStruct((L, D), F32),
        scratch_shapes=[pltpu.VMEM((tm + 16, DP), F32)],
        compiler_params=pltpu.CompilerParams(dimension_semantics=("arbitrary",), vmem_limit_bytes=_vmem_limit(est)),
        name="mixout",
    )(u, u, u, y_ssd, x, pool_w, pool_scale, w_out, g)


FFN_HALO = 16


def _ffn_kernel(hp_ref, hm_ref, hx_ref, gpre_ref, wg_ref, wv_ref, cw_ref, cb_ref, wd_ref, gpost_ref,
                o_ref, hn_ref, gate_ref, acc_ref):
    i = pl.program_id(0)
    f = pl.program_id(1)
    n = pl.num_programs(0)
    nf = pl.num_programs(1)
    tm = hm_ref.shape[0]
    H = FFN_HALO

    @pl.when(f == 0)
    def _():
        g = gpre_ref[...]
        hn_ref[0:H, :] = jnp.where(i > 0, _rms(hp_ref[...]) * g, 0.0).astype(BF16)
        hn_ref[H:H + tm, :] = (_rms(hm_ref[...]) * g).astype(BF16)
        hn_ref[H + tm:2 * H + tm, :] = jnp.where(i < n - 1, _rms(hx_ref[...]) * g, 0.0).astype(BF16)
        acc_ref[...] = jnp.zeros_like(acc_ref)

    gate_ref[...] = jnp.dot(hn_ref[...], wg_ref[...], preferred_element_type=F32)
    val = jnp.dot(hn_ref[H:H + tm, :], wv_ref[...], preferred_element_type=F32)
    half = FFN_CONV // 2
    gc = cw_ref[0:1, :] * gate_ref[H - half:H - half + tm, :]
    for j in range(1, FFN_CONV):
        gc = gc + cw_ref[j:j + 1, :] * gate_ref[H - half + j:H - half + j + tm, :]
    gc = gc + cb_ref[...]
    act = (jax.nn.gelu(gc, approximate=True) * val).astype(BF16)
    acc_ref[...] += jnp.dot(act, wd_ref[...], preferred_element_type=F32)

    @pl.when(f == nf - 1)
    def _():
        o_ref[...] = hm_ref[...] + _rms(acc_ref[...]) * gpost_ref[...]


def _ffn(h, gpre, w_up, cw, cb, w_down, gpost, *, tm, tf):
    L, D = h.shape
    DF = w_down.shape[0]
    nf = DF // tf
    H = FFN_HALO
    hb = tm // H
    nhb = L // H
    in_specs = [
        pl.BlockSpec((H, D), lambda i, f: (jnp.maximum(i * hb - 1, 0), 0)),
        pl.BlockSpec((tm, D), lambda i, f: (i, 0)),
        pl.BlockSpec((H, D), lambda i, f: (jnp.minimum(i * hb + hb, nhb - 1), 0)),
        pl.BlockSpec(gpre.shape, lambda i, f: (0, 0)),
        pl.BlockSpec((D, tf), lambda i, f: (0, f)),
        pl.BlockSpec((D, tf), lambda i, f: (0, nf + f)),
        pl.BlockSpec((FFN_CONV, tf), lambda i, f: (0, f)),
        pl.BlockSpec((1, tf), lambda i, f: (0, f)),
        pl.BlockSpec((tf, D), lambda i, f: (f, 0)),
        pl.BlockSpec(gpost.shape, lambda i, f: (0, 0)),
    ]
    est = (2 * tm * D * 4 * 2 + (tm + 2 * H) * D * 2 + tm * D * 4 + (tm + 2 * H) * tf * 4
           + 2 * 3 * D * tf * 2 + 6 * tm * tf * 4 + 2 * tm * D * 4)
    return pl.pallas_call(
        _ffn_kernel,
        grid=(L // tm, nf),
        in_specs=in_specs,
        out_specs=pl.BlockSpec((tm, D), lambda i, f: (i, 0)),
        out_shape=jax.ShapeDtypeStruct((L, D), F32),
        scratch_shapes=[
            pltpu.VMEM((tm + 2 * H, D), BF16),
            pltpu.VMEM((tm + 2 * H, tf), F32),
            pltpu.VMEM((tm, D), F32),
        ],
        compiler_params=pltpu.CompilerParams(dimension_semantics=("arbitrary", "arbitrary"),
                                             vmem_limit_bytes=_vmem_limit(est)),
        name="ffn",
    )(h, h, h, gpre, w_up, w_up, cw, cb, w_down, gpost)


def _ple_kernel(h_ref, p_ref, gpre_ref, wg_ref, wp_ref, gpost_ref, o_ref):
    h = h_ref[...]
    hn = (_rms(h) * gpre_ref[...]).astype(BF16)
    gate = jax.nn.sigmoid(jnp.dot(hn, wg_ref[...], preferred_element_type=F32))
    pe = jnp.dot(p_ref[...].astype(BF16), wp_ref[...], preferred_element_type=F32)
    o_ref[...] = h + _rms(gate * pe) * gpost_ref[...]


def _ple(h, p, gpre, w_gate, w_ple, gpost, *, tm):
    L, D = h.shape
    row = lambda n: pl.BlockSpec((tm, n), lambda i: (i, 0))
    est = 2 * tm * (2 * D + p.shape[1]) * 4 + 2 * (w_gate.size + w_ple.size) * 2 + 8 * tm * D * 4
    return pl.pallas_call(
        _ple_kernel,
        grid=(L // tm,),
        in_specs=[row(D), row(p.shape[1]), _const_spec(gpre.shape), _const_spec(w_gate.shape),
                  _const_spec(w_ple.shape), _const_spec(gpost.shape)],
        out_specs=row(D),
        out_shape=jax.ShapeDtypeStruct((L, D), F32),
        compiler_params=pltpu.CompilerParams(dimension_semantics=("arbitrary",), vmem_limit_bytes=_vmem_limit(est)),
        name="ple",
    )(h, p, gpre, w_gate, w_ple, gpost)


def _ssd_constants():
    Q = SSD_CHUNK
    r = jnp.arange(Q)
    lower = (r[:, None] >= r[None, :])
    tri = jnp.stack([lower, lower.T]).astype(BF16)
    tcat = jnp.concatenate([tri, tri, tri], axis=2)
    negmask = jnp.where(jnp.stack([lower, lower.T]), 0.0, NEG_BIG).astype(F32)
    k = jnp.arange(3 * V7X_LANES) % V7X_LANES
    r3 = (k[:, None] == (jnp.arange(SSD_HEADS * SSD_HEAD_DIM) // SSD_HEAD_DIM)[None, :]).astype(BF16)
    ecol = (k[:, None] == (jnp.arange(SSD_HEADS * Q) // Q)[None, :]).astype(BF16)
    return tcat, negmask, r3, ecol


def _pad_lanes(v, fill):
    out = jnp.full((v.shape[0], 1, V7X_LANES), fill, F32)
    return out.at[:, 0, :v.shape[1]].set(v.astype(F32))


def kernel(x, p, mix_norm_pre, mix_norm_post, w_in, ssd_conv_w, ssd_conv_b, ssd_dt_bias, ssd_a_log, ssd_d,
           ssd_norm, pool_w, pool_scale, w_out, ffn_norm_pre, ffn_norm_post, w_ffn_up, ffn_conv_w, ffn_conv_b,
           w_ffn_down, ple_norm_pre, w_ple_gate, w_ple, ple_norm_post):
    B, L, D = x.shape
    depth = w_in.shape[0]
    d_ssd = SSD_HEADS * SSD_HEAD_DIM
    n_bc = 2 * SSD_GROUPS * SSD_STATE
    o_xs = d_ssd
    o_bc = o_xs + d_ssd
    o_dt = o_bc + n_bc
    o_u = o_dt + 2 * SSD_HEADS
    tcat, negmask, r3, ecol = _ssd_constants()
    row = lambda v: v.reshape(1, -1).astype(F32)

    outs = []
    for b in range(B):
        h = x[b]
        for i in range(depth):
            wi = w_in[i]
            wdt = jnp.zeros((D, 2 * V7X_LANES), F32)
            wdt = wdt.at[:, 0:SSD_HEADS].set(wi[:, o_dt:o_dt + SSD_HEADS])
            wdt = wdt.at[:, V7X_LANES:V7X_LANES + SSD_HEADS].set(wi[:, o_dt + SSD_HEADS:o_u])
            z, xs_raw, bc_raw, u, dt_raw = _inproj(
                h, row(mix_norm_pre[i]), wi[:, 0:o_xs].astype(BF16), wi[:, o_xs:o_bc].astype(BF16),
                wi[:, o_bc:o_dt].astype(BF16), wi[:, o_u:].astype(BF16), wdt.astype(BF16), tm=512)
            cw = ssd_conv_w[i].astype(F32)
            cb = ssd_conv_b[i].astype(F32)
            y_ssd = _ssd(
                xs_raw, bc_raw, dt_raw, z,
                cw[:, 0:d_ssd], cb[0:d_ssd].reshape(1, -1), cw[:, d_ssd:], cb[d_ssd:].reshape(1, -1),
                _pad_lanes(ssd_dt_bias[i], 0.0), _pad_lanes(ssd_a_log[i], NEG_BIG),
                jnp.repeat(ssd_d[i].astype(F32), SSD_HEAD_DIM).reshape(1, -1), row(ssd_norm[i]),
                tcat, negmask, r3, ecol)
            h = _mixout(u, y_ssd, h, pool_w[i].astype(BF16), row(pool_scale[i]), w_out[i].astype(BF16),
                        row(mix_norm_post[i]), tm=512)
            h = _ffn(h, row(ffn_norm_pre[i]), w_ffn_up[i].astype(BF16), ffn_conv_w[i].astype(F32),
                     row(ffn_conv_b[i]), w_ffn_down[i].astype(BF16), row(ffn_norm_post[i]), tm=512, tf=512)
            h = _ple(h, p[i, b], row(ple_norm_pre[i]), w_ple_gate[i].astype(BF16), w_ple[i].astype(BF16),
                     row(ple_norm_post[i]), tm=512)
        outs.append(h)
    return jnp.stack(outs)
```

```python
import functools

import jax
import jax.numpy as jnp
from jax import lax
from jax.experimental import pallas as pl
from jax.experimental.pallas import tpu as pltpu

F32 = jnp.float32
BF16 = jnp.bfloat16
EPS = 1e-6

V7X_VMEM_BYTES = 64 * 1024 * 1024
V7X_LANES = 128
V7X_SUBLANES = 8
BF16_ROWS = 16

SSD_HEAD_DIM = 64
SSD_HEADS = 16
SSD_GROUPS = 2
SSD_HEADS_PER_GROUP = SSD_HEADS // SSD_GROUPS
SSD_STATE = 128
SSD_CONV = 5
SSD_CHUNK = 128
POOL_WINDOWS = (2, 4, 8, 16)
FFN_CONV = 3
NEG_BIG = -1e30


def _vmem_limit(nbytes):
    return int(min(nbytes, V7X_VMEM_BYTES - 6 * 1024 * 1024))


def _rms(x):
    return x * lax.rsqrt(jnp.mean(x * x, axis=-1, keepdims=True) + EPS)


def _split3(x):
    hi = x.astype(BF16).astype(F32)
    r1 = x - hi
    mid = r1.astype(BF16).astype(F32)
    lo = (r1 - mid).astype(BF16).astype(F32)
    return hi, mid, lo


def _const_spec(shape):
    nd = len(shape)
    return pl.BlockSpec(shape, lambda *_: (0,) * nd, pipeline_mode=pl.Buffered(1))


CONV_PIECE = 256


def _inproj_kernel(xp_ref, xm_ref, xn_ref, g_ref, wz_ref, wu_ref, wdt_ref, wxs_ref, wbc_ref,
                   cwx_ref, cbx_ref, cwb_ref, cbb_ref,
                   z_ref, u_ref, dt_ref, xs_ref, bc_ref, hn_ref, *ext_refs):
    i = pl.program_id(0)
    n = pl.num_programs(0)
    tm = xm_ref.shape[0]
    H = BF16_ROWS
    g = g_ref[...]
    hn_ref[0:H, :] = jnp.where(i > 0, _rms(xp_ref[...]) * g, 0.0).astype(BF16)
    hn_ref[H:H + tm, :] = (_rms(xm_ref[...]) * g).astype(BF16)
    hn_ref[H + tm:2 * H + tm, :] = jnp.where(i < n - 1, _rms(xn_ref[...]) * g, 0.0).astype(BF16)
    cs = CONV_PIECE
    pieces = [(wxs_ref, cwx_ref, cbx_ref, xs_ref, k * cs) for k in range(wxs_ref.shape[1] // cs)]
    pieces += [(wbc_ref, cwb_ref, cbb_ref, bc_ref, k * cs) for k in range(wbc_ref.shape[1] // cs)]
    plain = [(w_ref, o_ref, k * cs) for w_ref, o_ref in ((wz_ref, z_ref), (wu_ref, u_ref), (wdt_ref, dt_ref))
             for k in range(w_ref.shape[1] // cs)]

    def plain_dot(w_ref, o_ref, c0):
        o_ref[:, c0:c0 + cs] = jnp.dot(hn_ref[H:H + tm, :], w_ref[:, c0:c0 + cs],
                                       preferred_element_type=F32).astype(o_ref.dtype)

    half = SSD_CONV // 2
    for k, (w_ref, cw_ref, cb_ref, o_ref, c0) in enumerate(pieces):
        ext_ref = ext_refs[k % len(ext_refs)]
        ext_ref[...] = jnp.dot(hn_ref[...], w_ref[:, c0:c0 + cs], preferred_element_type=F32)
        if plain:
            plain_dot(*plain.pop(0))
        acc = cw_ref[0:1, c0:c0 + cs] * ext_ref[H - half:H - half + tm, :]
        for j in range(1, SSD_CONV):
            acc = acc + cw_ref[j:j + 1, c0:c0 + cs] * ext_ref[H - half + j:H - half + j + tm, :]
        acc = acc + cb_ref[:, c0:c0 + cs]
        o_ref[:, c0:c0 + cs] = (acc * jax.nn.sigmoid(acc)).astype(o_ref.dtype)
    for args in plain:
        plain_dot(*args)


def _inproj(x, g, wz, wu, wdt, wxs, wbc, cwx, cbx, cwb, cbb, *, tm):
    L, D = x.shape
    H = BF16_ROWS
    hb = tm // H
    nhb = L // H
    outs = [(wz.shape[1], F32), (wu.shape[1], F32), (wdt.shape[1], F32), (wxs.shape[1], BF16), (wbc.shape[1], BF16)]
    row = lambda n: pl.BlockSpec((tm, n), lambda i: (i, 0))
    consts = (g, wz, wu, wdt, wxs, wbc, cwx, cbx, cwb, cbb)
    ncols = sum(n for n, _ in outs)
    est = (2 * (tm + 2 * H) * D * 4 + D * ncols * 2 + 2 * tm * ncols * 4 + (tm + 2 * H) * D * 2
           + 12 * (tm + 2 * H) * wbc.shape[1] * 4 + 2 * tm * 1024 * 4)
    return pl.pallas_call(
        _inproj_kernel,
        grid=(L // tm,),
        in_specs=[pl.BlockSpec((H, D), lambda i: (jnp.maximum(i * hb - 1, 0), 0)), row(D),
                  pl.BlockSpec((H, D), lambda i: (jnp.minimum(i * hb + hb, nhb - 1), 0))]
                 + [_const_spec(c.shape) for c in consts],
        out_specs=[row(n) for n, _ in outs],
        out_shape=[jax.ShapeDtypeStruct((L, n), dt) for n, dt in outs],
        scratch_shapes=[pltpu.VMEM((tm + 2 * H, D), BF16)]
                       + [pltpu.VMEM((tm + 2 * H, CONV_PIECE), F32) for _ in range(2)],
        compiler_params=pltpu.CompilerParams(dimension_semantics=("arbitrary",), vmem_limit_bytes=_vmem_limit(est)),
        name="inproj",
    )(x, x, x, *consts)


SSD_PACK = 16


def _pack3(v, head_lanes):
    hi, mid, lo = _split3(v)
    zero = jnp.zeros_like(v)
    packed = (jnp.where(head_lanes, hi, zero)
              + pltpu.roll(jnp.where(head_lanes, mid, zero), SSD_PACK, axis=1)
              + pltpu.roll(jnp.where(head_lanes, lo, zero), 2 * SSD_PACK, axis=1))
    return packed.astype(BF16)


def _ssd_kernel(xs_ref, bc_ref, dt_ref, z_ref, dtb_ref, alog_ref, dexp_ref, nw_ref,
                t_ref, nm_ref, rexp_ref, ecol_ref,
                o_ref,
                yb_ref, h_ref):
    Q = SSD_CHUNK
    P = SSD_HEAD_DIM
    E = SSD_HEADS_PER_GROUP
    GW = E * P
    s = pl.program_id(0)
    i = pl.program_id(1)
    nc = pl.num_programs(1)
    c = i + (1 - s) * (nc - 1 - 2 * i)
    is_fwd = s == 1

    @pl.when(i == 0)
    def _():
        h_ref[...] = jnp.zeros_like(h_ref)

    xs = xs_ref[...].astype(F32)
    bc = bc_ref[...]

    head_lanes = lax.broadcasted_iota(jnp.int32, (Q, V7X_LANES), 1) < SSD_HEADS
    dt = jax.nn.softplus(dt_ref[...] + dtb_ref[...])
    a = -jnp.exp(alog_ref[...])
    da = dt * a
    da3 = jnp.concatenate([v.astype(BF16) for v in _split3(da)], axis=0)
    acum = jnp.dot(t_ref[...], da3, preferred_element_type=F32)
    tot = jnp.where(is_fwd, acum[Q - 1:Q, :], acum[0:1, :])
    acum_t = acum.T
    e_a = jnp.exp(acum)
    w_end = jnp.exp(tot - acum)

    lhs = jnp.concatenate([_pack3(v, head_lanes) for v in (dt, dt * w_end, e_a)], axis=0)
    exp3 = jnp.dot(lhs, rexp_ref[...], preferred_element_type=F32)
    dt_x = exp3[0:Q]
    dtw_x = exp3[Q:2 * Q]
    ea_x = exp3[2 * Q:3 * Q]
    cd_x = jnp.where(is_fwd, ea_x[Q - 1:Q, :], ea_x[0:1, :])
    acol = jnp.dot(_pack3(acum, head_lanes), ecol_ref[...], preferred_element_type=F32)

    xdt = (xs * dt_x).astype(BF16)
    xw = (xs * dtw_x).astype(BF16)
    negmask = nm_ref[...]
    first_head = lax.broadcasted_iota(jnp.int32, (Q, 2 * P), 1) < P

    y_groups = []
    for g in range(SSD_GROUPS):
        bm = bc[:, g * SSD_STATE:(g + 1) * SSD_STATE]
        cm = bc[:, (SSD_GROUPS + g) * SSD_STATE:(SSD_GROUPS + g + 1) * SSD_STATE]
        cb = lax.dot_general(cm, bm, (((1,), (1,)), ((), ())), preferred_element_type=F32)
        h_in = h_ref[g]
        y_off = jnp.dot(cm, h_in.astype(BF16), preferred_element_type=F32)
        y_g = y_off * ea_x[:, g * GW:(g + 1) * GW]
        st = lax.dot_general(bm, xw[:, g * GW:(g + 1) * GW], (((0,), (0,)), ((), ())),
                             preferred_element_type=F32)
        h_ref[g] = h_in * cd_x[:, g * GW:(g + 1) * GW] + st
        pairs = []
        for hp in range(E // 2):
            ms = []
            for k in range(2):
                h = g * E + hp * 2 + k
                seg = acol[:, h * Q:(h + 1) * Q] - acum_t[h:h + 1, :] + negmask
                ms.append((cb * jnp.exp(seg)).astype(BF16))
            m2 = jnp.concatenate(ms, axis=1)
            xp = xdt[:, (g * E + hp * 2) * P:(g * E + hp * 2 + 2) * P]
            zero = jnp.zeros_like(xp)
            rhs = jnp.concatenate([jnp.where(first_head, xp, zero), jnp.where(first_head, zero, xp)], axis=0)
            pairs.append(jnp.dot(m2, rhs, preferred_element_type=F32))
        y_groups.append(y_g + jnp.concatenate(pairs, axis=1))
    y = jnp.concatenate(y_groups, axis=1)

    row0 = pl.multiple_of(c * Q, Q)

    @pl.when(s == 0)
    def _():
        yb_ref[pl.ds(row0, Q), :] = y

    @pl.when(s == 1)
    def _():
        yy = y + yb_ref[pl.ds(row0, Q), :] + xs * dexp_ref[...]
        z = z_ref[...]
        yy = yy * (z * jax.nn.sigmoid(z))
        outs = []
        for g in range(SSD_GROUPS):
            outs.append(_rms(yy[:, g * GW:(g + 1) * GW]))
        o_ref[...] = (jnp.concatenate(outs, axis=1) * nw_ref[...]).astype(o_ref.dtype)


def _ssd(xs, bc, dt_raw, z, dtb, alog, dexp, nw, tcat, negmask, rexp, ecol):
    L, DS = xs.shape
    Q = SSD_CHUNK
    nc = L // Q

    def cidx(s, i):
        return i + (1 - s) * (nc - 1 - 2 * i)

    in_specs = [
        pl.BlockSpec((Q, DS), lambda s, i: (cidx(s, i), 0)),
        pl.BlockSpec((Q, bc.shape[1]), lambda s, i: (cidx(s, i), 0)),
        pl.BlockSpec((Q, V7X_LANES), lambda s, i: (cidx(s, i), 1 - s)),
        pl.BlockSpec((Q, DS), lambda s, i: (s * i, 0)),
        pl.BlockSpec((None, 1, V7X_LANES), lambda s, i: (1 - s, 0, 0)),
        pl.BlockSpec((None, 1, V7X_LANES), lambda s, i: (1 - s, 0, 0)),
        _const_spec(dexp.shape), _const_spec(nw.shape),
        pl.BlockSpec((None, Q, 3 * Q), lambda s, i: (1 - s, 0, 0)),
        pl.BlockSpec((None, Q, Q), lambda s, i: (1 - s, 0, 0)),
        _const_spec(rexp.shape), _const_spec(ecol.shape),
    ]
    est = (L * DS * 4 + 2 * SSD_GROUPS * SSD_STATE * DS * 2 + 4 * Q * (DS + bc.shape[1]) * 4
           + 2 * (rexp.size + ecol.size) * 2 + 2 * Q * DS * 4 * 2 + 24 * Q * DS * 4)
    return pl.pallas_call(
        _ssd_kernel,
        grid=(2, nc),
        in_specs=in_specs,
        out_specs=pl.BlockSpec((Q, DS), lambda s, i: (s * i, 0)),
        out_shape=jax.ShapeDtypeStruct((L, DS), BF16),
        scratch_shapes=[
            pltpu.VMEM((L, DS), F32),
            pltpu.VMEM((SSD_GROUPS, SSD_STATE, DS // SSD_GROUPS), F32),
        ],
        compiler_params=pltpu.CompilerParams(dimension_semantics=("arbitrary", "arbitrary"),
                                             vmem_limit_bytes=_vmem_limit(est)),
        name="ssd",
    )(xs, bc, dt_raw, z, dtb, alog, dexp, nw, tcat, negmask, rexp, ecol)


def _mixout_kernel(up_ref, um_ref, un_ref, ys_ref, x_ref, pw_ref, ps_ref, wo_ref, g_ref, o_ref, ext_ref, *, seq):
    i = pl.program_id(0)
    n = pl.num_programs(0)
    tm = um_ref.shape[0]
    cg = um_ref.shape[1] // len(POOL_WINDOWS)
    ext_ref[0:8, :] = jnp.where(i > 0, up_ref[...], 0.0)
    ext_ref[8:8 + tm, :] = um_ref[...]
    ext_ref[8 + tm:16 + tm, :] = jnp.where(i < n - 1, un_ref[...], 0.0)
    t = i * tm + lax.broadcasted_iota(jnp.int32, (tm, cg), 0)
    pooled = []
    for gi, k in enumerate(POOL_WINDOWS):
        cols = slice(gi * cg, (gi + 1) * cg)
        lo_off = 8 - k // 2
        acc = ext_ref[lo_off:lo_off + tm, cols]
        for j in range(1, k):
            acc = acc + ext_ref[lo_off + j:lo_off + j + tm, cols]
        cnt = (jnp.minimum(t + (k - k // 2), seq) - jnp.maximum(t - k // 2, 0)).astype(F32)
        mixed = acc / cnt - um_ref[:, cols]
        yp = jnp.dot(mixed.astype(BF16), pw_ref[gi], preferred_element_type=F32) * ps_ref[:, cols]
        pooled.append(yp.astype(BF16))
    ypool = jnp.concatenate(pooled, axis=1)
    ds = ys_ref.shape[1]
    mix = jnp.dot(ys_ref[...], wo_ref[0:ds, :], preferred_element_type=F32)
    mix = mix + jnp.dot(ypool, wo_ref[ds:, :], preferred_element_type=F32)
    o_ref[...] = x_ref[...] + _rms(mix) * g_ref[...]


def _mixout(u, y_ssd, x, pool_w, pool_scale, w_out, g, *, tm):
    L, DP = u.shape
    D = x.shape[1]
    hb = tm // V7X_SUBLANES
    nhb = L // V7X_SUBLANES
    row = lambda n_: pl.BlockSpec((tm, n_), lambda i: (i, 0))
    in_specs = [
        pl.BlockSpec((V7X_SUBLANES, DP), lambda i: (jnp.maximum(i * hb - 1, 0), 0)),
        row(DP),
        pl.BlockSpec((V7X_SUBLANES, DP), lambda i: (jnp.minimum(i * hb + hb, nhb - 1), 0)),
        row(y_ssd.shape[1]), row(D),
        _const_spec(pool_w.shape), _const_spec(pool_scale.shape), _const_spec(w_out.shape), _const_spec(g.shape),
    ]
    est = (2 * tm * (DP * 4 + y_ssd.shape[1] * 2 + D * 4 + D * 4) + 2 * (w_out.size + pool_w.size) * 2
           + (tm + 16) * DP * 4 + 6 * tm * D * 4)
    return pl.pallas_call(
        functools.partial(_mixout_kernel, seq=L),
        grid=(L // tm,),
        in_specs=in_specs,
        out_specs=row(D),
        out_shape=jax.ShapeDtypeStruct((L, D), F32),
        scratch_shapes=[pltpu.VMEM((tm + 16, DP), F32)],
        compiler_params=pltpu.CompilerParams(dimension_semantics=("arbitrary",), vmem_limit_bytes=_vmem_limit(est)),
        name="mixout",
    )(u, u, u, y_ssd, x, pool_w, pool_scale, w_out, g)


def _ffn_kernel(hp_ref, hm_ref, hx_ref, gpre_ref, wg_ref, wv_ref, cw_ref, cb_ref, wd_ref, gpost_ref,
                o_ref, hn_ref, gate_ref, acc_ref):
    i = pl.program_id(0)
    f = pl.program_id(1)
    n = pl.num_programs(0)
    nf = pl.num_programs(1)
    tm = hm_ref.shape[0]
    H = BF16_ROWS

    @pl.when(f == 0)
    def _():
        g = gpre_ref[...]
        hn_ref[0:H, :] = jnp.where(i > 0, _rms(hp_ref[...]) * g, 0.0).astype(BF16)
        hn_ref[H:H + tm, :] = (_rms(hm_ref[...]) * g).astype(BF16)
        hn_ref[H + tm:2 * H + tm, :] = jnp.where(i < n - 1, _rms(hx_ref[...]) * g, 0.0).astype(BF16)
        acc_ref[...] = jnp.zeros_like(acc_ref)

    gate_ref[...] = jnp.dot(hn_ref[...], wg_ref[...], preferred_element_type=F32)
    val = jnp.dot(hn_ref[H:H + tm, :], wv_ref[...], preferred_element_type=F32)
    half = FFN_CONV // 2
    gc = cw_ref[0:1, :] * gate_ref[H - half:H - half + tm, :]
    for j in range(1, FFN_CONV):
        gc = gc + cw_ref[j:j + 1, :] * gate_ref[H - half + j:H - half + j + tm, :]
    gc = gc + cb_ref[...]
    act = (jax.nn.gelu(gc, approximate=True) * val).astype(BF16)
    acc_ref[...] += jnp.dot(act, wd_ref[...], preferred_element_type=F32)

    @pl.when(f == nf - 1)
    def _():
        o_ref[...] = hm_ref[...] + _rms(acc_ref[...]) * gpost_ref[...]


def _ffn(h, gpre, w_up, cw, cb, w_down, gpost, *, tm, tf):
    L, D = h.shape
    DF = w_down.shape[0]
    nf = DF // tf
    H = BF16_ROWS
    hb = tm // H
    nhb = L // H
    in_specs = [
        pl.BlockSpec((H, D), lambda i, f: (jnp.maximum(i * hb - 1, 0), 0)),
        pl.BlockSpec((tm, D), lambda i, f: (i, 0)),
        pl.BlockSpec((H, D), lambda i, f: (jnp.minimum(i * hb + hb, nhb - 1), 0)),
        pl.BlockSpec(gpre.shape, lambda i, f: (0, 0)),
        pl.BlockSpec((D, tf), lambda i, f: (0, f)),
        pl.BlockSpec((D, tf), lambda i, f: (0, nf + f)),
        pl.BlockSpec((FFN_CONV, tf), lambda i, f: (0, f)),
        pl.BlockSpec((1, tf), lambda i, f: (0, f)),
        pl.BlockSpec((tf, D), lambda i, f: (f, 0)),
        pl.BlockSpec(gpost.shape, lambda i, f: (0, 0)),
    ]
    est = (2 * tm * D * 4 * 2 + (tm + 2 * H) * D * 2 + tm * D * 4 + (tm + 2 * H) * tf * 4
           + 2 * 3 * D * tf * 2 + 6 * tm * tf * 4 + 2 * tm * D * 4)
    return pl.pallas_call(
        _ffn_kernel,
        grid=(L // tm, nf),
        in_specs=in_specs,
        out_specs=pl.BlockSpec((tm, D), lambda i, f: (i, 0)),
        out_shape=jax.ShapeDtypeStruct((L, D), F32),
        scratch_shapes=[
            pltpu.VMEM((tm + 2 * H, D), BF16),
            pltpu.VMEM((tm + 2 * H, tf), F32),
            pltpu.VMEM((tm, D), F32),
        ],
        compiler_params=pltpu.CompilerParams(dimension_semantics=("arbitrary", "arbitrary"),
                                             vmem_limit_bytes=_vmem_limit(est)),
        name="ffn",
    )(h, h, h, gpre, w_up, w_up, cw, cb, w_down, gpost)


def _ple_kernel(h_ref, p_ref, gpre_ref, wg_ref, wp_ref, gpost_ref, o_ref):
    h = h_ref[...]
    hn = (_rms(h) * gpre_ref[...]).astype(BF16)
    gate = jax.nn.sigmoid(jnp.dot(hn, wg_ref[...], preferred_element_type=F32))
    pe = jnp.dot(p_ref[...].astype(BF16), wp_ref[...], preferred_element_type=F32)
    o_ref[...] = h + _rms(gate * pe) * gpost_ref[...]


def _ple(h, p, gpre, w_gate, w_ple, gpost, *, tm):
    L, D = h.shape
    row = lambda n: pl.BlockSpec((tm, n), lambda i: (i, 0))
    est = 2 * tm * (2 * D + p.shape[1]) * 4 + 2 * (w_gate.size + w_ple.size) * 2 + 8 * tm * D * 4
    return pl.pallas_call(
        _ple_kernel,
        grid=(L // tm,),
        in_specs=[row(D), row(p.shape[1]), _const_spec(gpre.shape), _const_spec(w_gate.shape),
                  _const_spec(w_ple.shape), _const_spec(gpost.shape)],
        out_specs=row(D),
        out_shape=jax.ShapeDtypeStruct((L, D), F32),
        compiler_params=pltpu.CompilerParams(dimension_semantics=("arbitrary",), vmem_limit_bytes=_vmem_limit(est)),
        name="ple",
    )(h, p, gpre, w_gate, w_ple, gpost)


def _ssd_constants():
    Q = SSD_CHUNK
    r = jnp.arange(Q)
    lower = (r[:, None] >= r[None, :])
    tri = jnp.stack([lower, lower.T]).astype(BF16)
    tcat = jnp.concatenate([tri, tri, tri], axis=2)
    negmask = jnp.where(jnp.stack([lower, lower.T]), 0.0, NEG_BIG).astype(F32)
    k = jnp.arange(V7X_LANES)
    head_of_lane = jnp.where(k < 3 * SSD_PACK, k % SSD_PACK, -1)
    rexp = (head_of_lane[:, None] == (jnp.arange(SSD_HEADS * SSD_HEAD_DIM) // SSD_HEAD_DIM)[None, :]).astype(BF16)
    ecol = (head_of_lane[:, None] == (jnp.arange(SSD_HEADS * Q) // Q)[None, :]).astype(BF16)
    return tcat, negmask, rexp, ecol


def _pad_lanes(v, fill):
    out = jnp.full((v.shape[0], 1, V7X_LANES), fill, F32)
    return out.at[:, 0, :v.shape[1]].set(v.astype(F32))


def kernel(x, p, mix_norm_pre, mix_norm_post, w_in, ssd_conv_w, ssd_conv_b, ssd_dt_bias, ssd_a_log, ssd_d,
           ssd_norm, pool_w, pool_scale, w_out, ffn_norm_pre, ffn_norm_post, w_ffn_up, ffn_conv_w, ffn_conv_b,
           w_ffn_down, ple_norm_pre, w_ple_gate, w_ple, ple_norm_post):
    B, L, D = x.shape
    depth = w_in.shape[0]
    d_ssd = SSD_HEADS * SSD_HEAD_DIM
    n_bc = 2 * SSD_GROUPS * SSD_STATE
    o_xs = d_ssd
    o_bc = o_xs + d_ssd
    o_dt = o_bc + n_bc
    o_u = o_dt + 2 * SSD_HEADS
    tcat, negmask, rexp, ecol = _ssd_constants()
    row = lambda v: v.reshape(1, -1).astype(F32)

    outs = []
    for b in range(B):
        h = x[b]
        for i in range(depth):
            wi = w_in[i]
            wdt = jnp.zeros((D, 2 * V7X_LANES), F32)
            wdt = wdt.at[:, 0:SSD_HEADS].set(wi[:, o_dt:o_dt + SSD_HEADS])
            wdt = wdt.at[:, V7X_LANES:V7X_LANES + SSD_HEADS].set(wi[:, o_dt + SSD_HEADS:o_u])
            cw = ssd_conv_w[i].astype(F32)
            cb = ssd_conv_b[i].astype(F32)
            z, u, dt_raw, xs, bc = _inproj(
                h, row(mix_norm_pre[i]), wi[:, 0:o_xs].astype(BF16), wi[:, o_u:].astype(BF16), wdt.astype(BF16),
                wi[:, o_xs:o_bc].astype(BF16), wi[:, o_bc:o_dt].astype(BF16),
                cw[:, 0:d_ssd], cb[0:d_ssd].reshape(1, -1), cw[:, d_ssd:], cb[d_ssd:].reshape(1, -1), tm=512)
            y_ssd = _ssd(
                xs, bc, dt_raw, z, _pad_lanes(ssd_dt_bias[i], 0.0), _pad_lanes(ssd_a_log[i], NEG_BIG),
                jnp.repeat(ssd_d[i].astype(F32), SSD_HEAD_DIM).reshape(1, -1), row(ssd_norm[i]),
                tcat, negmask, rexp, ecol)
            h = _mixout(u, y_ssd, h, pool_w[i].astype(BF16), row(pool_scale[i]), w_out[i].astype(BF16),
                        row(mix_norm_post[i]), tm=512)
            h = _ffn(h, row(ffn_norm_pre[i]), w_ffn_up[i].astype(BF16), ffn_conv_w[i].astype(F32),
                     row(ffn_conv_b[i]), w_ffn_down[i].astype(BF16), row(ffn_norm_post[i]), tm=512, tf=512)
            h = _ple(h, p[i, b], row(ple_norm_pre[i]), w_ple_gate[i].astype(BF16), w_ple[i].astype(BF16),
                     row(ple_norm_post[i]), tm=512)
        outs.append(h)
    return jnp.stack(outs)
```

```python
import functools

import jax
import jax.numpy as jnp
from jax import lax
from jax.experimental import pallas as pl
from jax.experimental.pallas import tpu as pltpu

F32 = jnp.float32
BF16 = jnp.bfloat16
EPS = 1e-6

V7X_VMEM_BYTES = 64 * 1024 * 1024
V7X_LANES = 128
V7X_SUBLANES = 8
BF16_ROWS = 16

SSD_HEAD_DIM = 64
SSD_HEADS = 16
SSD_GROUPS = 2
SSD_HEADS_PER_GROUP = SSD_HEADS // SSD_GROUPS
SSD_STATE = 128
SSD_CONV = 5
SSD_CHUNK = 128
POOL_WINDOWS = (2, 4, 8, 16)
FFN_CONV = 3
NEG_BIG = -1e30

INPROJ_ROWS = 512
MIXOUT_ROWS = 512
FFN_ROWS = 1024
FFN_COLS = 512
PLE_ROWS = 512
CONV_PIECE = 256


def _vmem_limit(nbytes):
    return int(min(nbytes, V7X_VMEM_BYTES - 6 * 1024 * 1024))


def _rms(x):
    return x * lax.rsqrt(jnp.mean(x * x, axis=-1, keepdims=True) + EPS)


def _split3(x):
    hi = x.astype(BF16).astype(F32)
    r1 = x - hi
    mid = r1.astype(BF16).astype(F32)
    lo = (r1 - mid).astype(BF16).astype(F32)
    return hi, mid, lo


def _const_spec(shape):
    nd = len(shape)
    return pl.BlockSpec(shape, lambda *_: (0,) * nd, pipeline_mode=pl.Buffered(1))


def _halo_specs(rows, halo, total_rows, ncols):
    hb = rows // halo
    nhb = total_rows // halo
    return [
        pl.BlockSpec((halo, ncols), lambda i, *_: (jnp.maximum(i * hb - 1, 0), 0)),
        pl.BlockSpec((rows, ncols), lambda i, *_: (i, 0)),
        pl.BlockSpec((halo, ncols), lambda i, *_: (jnp.minimum(i * hb + hb, nhb - 1), 0)),
    ]


def _inproj_kernel(xp_ref, xm_ref, xn_ref, g_ref, w_ref, cw_ref, cb_ref, wd_ref, wo_ref, wg_ref,
                   xs_ref, bc_ref, z_ref, u_ref, dt_ref, wd16_ref, wo16_ref, wg16_ref,
                   hn_ref, *ext_refs):
    i = pl.program_id(0)
    n = pl.num_programs(0)
    tm = xm_ref.shape[0]
    H = BF16_ROWS
    g = g_ref[...]
    hn_ref[0:H, :] = jnp.where(i > 0, _rms(xp_ref[...]) * g, 0.0).astype(BF16)
    hn_ref[H:H + tm, :] = (_rms(xm_ref[...]) * g).astype(BF16)
    hn_ref[H + tm:2 * H + tm, :] = jnp.where(i < n - 1, _rms(xn_ref[...]) * g, 0.0).astype(BF16)
    for src, dst in ((wd_ref, wd16_ref), (wo_ref, wo16_ref), (wg_ref, wg16_ref)):
        dst[...] = src[...].astype(BF16)

    cs = CONV_PIECE
    conv_outs = [(o_ref, k * cs) for o_ref in (xs_ref, bc_ref) for k in range(o_ref.shape[1] // cs)]
    plain_outs = [(o_ref, k * cs) for o_ref in (z_ref, u_ref, dt_ref) for k in range(o_ref.shape[1] // cs)]

    def plain_dot(k):
        o_ref, c0 = plain_outs[k]
        wc = (len(conv_outs) + k) * cs
        o_ref[:, c0:c0 + cs] = jnp.dot(hn_ref[H:H + tm, :], w_ref[:, wc:wc + cs],
                                       preferred_element_type=F32).astype(o_ref.dtype)

    half = SSD_CONV // 2
    for k, (o_ref, c0) in enumerate(conv_outs):
        wc = k * cs
        ext_ref = ext_refs[k % len(ext_refs)]
        ext_ref[...] = jnp.dot(hn_ref[...], w_ref[:, wc:wc + cs], preferred_element_type=F32)
        if k < len(plain_outs):
            plain_dot(k)
        acc = cw_ref[0:1, wc:wc + cs] * ext_ref[H - half:H - half + tm, :]
        for j in range(1, SSD_CONV):
            acc = acc + cw_ref[j:j + 1, wc:wc + cs] * ext_ref[H - half + j:H - half + j + tm, :]
        acc = acc + cb_ref[:, wc:wc + cs]
        o_ref[:, c0:c0 + cs] = (acc * jax.nn.sigmoid(acc)).astype(o_ref.dtype)
    for k in range(len(conv_outs), len(plain_outs)):
        plain_dot(k)


def _inproj(x, g, w, cw, cb, out_cols, w_down, w_out, w_gate, *, tm):
    L, D = x.shape
    H = BF16_ROWS
    n = L // tm
    dts = (BF16, BF16, F32, F32, F32)
    row = lambda c: pl.BlockSpec((tm, c), lambda i: (i, 0))
    slab = lambda a: pl.BlockSpec((a.shape[0] // n, a.shape[1]), lambda i: (i, 0))
    casts = (w_down, w_out, w_gate)
    est = (2 * (tm + 2 * H) * D * 4 + w.size * 2 + 2 * tm * sum(out_cols) * 4 + (tm + 2 * H) * D * 2
           + 4 * (tm + 2 * H) * CONV_PIECE * 4 + 2 * sum(a.size for a in casts) // n * 6 + 8 * tm * CONV_PIECE * 4)
    return pl.pallas_call(
        _inproj_kernel,
        grid=(n,),
        in_specs=_halo_specs(tm, H, L, D) + [_const_spec(a.shape) for a in (g, w, cw, cb)] + [slab(a) for a in casts],
        out_specs=[row(c) for c in out_cols] + [slab(a) for a in casts],
        out_shape=[jax.ShapeDtypeStruct((L, c), dt) for c, dt in zip(out_cols, dts)]
                  + [jax.ShapeDtypeStruct(a.shape, BF16) for a in casts],
        scratch_shapes=[pltpu.VMEM((tm + 2 * H, D), BF16)]
                       + [pltpu.VMEM((tm + 2 * H, CONV_PIECE), F32) for _ in range(2)],
        compiler_params=pltpu.CompilerParams(dimension_semantics=("arbitrary",), vmem_limit_bytes=_vmem_limit(est)),
        name="inproj",
    )(x, x, x, g, w, cw, cb, *casts)


SSD_PACK = 16


def _pack3(v, head_lanes):
    hi, mid, lo = _split3(v)
    zero = jnp.zeros_like(v)
    packed = (jnp.where(head_lanes, hi, zero)
              + pltpu.roll(jnp.where(head_lanes, mid, zero), SSD_PACK, axis=1)
              + pltpu.roll(jnp.where(head_lanes, lo, zero), 2 * SSD_PACK, axis=1))
    return packed.astype(BF16)


def _ssd_kernel(xs_ref, bc_ref, dt_ref, z_ref, dtb_ref, alog_ref, dexp_ref, nw_ref,
                t_ref, nm_ref, rexp_ref, ecol_ref, wup_ref,
                o_ref, wup16_ref,
                yb_ref, h_ref):
    Q = SSD_CHUNK
    P = SSD_HEAD_DIM
    E = SSD_HEADS_PER_GROUP
    GW = E * P
    s = pl.program_id(0)
    i = pl.program_id(1)
    nc = pl.num_programs(1)
    c = i + (1 - s) * (nc - 1 - 2 * i)
    is_fwd = s == 1

    wup16_ref[...] = wup_ref[...].astype(BF16)

    @pl.when(i == 0)
    def _():
        h_ref[...] = jnp.zeros_like(h_ref)

    xs = xs_ref[...].astype(F32)
    bc = bc_ref[...]

    head_lanes = lax.broadcasted_iota(jnp.int32, (Q, V7X_LANES), 1) < SSD_HEADS
    dt = jax.nn.softplus(dt_ref[...] + dtb_ref[...])
    a = -jnp.exp(alog_ref[...])
    da = dt * a
    da3 = jnp.concatenate([v.astype(BF16) for v in _split3(da)], axis=0)
    acum = jnp.dot(t_ref[...], da3, preferred_element_type=F32)
    tot = jnp.where(is_fwd, acum[Q - 1:Q, :], acum[0:1, :])
    acum_t = acum.T
    e_a = jnp.exp(acum)
    w_end = jnp.exp(tot - acum)

    lhs = jnp.concatenate([_pack3(v, head_lanes) for v in (dt, dt * w_end, e_a)], axis=0)
    exp3 = jnp.dot(lhs, rexp_ref[...], preferred_element_type=F32)
    dt_x = exp3[0:Q]
    dtw_x = exp3[Q:2 * Q]
    ea_x = exp3[2 * Q:3 * Q]
    cd_x = jnp.where(is_fwd, ea_x[Q - 1:Q, :], ea_x[0:1, :])
    acol = jnp.dot(_pack3(acum, head_lanes), ecol_ref[...], preferred_element_type=F32)

    xdt = (xs * dt_x).astype(BF16)
    xw = (xs * dtw_x).astype(BF16)
    negmask = nm_ref[...]
    first_head = lax.broadcasted_iota(jnp.int32, (Q, 2 * P), 1) < P

    y_groups = []
    for g in range(SSD_GROUPS):
        bm = bc[:, g * SSD_STATE:(g + 1) * SSD_STATE]
        cm = bc[:, (SSD_GROUPS + g) * SSD_STATE:(SSD_GROUPS + g + 1) * SSD_STATE]
        cb = lax.dot_general(cm, bm, (((1,), (1,)), ((), ())), preferred_element_type=F32)
        h_in = h_ref[g]
        y_off = jnp.dot(cm, h_in.astype(BF16), preferred_element_type=F32)
        y_g = y_off * ea_x[:, g * GW:(g + 1) * GW]
        st = lax.dot_general(bm, xw[:, g * GW:(g + 1) * GW], (((0,), (0,)), ((), ())),
                             preferred_element_type=F32)
        h_ref[g] = h_in * cd_x[:, g * GW:(g + 1) * GW] + st
        pairs = []
        for hp in range(E // 2):
            ms = []
            for k in range(2):
                h = g * E + hp * 2 + k
                seg = acol[:, h * Q:(h + 1) * Q] - acum_t[h:h + 1, :] + negmask
                ms.append((cb * jnp.exp(seg)).astype(BF16))
            m2 = jnp.concatenate(ms, axis=1)
            xp = xdt[:, (g * E + hp * 2) * P:(g * E + hp * 2 + 2) * P]
            zero = jnp.zeros_like(xp)
            rhs = jnp.concatenate([jnp.where(first_head, xp, zero), jnp.where(first_head, zero, xp)], axis=0)
            pairs.append(jnp.dot(m2, rhs, preferred_element_type=F32))
        y_groups.append(y_g + jnp.concatenate(pairs, axis=1))
    y = jnp.concatenate(y_groups, axis=1)

    row0 = pl.multiple_of(c * Q, Q)

    @pl.when(s == 0)
    def _():
        yb_ref[pl.ds(row0, Q), :] = y

    @pl.when(s == 1)
    def _():
        yy = y + yb_ref[pl.ds(row0, Q), :] + xs * dexp_ref[...]
        z = z_ref[...]
        yy = yy * (z * jax.nn.sigmoid(z))
        outs = []
        for g in range(SSD_GROUPS):
            outs.append(_rms(yy[:, g * GW:(g + 1) * GW]))
        o_ref[...] = (jnp.concatenate(outs, axis=1) * nw_ref[...]).astype(o_ref.dtype)


def _ssd(xs, bc, dt_raw, z, dtb, alog, dexp, nw, tcat, negmask, rexp, ecol, w_up):
    L, DS = xs.shape
    Q = SSD_CHUNK
    nc = L // Q
    slab_rows = w_up.shape[0] // (2 * nc)

    def cidx(s, i):
        return i + (1 - s) * (nc - 1 - 2 * i)

    slab = pl.BlockSpec((slab_rows, w_up.shape[1]), lambda s, i: (s * nc + i, 0))
    in_specs = [
        pl.BlockSpec((Q, DS), lambda s, i: (cidx(s, i), 0)),
        pl.BlockSpec((Q, bc.shape[1]), lambda s, i: (cidx(s, i), 0)),
        pl.BlockSpec((Q, V7X_LANES), lambda s, i: (cidx(s, i), 1 - s)),
        pl.BlockSpec((Q, DS), lambda s, i: (s * i, 0)),
        pl.BlockSpec((None, 1, V7X_LANES), lambda s, i: (1 - s, 0, 0)),
        pl.BlockSpec((None, 1, V7X_LANES), lambda s, i: (1 - s, 0, 0)),
        _const_spec(dexp.shape), _const_spec(nw.shape),
        pl.BlockSpec((None, Q, 3 * Q), lambda s, i: (1 - s, 0, 0)),
        pl.BlockSpec((None, Q, Q), lambda s, i: (1 - s, 0, 0)),
        _const_spec(rexp.shape), _const_spec(ecol.shape),
        slab,
    ]
    est = (L * DS * 4 + 2 * SSD_GROUPS * SSD_STATE * DS * 2 + 4 * Q * (DS + bc.shape[1]) * 4
           + 2 * (rexp.size + ecol.size) * 2 + 2 * Q * DS * 4 * 2 + 24 * Q * DS * 4 + 12 * slab_rows * w_up.shape[1])
    return pl.pallas_call(
        _ssd_kernel,
        grid=(2, nc),
        in_specs=in_specs,
        out_specs=[pl.BlockSpec((Q, DS), lambda s, i: (s * i, 0)), slab],
        out_shape=[jax.ShapeDtypeStruct((L, DS), BF16), jax.ShapeDtypeStruct(w_up.shape, BF16)],
        scratch_shapes=[
            pltpu.VMEM((L, DS), F32),
            pltpu.VMEM((SSD_GROUPS, SSD_STATE, DS // SSD_GROUPS), F32),
        ],
        compiler_params=pltpu.CompilerParams(dimension_semantics=("arbitrary", "arbitrary"),
                                             vmem_limit_bytes=_vmem_limit(est)),
        name="ssd",
    )(xs, bc, dt_raw, z, dtb, alog, dexp, nw, tcat, negmask, rexp, ecol, w_up)


def _mixout_kernel(up_ref, um_ref, un_ref, ys_ref, x_ref, pw_ref, ps_ref, wo_ref, g_ref, gn_ref,
                   o_ref, on_ref, ext_ref, *, seq):
    i = pl.program_id(0)
    n = pl.num_programs(0)
    tm = um_ref.shape[0]
    cg = um_ref.shape[1] // len(POOL_WINDOWS)
    ext_ref[0:8, :] = jnp.where(i > 0, up_ref[...], 0.0)
    ext_ref[8:8 + tm, :] = um_ref[...]
    ext_ref[8 + tm:16 + tm, :] = jnp.where(i < n - 1, un_ref[...], 0.0)
    t = i * tm + lax.broadcasted_iota(jnp.int32, (tm, cg), 0)
    pooled = []
    for gi, k in enumerate(POOL_WINDOWS):
        cols = slice(gi * cg, (gi + 1) * cg)
        lo_off = 8 - k // 2
        acc = ext_ref[lo_off:lo_off + tm, cols]
        for j in range(1, k):
            acc = acc + ext_ref[lo_off + j:lo_off + j + tm, cols]
        cnt = (jnp.minimum(t + (k - k // 2), seq) - jnp.maximum(t - k // 2, 0)).astype(F32)
        mixed = acc / cnt - um_ref[:, cols]
        yp = jnp.dot(mixed.astype(BF16), pw_ref[gi], preferred_element_type=F32) * ps_ref[:, cols]
        pooled.append(yp.astype(BF16))
    ypool = jnp.concatenate(pooled, axis=1)
    ds = ys_ref.shape[1]
    mix = jnp.dot(ys_ref[...], wo_ref[0:ds, :], preferred_element_type=F32)
    mix = mix + jnp.dot(ypool, wo_ref[ds:, :], preferred_element_type=F32)
    h = x_ref[...] + _rms(mix) * g_ref[...]
    o_ref[...] = h
    on_ref[...] = (_rms(h) * gn_ref[...]).astype(on_ref.dtype)


def _mixout(u, y_ssd, x, pool_w, pool_scale, w_out, g, g_next, *, tm):
    L, DP = u.shape
    D = x.shape[1]
    row = lambda n_: pl.BlockSpec((tm, n_), lambda i: (i, 0))
    in_specs = (_halo_specs(tm, V7X_SUBLANES, L, DP) + [row(y_ssd.shape[1]), row(D)]
                + [_const_spec(a.shape) for a in (pool_w, pool_scale, w_out, g, g_next)])
    est = (2 * tm * (DP * 4 + y_ssd.shape[1] * 2 + D * 4 + D * 4 + D * 2) + (w_out.size + pool_w.size) * 2
           + (tm + 16) * DP * 4 + 8 * tm * D * 4)
    return pl.pallas_call(
        functools.partial(_mixout_kernel, seq=L),
        grid=(L // tm,),
        in_specs=in_specs,
        out_specs=[row(D), row(D)],
        out_shape=[jax.ShapeDtypeStruct((L, D), F32), jax.ShapeDtypeStruct((L, D), BF16)],
        scratch_shapes=[pltpu.VMEM((tm + 16, DP), F32)],
        compiler_params=pltpu.CompilerParams(dimension_semantics=("arbitrary",), vmem_limit_bytes=_vmem_limit(est)),
        name="mixout",
    )(u, u, u, y_ssd, x, pool_w, pool_scale, w_out, g, g_next)


def _ffn_kernel(hp_ref, hm_ref, hx_ref, wg_ref, wv_ref, cw_ref, cb_ref, wd_ref, gpost_ref,
                o_ref, hn_ref, gate_ref):
    i = pl.program_id(0)
    f = pl.program_id(1)
    n = pl.num_programs(0)
    nf = pl.num_programs(1)
    tm = hm_ref.shape[0]
    H = BF16_ROWS

    @pl.when(f == 0)
    def _():
        hn_ref[0:H, :] = jnp.where(i > 0, hp_ref[...], jnp.zeros_like(hp_ref))
        hn_ref[H:H + tm, :] = hm_ref[...]
        hn_ref[H + tm:2 * H + tm, :] = jnp.where(i < n - 1, hx_ref[...], jnp.zeros_like(hx_ref))
        o_ref[...] = jnp.zeros_like(o_ref)

    gate_ref[...] = jnp.dot(hn_ref[...], wg_ref[...], preferred_element_type=F32)
    val = jnp.dot(hm_ref[...], wv_ref[...], preferred_element_type=F32)
    half = FFN_CONV // 2
    gc = cw_ref[0:1, :] * gate_ref[H - half:H - half + tm, :]
    for j in range(1, FFN_CONV):
        gc = gc + cw_ref[j:j + 1, :] * gate_ref[H - half + j:H - half + j + tm, :]
    gc = gc + cb_ref[...]
    act = (jax.nn.gelu(gc, approximate=True) * val).astype(BF16)
    o_ref[...] += jnp.dot(act, wd_ref[...], preferred_element_type=F32)

    @pl.when(f == nf - 1)
    def _():
        o_ref[...] = _rms(o_ref[...]) * gpost_ref[...]


def _ffn(hn, w_up, cw, cb, w_down, gpost, *, tm, tf):
    L, D = hn.shape
    DF = w_down.shape[0]
    nf = DF // tf
    H = BF16_ROWS
    in_specs = _halo_specs(tm, H, L, D) + [
        pl.BlockSpec((D, tf), lambda i, f: (0, f)),
        pl.BlockSpec((D, tf), lambda i, f: (0, nf + f)),
        pl.BlockSpec((FFN_CONV, tf), lambda i, f: (0, f)),
        pl.BlockSpec((1, tf), lambda i, f: (0, f)),
        pl.BlockSpec((tf, D), lambda i, f: (f, 0)),
        pl.BlockSpec(gpost.shape, lambda i, f: (0, 0)),
    ]
    est = (2 * (tm + 2 * H) * D * 2 + 2 * tm * D * 4 + (tm + 2 * H) * D * 2 + (tm + 2 * H) * tf * 4
           + 2 * 3 * D * tf * 2 + 8 * tm * tf * 4 + tm * D * 4)
    return pl.pallas_call(
        _ffn_kernel,
        grid=(L // tm, nf),
        in_specs=in_specs,
        out_specs=pl.BlockSpec((tm, D), lambda i, f: (i, 0)),
        out_shape=jax.ShapeDtypeStruct((L, D), F32),
        scratch_shapes=[
            pltpu.VMEM((tm + 2 * H, D), BF16),
            pltpu.VMEM((tm + 2 * H, tf), F32),
        ],
        compiler_params=pltpu.CompilerParams(dimension_semantics=("arbitrary", "arbitrary"),
                                             vmem_limit_bytes=_vmem_limit(est)),
        name="ffn",
    )(hn, hn, hn, w_up, w_up, cw, cb, w_down, gpost)


def _ple_kernel(h_ref, r_ref, p_ref, gpre_ref, wg_ref, wp_ref, gpost_ref, o_ref):
    h = h_ref[...] + r_ref[...]
    hn = (_rms(h) * gpre_ref[...]).astype(BF16)
    gate = jax.nn.sigmoid(jnp.dot(hn, wg_ref[...], preferred_element_type=F32))
    pe = jnp.dot(p_ref[...].astype(BF16), wp_ref[...], preferred_element_type=F32)
    o_ref[...] = h + _rms(gate * pe) * gpost_ref[...]


def _ple(h, r, p, gpre, w_gate, w_ple, gpost, *, tm):
    L, D = h.shape
    row = lambda n: pl.BlockSpec((tm, n), lambda i: (i, 0))
    est = 2 * tm * (3 * D + p.shape[1]) * 4 + (w_gate.size + w_ple.size) * 2 + 8 * tm * D * 4
    return pl.pallas_call(
        _ple_kernel,
        grid=(L // tm,),
        in_specs=[row(D), row(D), row(p.shape[1])] + [_const_spec(a.shape) for a in (gpre, w_gate, w_ple, gpost)],
        out_specs=row(D),
        out_shape=jax.ShapeDtypeStruct((L, D), F32),
        compiler_params=pltpu.CompilerParams(dimension_semantics=("arbitrary",), vmem_limit_bytes=_vmem_limit(est)),
        name="ple",
    )(h, r, p, gpre, w_gate, w_ple, gpost)


def _ssd_constants():
    Q = SSD_CHUNK
    r = jnp.arange(Q)
    lower = (r[:, None] >= r[None, :])
    tri = jnp.stack([lower, lower.T]).astype(BF16)
    tcat = jnp.concatenate([tri, tri, tri], axis=2)
    negmask = jnp.where(jnp.stack([lower, lower.T]), 0.0, NEG_BIG).astype(F32)
    k = jnp.arange(V7X_LANES)
    head_of_lane = jnp.where(k < 3 * SSD_PACK, k % SSD_PACK, -1)
    rexp = (head_of_lane[:, None] == (jnp.arange(SSD_HEADS * SSD_HEAD_DIM) // SSD_HEAD_DIM)[None, :]).astype(BF16)
    ecol = (head_of_lane[:, None] == (jnp.arange(SSD_HEADS * Q) // Q)[None, :]).astype(BF16)
    return tcat, negmask, rexp, ecol


def _pad_lanes(v, fill):
    out = jnp.full((v.shape[0], 1, V7X_LANES), fill, F32)
    return out.at[:, 0, :v.shape[1]].set(v.astype(F32))


def kernel(x, p, mix_norm_pre, mix_norm_post, w_in, ssd_conv_w, ssd_conv_b, ssd_dt_bias, ssd_a_log, ssd_d,
           ssd_norm, pool_w, pool_scale, w_out, ffn_norm_pre, ffn_norm_post, w_ffn_up, ffn_conv_w, ffn_conv_b,
           w_ffn_down, ple_norm_pre, w_ple_gate, w_ple, ple_norm_post):
    B, L, D = x.shape
    depth = w_in.shape[0]
    d_ssd = SSD_HEADS * SSD_HEAD_DIM
    n_bc = 2 * SSD_GROUPS * SSD_STATE
    o_xbc = d_ssd
    o_dt = o_xbc + d_ssd + n_bc
    o_u = o_dt + 2 * SSD_HEADS
    d_pool = w_in.shape[2] - o_u
    tcat, negmask, rexp, ecol = _ssd_constants()
    row = lambda v: v.reshape(1, -1).astype(F32)
    lane_pad = jnp.zeros((D, V7X_LANES - SSD_HEADS), F32)

    outs = []
    for b in range(B):
        h = x[b]
        for i in range(depth):
            wi = w_in[i]
            w_cat = jnp.concatenate(
                [wi[:, o_xbc:o_dt], wi[:, 0:o_xbc], wi[:, o_u:], wi[:, o_dt:o_dt + SSD_HEADS], lane_pad,
                 wi[:, o_dt + SSD_HEADS:o_u], lane_pad], axis=1).astype(BF16)
            xs, bc, z, u, dt_raw, w_down16, w_out16, w_gate16 = _inproj(
                h, row(mix_norm_pre[i]), w_cat, ssd_conv_w[i].astype(F32), row(ssd_conv_b[i]),
                (d_ssd, n_bc, d_ssd, d_pool, 2 * V7X_LANES), w_ffn_down[i], w_out[i], w_ple_gate[i], tm=INPROJ_ROWS)
            y_ssd, w_up16 = _ssd(
                xs, bc, dt_raw, z, _pad_lanes(ssd_dt_bias[i], 0.0), _pad_lanes(ssd_a_log[i], NEG_BIG),
                jnp.repeat(ssd_d[i].astype(F32), SSD_HEAD_DIM).reshape(1, -1), row(ssd_norm[i]),
                tcat, negmask, rexp, ecol, w_ffn_up[i])
            h, hn = _mixout(u, y_ssd, h, pool_w[i].astype(BF16), row(pool_scale[i]), w_out16,
                            row(mix_norm_post[i]), row(ffn_norm_pre[i]), tm=MIXOUT_ROWS)
            r = _ffn(hn, w_up16, ffn_conv_w[i].astype(F32), row(ffn_conv_b[i]), w_down16, row(ffn_norm_post[i]),
                     tm=FFN_ROWS, tf=FFN_COLS)
            h = _ple(h, r, p[i, b], row(ple_norm_pre[i]), w_gate16, w_ple[i].astype(BF16),
                     row(ple_norm_post[i]), tm=PLE_ROWS)
        outs.append(h)
    return jnp.stack(outs)
```

```python
import functools

import jax
import jax.numpy as jnp
from jax import lax
from jax.experimental import pallas as pl
from jax.experimental.pallas import tpu as pltpu

F32 = jnp.float32
BF16 = jnp.bfloat16
EPS = 1e-6

V7X_VMEM_BYTES = 64 * 1024 * 1024
V7X_LANES = 128
V7X_SUBLANES = 8
BF16_ROWS = 16

SSD_HEAD_DIM = 64
SSD_HEADS = 16
SSD_GROUPS = 2
SSD_HEADS_PER_GROUP = SSD_HEADS // SSD_GROUPS
SSD_STATE = 128
SSD_CONV = 5
SSD_CHUNK = 128
POOL_WINDOWS = (2, 4, 8, 16)
FFN_CONV = 3
NEG_BIG = -1e30

INPROJ_ROWS = 512
MIXOUT_ROWS = 512
FFN_ROWS = 1024
FFN_COLS = 512
PLE_ROWS = 512
CONV_PIECE = 256


def _vmem_limit(nbytes):
    return int(min(nbytes, V7X_VMEM_BYTES - 6 * 1024 * 1024))


def _rms(x):
    return x * lax.rsqrt(jnp.mean(x * x, axis=-1, keepdims=True) + EPS)


def _split3(x):
    hi = x.astype(BF16).astype(F32)
    r1 = x - hi
    mid = r1.astype(BF16).astype(F32)
    lo = (r1 - mid).astype(BF16).astype(F32)
    return hi, mid, lo


def _const_spec(shape):
    nd = len(shape)
    return pl.BlockSpec(shape, lambda *_: (0,) * nd, pipeline_mode=pl.Buffered(1))


def _halo_specs(rows, halo, total_rows, ncols):
    hb = rows // halo
    nhb = total_rows // halo
    return [
        pl.BlockSpec((halo, ncols), lambda i, *_: (jnp.maximum(i * hb - 1, 0), 0)),
        pl.BlockSpec((rows, ncols), lambda i, *_: (i, 0)),
        pl.BlockSpec((halo, ncols), lambda i, *_: (jnp.minimum(i * hb + hb, nhb - 1), 0)),
    ]


def _inproj_kernel(xp_ref, xm_ref, xn_ref, g_ref, wm_ref, wt_ref, cw_ref, cb_ref, wd_ref, wo_ref, wg_ref,
                   xs_ref, bc_ref, z_ref, u_ref, dt_ref, wd16_ref, wo16_ref, wg16_ref,
                   hn_ref, *ext_refs):
    i = pl.program_id(0)
    n = pl.num_programs(0)
    tm = xm_ref.shape[0]
    H = BF16_ROWS
    g = g_ref[...]
    hn_ref[0:H, :] = jnp.where(i > 0, _rms(xp_ref[...]) * g, 0.0).astype(BF16)
    hn_ref[H:H + tm, :] = (_rms(xm_ref[...]) * g).astype(BF16)
    hn_ref[H + tm:2 * H + tm, :] = jnp.where(i < n - 1, _rms(xn_ref[...]) * g, 0.0).astype(BF16)
    for src, dst in ((wd_ref, wd16_ref), (wo_ref, wo16_ref), (wg_ref, wg16_ref)):
        dst[...] = src[...].astype(BF16)

    cs = CONV_PIECE
    nz, nxs, nu = z_ref.shape[1], xs_ref.shape[1], u_ref.shape[1]
    conv_outs = [(xs_ref, c, wm_ref, nz + c, c) for c in range(0, nxs, cs)]
    conv_outs += [(bc_ref, c, wm_ref, nz + nxs + c, nxs + c) for c in range(0, bc_ref.shape[1], cs)]
    plain_outs = [(z_ref, c, wm_ref, c) for c in range(0, nz, cs)]
    plain_outs += [(u_ref, c, wt_ref, c) for c in range(0, nu, cs)]
    plain_outs += [(dt_ref, c, wt_ref, nu + c) for c in range(0, dt_ref.shape[1], cs)]

    def plain_dot(o_ref, c0, w_ref, wc):
        o_ref[:, c0:c0 + cs] = jnp.dot(hn_ref[H:H + tm, :], w_ref[:, wc:wc + cs],
                                       preferred_element_type=F32).astype(o_ref.dtype)

    half = SSD_CONV // 2
    for k, (o_ref, c0, w_ref, wc, cc) in enumerate(conv_outs):
        ext_ref = ext_refs[k % len(ext_refs)]
        ext_ref[...] = jnp.dot(hn_ref[...], w_ref[:, wc:wc + cs], preferred_element_type=F32)
        if k < len(plain_outs):
            plain_dot(*plain_outs[k])
        acc = cw_ref[0:1, cc:cc + cs] * ext_ref[H - half:H - half + tm, :]
        for j in range(1, SSD_CONV):
            acc = acc + cw_ref[j:j + 1, cc:cc + cs] * ext_ref[H - half + j:H - half + j + tm, :]
        acc = acc + cb_ref[:, cc:cc + cs]
        o_ref[:, c0:c0 + cs] = (acc * jax.nn.sigmoid(acc)).astype(o_ref.dtype)
    for args in plain_outs[len(conv_outs):]:
        plain_dot(*args)


def _inproj(x, g, w_main, w_tail, cw, cb, out_cols, w_down, w_out, w_gate, *, tm):
    L, D = x.shape
    H = BF16_ROWS
    n = L // tm
    dts = (BF16, BF16, F32, F32, F32)
    row = lambda c: pl.BlockSpec((tm, c), lambda i: (i, 0))
    slab = lambda a: pl.BlockSpec((a.shape[0] // n, a.shape[1]), lambda i: (i, 0))
    casts = (w_down, w_out, w_gate)
    est = (2 * (tm + 2 * H) * D * 4 + (w_main.size + w_tail.size) * 2 + 2 * tm * sum(out_cols) * 4
           + (tm + 2 * H) * D * 2 + 4 * (tm + 2 * H) * CONV_PIECE * 4 + 2 * sum(a.size for a in casts) // n * 6
           + 8 * tm * CONV_PIECE * 4)
    return pl.pallas_call(
        _inproj_kernel,
        grid=(n,),
        in_specs=_halo_specs(tm, H, L, D) + [_const_spec(a.shape) for a in (g, w_main, w_tail, cw, cb)]
                 + [slab(a) for a in casts],
        out_specs=[row(c) for c in out_cols] + [slab(a) for a in casts],
        out_shape=[jax.ShapeDtypeStruct((L, c), dt) for c, dt in zip(out_cols, dts)]
                  + [jax.ShapeDtypeStruct(a.shape, BF16) for a in casts],
        scratch_shapes=[pltpu.VMEM((tm + 2 * H, D), BF16)]
                       + [pltpu.VMEM((tm + 2 * H, CONV_PIECE), F32) for _ in range(2)],
        compiler_params=pltpu.CompilerParams(dimension_semantics=("arbitrary",), vmem_limit_bytes=_vmem_limit(est)),
        name="inproj",
    )(x, x, x, g, w_main, w_tail, cw, cb, *casts)


SSD_PACK = 16


def _pack3(v, head_lanes):
    hi, mid, lo = _split3(jnp.where(head_lanes, v, jnp.zeros_like(v)))
    packed = hi + pltpu.roll(mid, SSD_PACK, axis=1) + pltpu.roll(lo, 2 * SSD_PACK, axis=1)
    return packed.astype(BF16)


def _scan_both(xs_refs, bc_refs, dt_refs, dtb_ref, alog_ref, t_ref, nm_ref, rexp_ref, ecol_ref, h_ref):
    Q = SSD_CHUNK
    P = SSD_HEAD_DIM
    E = SSD_HEADS_PER_GROUP
    GW = E * P
    DIRS = (0, 1)
    last = (Q - 1, 0)
    xs = [xs_refs[d][...].astype(F32) for d in DIRS]
    bc = [bc_refs[d][...] for d in DIRS]

    head_lanes = lax.broadcasted_iota(jnp.int32, (Q, V7X_LANES), 1) < SSD_HEADS
    dt = [jax.nn.softplus(dt_refs[d][...] + dtb_ref[d]) for d in DIRS]
    da = [dt[d] * -jnp.exp(alog_ref[d]) for d in DIRS]
    da3 = [jnp.concatenate([v.astype(BF16) for v in _split3(da[d])], axis=0) for d in DIRS]
    acum = [jnp.dot(t_ref[d], da3[d], preferred_element_type=F32) for d in DIRS]
    tot = [acum[d][last[d]:last[d] + 1, :] for d in DIRS]
    acum_t = [acum[d].T for d in DIRS]
    e_a = [jnp.exp(acum[d]) for d in DIRS]
    w_end = [jnp.exp(tot[d] - acum[d]) for d in DIRS]

    lhs = [jnp.concatenate([_pack3(v, head_lanes) for v in (dt[d], dt[d] * w_end[d], e_a[d])], axis=0)
           for d in DIRS]
    exp3 = [jnp.dot(lhs[d], rexp_ref[...], preferred_element_type=F32) for d in DIRS]
    dt_x = [exp3[d][0:Q] for d in DIRS]
    dtw_x = [exp3[d][Q:2 * Q] for d in DIRS]
    ea_x = [exp3[d][2 * Q:3 * Q] for d in DIRS]
    cd_x = [ea_x[d][last[d]:last[d] + 1, :] for d in DIRS]
    acol = [jnp.dot(_pack3(acum[d], head_lanes), ecol_ref[...], preferred_element_type=F32) for d in DIRS]

    xdt = [(xs[d] * dt_x[d]).astype(BF16) for d in DIRS]
    xw = [(xs[d] * dtw_x[d]).astype(BF16) for d in DIRS]
    negmask = [nm_ref[d] for d in DIRS]
    first_head = lax.broadcasted_iota(jnp.int32, (Q, 2 * P), 1) < P

    y_groups = [[], []]
    for g in range(SSD_GROUPS):
        gw = slice(g * GW, (g + 1) * GW)
        bm = [bc[d][:, g * SSD_STATE:(g + 1) * SSD_STATE] for d in DIRS]
        cm = [bc[d][:, (SSD_GROUPS + g) * SSD_STATE:(SSD_GROUPS + g + 1) * SSD_STATE] for d in DIRS]
        cb = [lax.dot_general(cm[d], bm[d], (((1,), (1,)), ((), ())), preferred_element_type=F32)
              for d in DIRS]
        h_in = [h_ref[d, g] for d in DIRS]
        y_off = [jnp.dot(cm[d], h_in[d].astype(BF16), preferred_element_type=F32) for d in DIRS]
        y_g = [y_off[d] * ea_x[d][:, gw] for d in DIRS]
        st = [lax.dot_general(bm[d], xw[d][:, gw], (((0,), (0,)), ((), ())), preferred_element_type=F32)
              for d in DIRS]
        for d in DIRS:
            h_ref[d, g] = h_in[d] * cd_x[d][:, gw] + st[d]
        pairs = [[], []]
        for hp in range(E // 2):
            for d in DIRS:
                ms = []
                for k in range(2):
                    h = g * E + hp * 2 + k
                    seg = acol[d][:, h * Q:(h + 1) * Q] - acum_t[d][h:h + 1, :] + negmask[d]
                    ms.append((cb[d] * jnp.exp(seg)).astype(BF16))
                m2 = jnp.concatenate(ms, axis=1)
                xp = xdt[d][:, (g * E + hp * 2) * P:(g * E + hp * 2 + 2) * P]
                zero = jnp.zeros_like(xp)
                rhs = jnp.concatenate([jnp.where(first_head, xp, zero), jnp.where(first_head, zero, xp)], axis=0)
                pairs[d].append(jnp.dot(m2, rhs, preferred_element_type=F32))
        for d in DIRS:
            y_groups[d].append(y_g[d] + jnp.concatenate(pairs[d], axis=1))
    return [(jnp.concatenate(y_groups[d], axis=1), xs[d]) for d in DIRS]


def _ssd_kernel(xsf_ref, bcf_ref, dtf_ref, zf_ref, xsb_ref, bcb_ref, dtb_raw_ref, zb_ref,
                dtb_ref, alog_ref, dexp_ref, nw_ref, t_ref, nm_ref, rexp_ref, ecol_ref, wup_ref,
                lo_ref, hi_ref, wup16_ref,
                y_ref, h_ref):
    Q = SSD_CHUNK
    GW = SSD_HEADS_PER_GROUP * SSD_HEAD_DIM
    i = pl.program_id(0)
    nc = pl.num_programs(0)

    wup16_ref[...] = wup_ref[...].astype(BF16)

    @pl.when(i == 0)
    def _():
        h_ref[...] = jnp.zeros_like(h_ref)

    (y_f, xs_f), (y_b, xs_b) = _scan_both((xsf_ref, xsb_ref), (bcf_ref, bcb_ref), (dtf_ref, dtb_raw_ref),
                                          dtb_ref, alog_ref, t_ref, nm_ref, rexp_ref, ecol_ref, h_ref)
    row_f = pl.multiple_of(i * Q, Q)
    row_b = pl.multiple_of((nc - 1 - i) * Q, Q)

    @pl.when(i < nc // 2)
    def _():
        y_ref[pl.ds(row_f, Q), :] = y_f
        y_ref[pl.ds(row_b, Q), :] = y_b

    def finish(y, xs, z_ref, o_ref):
        yy = y + xs * dexp_ref[...]
        z = z_ref[...]
        yy = yy * (z * jax.nn.sigmoid(z))
        outs = [_rms(yy[:, g * GW:(g + 1) * GW]) for g in range(SSD_GROUPS)]
        o_ref[...] = (jnp.concatenate(outs, axis=1) * nw_ref[...]).astype(o_ref.dtype)

    @pl.when(i >= nc // 2)
    def _():
        finish(y_f + y_ref[pl.ds(row_f, Q), :], xs_f, zf_ref, hi_ref)
        finish(y_b + y_ref[pl.ds(row_b, Q), :], xs_b, zb_ref, lo_ref)


def _ssd(xs, bc, dt_raw, z, dtb, alog, dexp, nw, tcat, negmask, rexp, ecol, w_up):
    L, DS = xs.shape
    Q = SSD_CHUNK
    nc = L // Q
    hc = nc // 2
    slab = pl.BlockSpec((w_up.shape[0] // nc, w_up.shape[1]), lambda i: (i, 0))
    fwd = lambda i: i
    bwd = lambda i: nc - 1 - i
    fwd_late = lambda i: jnp.maximum(i, hc)
    bwd_late = lambda i: jnp.minimum(nc - 1 - i, hc - 1)
    in_specs = [
        pl.BlockSpec((Q, DS), lambda i: (fwd(i), 0)),
        pl.BlockSpec((Q, bc.shape[1]), lambda i: (fwd(i), 0)),
        pl.BlockSpec((Q, V7X_LANES), lambda i: (fwd(i), 0)),
        pl.BlockSpec((Q, DS), lambda i: (fwd_late(i), 0)),
        pl.BlockSpec((Q, DS), lambda i: (bwd(i), 0)),
        pl.BlockSpec((Q, bc.shape[1]), lambda i: (bwd(i), 0)),
        pl.BlockSpec((Q, V7X_LANES), lambda i: (bwd(i), 1)),
        pl.BlockSpec((Q, DS), lambda i: (bwd_late(i), 0)),
    ] + [_const_spec(a.shape) for a in (dtb, alog, dexp, nw, tcat, negmask, rexp, ecol)] + [slab]
    est = (L * DS * 4 + 2 * SSD_GROUPS * SSD_STATE * DS * 4 + 8 * Q * (DS + bc.shape[1]) * 4
           + (rexp.size + ecol.size) * 2 + 4 * Q * DS * 4 * 2 + 40 * Q * DS * 4 + 12 * w_up.size // nc)
    return pl.pallas_call(
        _ssd_kernel,
        grid=(nc,),
        in_specs=in_specs,
        out_specs=[pl.BlockSpec((Q, DS), lambda i: (bwd_late(i), 0)),
                   pl.BlockSpec((Q, DS), lambda i: (fwd_late(i) - hc, 0)), slab],
        out_shape=[jax.ShapeDtypeStruct((L // 2, DS), BF16), jax.ShapeDtypeStruct((L // 2, DS), BF16),
                   jax.ShapeDtypeStruct(w_up.shape, BF16)],
        scratch_shapes=[
            pltpu.VMEM((L, DS), F32),
            pltpu.VMEM((2, SSD_GROUPS, SSD_STATE, DS // SSD_GROUPS), F32),
        ],
        compiler_params=pltpu.CompilerParams(dimension_semantics=("arbitrary",), vmem_limit_bytes=_vmem_limit(est)),
        name="ssd",
    )(xs, bc, dt_raw, z, xs, bc, dt_raw, z, dtb, alog, dexp, nw, tcat, negmask, rexp, ecol, w_up)


def _mixout_kernel(up_ref, um_ref, un_ref, ylo_ref, yhi_ref, x_ref, pw_ref, ps_ref, wo_ref, g_ref, gn_ref,
                   o_ref, on_ref, ext_ref, *, seq):
    i = pl.program_id(0)
    n = pl.num_programs(0)
    tm = um_ref.shape[0]
    cg = um_ref.shape[1] // len(POOL_WINDOWS)
    ext_ref[0:8, :] = jnp.where(i > 0, up_ref[...], 0.0)
    ext_ref[8:8 + tm, :] = um_ref[...]
    ext_ref[8 + tm:16 + tm, :] = jnp.where(i < n - 1, un_ref[...], 0.0)
    t = i * tm + lax.broadcasted_iota(jnp.int32, (tm, cg), 0)
    pooled = []
    for gi, k in enumerate(POOL_WINDOWS):
        cols = slice(gi * cg, (gi + 1) * cg)
        lo_off = 8 - k // 2
        acc = ext_ref[lo_off:lo_off + tm, cols]
        for j in range(1, k):
            acc = acc + ext_ref[lo_off + j:lo_off + j + tm, cols]
        cnt = (jnp.minimum(t + (k - k // 2), seq) - jnp.maximum(t - k // 2, 0)).astype(F32)
        mixed = acc / cnt - um_ref[:, cols]
        yp = jnp.dot(mixed.astype(BF16), pw_ref[gi], preferred_element_type=F32) * ps_ref[:, cols]
        pooled.append(yp.astype(BF16))
    ypool = jnp.concatenate(pooled, axis=1)
    ys = jnp.where(i < n // 2, ylo_ref[...], yhi_ref[...])
    ds = ys.shape[1]
    mix = jnp.dot(ys, wo_ref[0:ds, :], preferred_element_type=F32)
    mix = mix + jnp.dot(ypool, wo_ref[ds:, :], preferred_element_type=F32)
    h = x_ref[...] + _rms(mix) * g_ref[...]
    o_ref[...] = h
    on_ref[...] = (_rms(h) * gn_ref[...]).astype(on_ref.dtype)


def _mixout(u, y_lo, y_hi, x, pool_w, pool_scale, w_out, g, g_next, *, tm):
    L, DP = u.shape
    D = x.shape[1]
    DS = y_lo.shape[1]
    n = L // tm
    row = lambda n_: pl.BlockSpec((tm, n_), lambda i: (i, 0))
    in_specs = (_halo_specs(tm, V7X_SUBLANES, L, DP)
                + [pl.BlockSpec((tm, DS), lambda i: (jnp.minimum(i, n // 2 - 1), 0)),
                   pl.BlockSpec((tm, DS), lambda i: (jnp.maximum(i - n // 2, 0), 0)), row(D)]
                + [_const_spec(a.shape) for a in (pool_w, pool_scale, w_out, g, g_next)])
    est = (2 * tm * (DP * 4 + 2 * DS * 2 + D * 4 + D * 4 + D * 2) + (w_out.size + pool_w.size) * 2
           + (tm + 16) * DP * 4 + 8 * tm * D * 4)
    return pl.pallas_call(
        functools.partial(_mixout_kernel, seq=L),
        grid=(n,),
        in_specs=in_specs,
        out_specs=[row(D), row(D)],
        out_shape=[jax.ShapeDtypeStruct((L, D), F32), jax.ShapeDtypeStruct((L, D), BF16)],
        scratch_shapes=[pltpu.VMEM((tm + 16, DP), F32)],
        compiler_params=pltpu.CompilerParams(dimension_semantics=("arbitrary",), vmem_limit_bytes=_vmem_limit(est)),
        name="mixout",
    )(u, u, u, y_lo, y_hi, x, pool_w, pool_scale, w_out, g, g_next)


def _ffn_kernel(hp_ref, hm_ref, hx_ref, wg_ref, wv_ref, cw_ref, cb_ref, wd_ref, gpost_ref,
                o_ref, hn_ref, gate_ref):
    i = pl.program_id(0)
    f = pl.program_id(1)
    n = pl.num_programs(0)
    nf = pl.num_programs(1)
    tm = hm_ref.shape[0]
    H = BF16_ROWS

    @pl.when(f == 0)
    def _():
        hn_ref[0:H, :] = jnp.where(i > 0, hp_ref[...], jnp.zeros_like(hp_ref))
        hn_ref[H:H + tm, :] = hm_ref[...]
        hn_ref[H + tm:2 * H + tm, :] = jnp.where(i < n - 1, hx_ref[...], jnp.zeros_like(hx_ref))
        o_ref[...] = jnp.zeros_like(o_ref)

    gate_ref[...] = jnp.dot(hn_ref[...], wg_ref[...], preferred_element_type=F32)
    val = jnp.dot(hm_ref[...], wv_ref[...], preferred_element_type=F32)
    half = FFN_CONV // 2
    gc = cw_ref[0:1, :] * gate_ref[H - half:H - half + tm, :]
    for j in range(1, FFN_CONV):
        gc = gc + cw_ref[j:j + 1, :] * gate_ref[H - half + j:H - half + j + tm, :]
    gc = gc + cb_ref[...]
    act = (jax.nn.gelu(gc, approximate=True) * val).astype(BF16)
    o_ref[...] += jnp.dot(act, wd_ref[...], preferred_element_type=F32)

    @pl.when(f == nf - 1)
    def _():
        o_ref[...] = _rms(o_ref[...]) * gpost_ref[...]


def _ffn(hn, w_up, cw, cb, w_down, gpost, *, tm, tf):
    L, D = hn.shape
    DF = w_down.shape[0]
    nf = DF // tf
    H = BF16_ROWS
    in_specs = _halo_specs(tm, H, L, D) + [
        pl.BlockSpec((D, tf), lambda i, f: (0, f)),
        pl.BlockSpec((D, tf), lambda i, f: (0, nf + f)),
        pl.BlockSpec((FFN_CONV, tf), lambda i, f: (0, f)),
        pl.BlockSpec((1, tf), lambda i, f: (0, f)),
        pl.BlockSpec((tf, D), lambda i, f: (f, 0)),
        pl.BlockSpec(gpost.shape, lambda i, f: (0, 0)),
    ]
    est = (2 * (tm + 2 * H) * D * 2 + 2 * tm * D * 4 + (tm + 2 * H) * D * 2 + (tm + 2 * H) * tf * 4
           + 2 * 3 * D * tf * 2 + 8 * tm * tf * 4 + tm * D * 4)
    return pl.pallas_call(
        _ffn_kernel,
        grid=(L // tm, nf),
        in_specs=in_specs,
        out_specs=pl.BlockSpec((tm, D), lambda i, f: (i, 0)),
        out_shape=jax.ShapeDtypeStruct((L, D), F32),
        scratch_shapes=[
            pltpu.VMEM((tm + 2 * H, D), BF16),
            pltpu.VMEM((tm + 2 * H, tf), F32),
        ],
        compiler_params=pltpu.CompilerParams(dimension_semantics=("arbitrary", "arbitrary"),
                                             vmem_limit_bytes=_vmem_limit(est)),
        name="ffn",
    )(hn, hn, hn, w_up, w_up, cw, cb, w_down, gpost)


def _ple_kernel(h_ref, r_ref, p_ref, gpre_ref, wg_ref, wp_ref, gpost_ref, o_ref):
    h = h_ref[...] + r_ref[...]
    hn = (_rms(h) * gpre_ref[...]).astype(BF16)
    gate = jax.nn.sigmoid(jnp.dot(hn, wg_ref[...], preferred_element_type=F32))
    pe = jnp.dot(p_ref[...].astype(BF16), wp_ref[...], preferred_element_type=F32)
    o_ref[...] = h + _rms(gate * pe) * gpost_ref[...]


def _ple(h, r, p, gpre, w_gate, w_ple, gpost, *, tm):
    L, D = h.shape
    row = lambda n: pl.BlockSpec((tm, n), lambda i: (i, 0))
    est = 2 * tm * (3 * D + p.shape[1]) * 4 + (w_gate.size + w_ple.size) * 2 + 8 * tm * D * 4
    return pl.pallas_call(
        _ple_kernel,
        grid=(L // tm,),
        in_specs=[row(D), row(D), row(p.shape[1])] + [_const_spec(a.shape) for a in (gpre, w_gate, w_ple, gpost)],
        out_specs=row(D),
        out_shape=jax.ShapeDtypeStruct((L, D), F32),
        compiler_params=pltpu.CompilerParams(dimension_semantics=("arbitrary",), vmem_limit_bytes=_vmem_limit(est)),
        name="ple",
    )(h, r, p, gpre, w_gate, w_ple, gpost)


def _ssd_constants():
    Q = SSD_CHUNK
    r = jnp.arange(Q)
    lower = (r[:, None] >= r[None, :])
    tri = jnp.stack([lower, lower.T]).astype(BF16)
    tcat = jnp.concatenate([tri, tri, tri], axis=2)
    negmask = jnp.where(jnp.stack([lower, lower.T]), 0.0, NEG_BIG).astype(F32)
    k = jnp.arange(V7X_LANES)
    head_of_lane = jnp.where(k < 3 * SSD_PACK, k % SSD_PACK, -1)
    rexp = (head_of_lane[:, None] == (jnp.arange(SSD_HEADS * SSD_HEAD_DIM) // SSD_HEAD_DIM)[None, :]).astype(BF16)
    ecol = (head_of_lane[:, None] == (jnp.arange(SSD_HEADS * Q) // Q)[None, :]).astype(BF16)
    return tcat, negmask, rexp, ecol


def _pad_lanes(v, fill):
    out = jnp.full((v.shape[0], 1, V7X_LANES), fill, F32)
    return out.at[:, 0, :v.shape[1]].set(v.astype(F32))


def kernel(x, p, mix_norm_pre, mix_norm_post, w_in, ssd_conv_w, ssd_conv_b, ssd_dt_bias, ssd_a_log, ssd_d,
           ssd_norm, pool_w, pool_scale, w_out, ffn_norm_pre, ffn_norm_post, w_ffn_up, ffn_conv_w, ffn_conv_b,
           w_ffn_down, ple_norm_pre, w_ple_gate, w_ple, ple_norm_post):
    B, L, D = x.shape
    depth = w_in.shape[0]
    d_ssd = SSD_HEADS * SSD_HEAD_DIM
    n_bc = 2 * SSD_GROUPS * SSD_STATE
    o_dt = 2 * d_ssd + n_bc
    o_u = o_dt + 2 * SSD_HEADS
    d_pool = w_in.shape[2] - o_u
    tcat, negmask, rexp, ecol = _ssd_constants()
    row = lambda v: v.reshape(1, -1).astype(F32)
    lane_pad = jnp.zeros((D, V7X_LANES - SSD_HEADS), F32)

    outs = []
    for b in range(B):
        h = x[b]
        for i in range(depth):
            wi = w_in[i]
            w_main = wi[:, 0:o_dt].astype(BF16)
            w_tail = jnp.concatenate([wi[:, o_u:], wi[:, o_dt:o_dt + SSD_HEADS], lane_pad,
                                      wi[:, o_dt + SSD_HEADS:o_u], lane_pad], axis=1).astype(BF16)
            xs, bc, z, u, dt_raw, w_down16, w_out16, w_gate16 = _inproj(
                h, row(mix_norm_pre[i]), w_main, w_tail, ssd_conv_w[i].astype(F32), row(ssd_conv_b[i]),
                (d_ssd, n_bc, d_ssd, d_pool, 2 * V7X_LANES), w_ffn_down[i], w_out[i], w_ple_gate[i], tm=INPROJ_ROWS)
            y_lo, y_hi, w_up16 = _ssd(
                xs, bc, dt_raw, z, _pad_lanes(ssd_dt_bias[i], 0.0), _pad_lanes(ssd_a_log[i], NEG_BIG),
                jnp.repeat(ssd_d[i].astype(F32), SSD_HEAD_DIM).reshape(1, -1), row(ssd_norm[i]),
                tcat, negmask, rexp, ecol, w_ffn_up[i])
            h, hn = _mixout(u, y_lo, y_hi, h, pool_w[i].astype(BF16), row(pool_scale[i]), w_out16,
                            row(mix_norm_post[i]), row(ffn_norm_pre[i]), tm=MIXOUT_ROWS)
            r = _ffn(hn, w_up16, ffn_conv_w[i].astype(F32), row(ffn_conv_b[i]), w_down16, row(ffn_norm_post[i]),
                     tm=FFN_ROWS, tf=FFN_COLS)
            h = _ple(h, r, p[i, b], row(ple_norm_pre[i]), w_gate16, w_ple[i].astype(BF16),
                     row(ple_norm_post[i]), tm=PLE_ROWS)
        outs.append(h)
    return jnp.stack(outs)
```

```python
import functools

import jax
import jax.numpy as jnp
from jax import lax
from jax.experimental import pallas as pl
from jax.experimental.pallas import tpu as pltpu

F32 = jnp.float32
BF16 = jnp.bfloat16
EPS = 1e-6

V7X_VMEM_BYTES = 64 * 1024 * 1024
V7X_LANES = 128
V7X_SUBLANES = 8
BF16_ROWS = 16

SSD_HEAD_DIM = 64
SSD_HEADS = 16
SSD_GROUPS = 2
SSD_HEADS_PER_GROUP = SSD_HEADS // SSD_GROUPS
SSD_STATE = 128
SSD_CONV = 5
SSD_CHUNK = 128
POOL_WINDOWS = (2, 4, 8, 16)
FFN_CONV = 3
NEG_BIG = -1e30

INPROJ_ROWS = 512
MIXOUT_ROWS = 512
FFN_ROWS = 1024
FFN_COLS = 512
PLE_ROWS = 512
CONV_PIECE = 256


def _vmem_limit(nbytes):
    return int(min(nbytes, V7X_VMEM_BYTES - 6 * 1024 * 1024))


def _rms(x):
    return x * lax.rsqrt(jnp.mean(x * x, axis=-1, keepdims=True) + EPS)


def _split3(x):
    hi = x.astype(BF16).astype(F32)
    r1 = x - hi
    mid = r1.astype(BF16).astype(F32)
    lo = (r1 - mid).astype(BF16).astype(F32)
    return hi, mid, lo


def _const_spec(shape):
    nd = len(shape)
    return pl.BlockSpec(shape, lambda *_: (0,) * nd, pipeline_mode=pl.Buffered(1))


def _halo_specs(rows, halo, total_rows, ncols):
    hb = rows // halo
    nhb = total_rows // halo
    return [
        pl.BlockSpec((halo, ncols), lambda i, *_: (jnp.maximum(i * hb - 1, 0), 0)),
        pl.BlockSpec((rows, ncols), lambda i, *_: (i, 0)),
        pl.BlockSpec((halo, ncols), lambda i, *_: (jnp.minimum(i * hb + hb, nhb - 1), 0)),
    ]


def _inproj_kernel(xp_ref, xm_ref, xn_ref, g_ref, w_ref, cw_ref, cb_ref, wd_ref, wo_ref, wg_ref,
                   xs_ref, bc_ref, z_ref, u_ref, dt_ref, wd16_ref, wo16_ref, wg16_ref,
                   hn_ref, *ext_refs):
    i = pl.program_id(0)
    n = pl.num_programs(0)
    tm = xm_ref.shape[0]
    H = BF16_ROWS
    g = g_ref[...]
    hn_ref[0:H, :] = jnp.where(i > 0, _rms(xp_ref[...]) * g, 0.0).astype(BF16)
    hn_ref[H:H + tm, :] = (_rms(xm_ref[...]) * g).astype(BF16)
    hn_ref[H + tm:2 * H + tm, :] = jnp.where(i < n - 1, _rms(xn_ref[...]) * g, 0.0).astype(BF16)
    for src, dst in ((wd_ref, wd16_ref), (wo_ref, wo16_ref), (wg_ref, wg16_ref)):
        dst[...] = src[...].astype(BF16)

    cs = CONV_PIECE
    nz, nxs, nbc, nu = z_ref.shape[1], xs_ref.shape[1], bc_ref.shape[1], u_ref.shape[1]
    o_dt = nz + nxs + nbc
    n_dt = w_ref.shape[1] - o_dt - nu
    conv_outs = [(xs_ref, c, nz + c, c) for c in range(0, nxs, cs)]
    conv_outs += [(bc_ref, c, nz + nxs + c, nxs + c) for c in range(0, nbc, cs)]

    def project(c0, c1):
        return jnp.dot(hn_ref[H:H + tm, :], w_ref[:, c0:c1], preferred_element_type=F32)

    def z_chunk(c):
        z_ref[:, c:c + cs] = project(c, c + cs)

    tail = []
    split = o_dt + 2 * cs
    plain = [functools.partial(z_chunk, c) for c in range(0, nz, cs)]
    plain += [lambda: tail.append(project(o_dt, split)), lambda: tail.append(project(split, w_ref.shape[1]))]

    half = SSD_CONV // 2
    for k, (o_ref, c0, wc, cc) in enumerate(conv_outs):
        ext_ref = ext_refs[k % len(ext_refs)]
        ext_ref[...] = jnp.dot(hn_ref[...], w_ref[:, wc:wc + cs], preferred_element_type=F32)
        if k < len(plain):
            plain[k]()
        acc = cw_ref[0:1, cc:cc + cs] * ext_ref[H - half:H - half + tm, :]
        for j in range(1, SSD_CONV):
            acc = acc + cw_ref[j:j + 1, cc:cc + cs] * ext_ref[H - half + j:H - half + j + tm, :]
        acc = acc + cb_ref[:, cc:cc + cs]
        o_ref[:, c0:c0 + cs] = (acc * jax.nn.sigmoid(acc)).astype(o_ref.dtype)
    for fn in plain[len(conv_outs):]:
        fn()
    dt_ref[...] = tail[0][:, 0:dt_ref.shape[1]]
    u_ref[...] = jnp.concatenate([tail[0][:, n_dt:], tail[1]], axis=1)


def _inproj(x, g, w, cw, cb, out_cols, w_down, w_out, w_gate, *, tm):
    L, D = x.shape
    H = BF16_ROWS
    n = L // tm
    dts = (BF16, BF16, F32, F32, F32)
    row = lambda c: pl.BlockSpec((tm, c), lambda i: (i, 0))
    slab = lambda a: pl.BlockSpec((a.shape[0] // n, a.shape[1]), lambda i: (i, 0))
    casts = (w_down, w_out, w_gate)
    est = (2 * (tm + 2 * H) * D * 4 + w.size * 2 + 2 * tm * sum(out_cols) * 4
           + (tm + 2 * H) * D * 2 + 4 * (tm + 2 * H) * CONV_PIECE * 4 + 2 * sum(a.size for a in casts) // n * 6
           + 16 * tm * CONV_PIECE * 4)
    return pl.pallas_call(
        _inproj_kernel,
        grid=(n,),
        in_specs=_halo_specs(tm, H, L, D) + [_const_spec(a.shape) for a in (g, w, cw, cb)]
                 + [slab(a) for a in casts],
        out_specs=[row(c) for c in out_cols] + [slab(a) for a in casts],
        out_shape=[jax.ShapeDtypeStruct((L, c), dt) for c, dt in zip(out_cols, dts)]
                  + [jax.ShapeDtypeStruct(a.shape, BF16) for a in casts],
        scratch_shapes=[pltpu.VMEM((tm + 2 * H, D), BF16)]
                       + [pltpu.VMEM((tm + 2 * H, CONV_PIECE), F32) for _ in range(2)],
        compiler_params=pltpu.CompilerParams(dimension_semantics=("arbitrary",), vmem_limit_bytes=_vmem_limit(est)),
        name="inproj",
    )(x, x, x, g, w, cw, cb, *casts)


SSD_PACK = 16


LOG2E = 1.4426950408889634


def _pack3(v, head_lanes):
    hi, mid, lo = _split3(jnp.where(head_lanes, v, jnp.zeros_like(v)))
    packed = hi + pltpu.roll(mid, SSD_PACK, axis=1) + pltpu.roll(lo, 2 * SSD_PACK, axis=1)
    return packed.astype(BF16)


def _scan_both(xs_refs, bc_refs, dt_refs, dtb_ref, alog_ref, t_ref, nm_ref, rexp_ref, ecol_ref, h_ref):
    Q = SSD_CHUNK
    P = SSD_HEAD_DIM
    E = SSD_HEADS_PER_GROUP
    GW = E * P
    DIRS = (0, 1)
    last = (Q - 1, 0)
    xs16 = [xs_refs[d][...] for d in DIRS]
    bc =[bc_refs[d][...] for d in DIRS]

    lane = lax.broadcasted_iota(jnp.int32, (Q, V7X_LANES), 1)
    head_lanes = [jnp.logical_and(lane >= d * SSD_HEADS, lane < (d + 1) * SSD_HEADS) for d in DIRS]
    a = -jnp.exp(alog_ref[...])
    dt = [jax.nn.softplus(dt_refs[d][...] + dtb_ref[...]) for d in DIRS]
    da = [dt[d] * a for d in DIRS]
    da3 = [jnp.concatenate([v.astype(BF16) for v in _split3(da[d])], axis=0) for d in DIRS]
    acum = [jnp.dot(t_ref[d], da3[d], preferred_element_type=F32) * LOG2E for d in DIRS]
    tot = [acum[d][last[d]:last[d] + 1, :] for d in DIRS]
    src_t = [(acum[d] - jnp.log2(dt[d])).T for d in DIRS]
    e_a = [jnp.exp2(acum[d]) for d in DIRS]
    w_end = [jnp.exp2(tot[d] - acum[d]) for d in DIRS]

    pack_w = [_pack3(dt[d] * w_end[d], head_lanes[d]) for d in DIRS]
    pack_e = [_pack3(e_a[d], head_lanes[d]) for d in DIRS]
    pack_a = [_pack3(acum[d], head_lanes[d]) for d in DIRS]
    negmask = [nm_ref[d] for d in DIRS]
    first_head = lax.broadcasted_iota(jnp.int32, (Q, 2 * P), 1) < P

    y_groups = [[], []]
    for g in range(SSD_GROUPS):
        c0 = g * GW
        bm = [bc[d][:, g * SSD_STATE:(g + 1) * SSD_STATE] for d in DIRS]
        cm = [bc[d][:, (SSD_GROUPS + g) * SSD_STATE:(SSD_GROUPS + g + 1) * SSD_STATE] for d in DIRS]
        cb = [lax.dot_general(cm[d], bm[d], (((1,), (1,)), ((), ())), preferred_element_type=F32)
              for d in DIRS]
        dtw_x = [jnp.dot(pack_w[d], rexp_ref[d, :, c0:c0 + GW], preferred_element_type=F32) for d in DIRS]
        xw = [(xs16[d][:, c0:c0 + GW].astype(F32) * dtw_x[d]).astype(BF16) for d in DIRS]
        st = [lax.dot_general(bm[d], xw[d], (((0,), (0,)), ((), ())), preferred_element_type=F32)
              for d in DIRS]
        ea_x = [jnp.dot(pack_e[d], rexp_ref[d, :, c0:c0 + GW], preferred_element_type=F32) for d in DIRS]
        h_in = [h_ref[d, g] for d in DIRS]
        y_off = [jnp.dot(cm[d], h_in[d].astype(BF16), preferred_element_type=F32) for d in DIRS]
        y_g = [y_off[d] * ea_x[d] for d in DIRS]
        for d in DIRS:
            h_ref[d, g] = h_in[d] * ea_x[d][last[d]:last[d] + 1, :] + st[d]
        acol = [jnp.dot(pack_a[d], ecol_ref[d, :, g * E * Q:(g + 1) * E * Q], preferred_element_type=F32)
                for d in DIRS]
        pairs = [[], []]
        for hp in range(E // 2):
            for d in DIRS:
                h0 = g * E + hp * 2
                ms = []
                for k in range(2):
                    r = d * SSD_HEADS + h0 + k
                    seg = acol[d][:, (hp * 2 + k) * Q:(hp * 2 + k + 1) * Q] - src_t[d][r:r + 1, :] + negmask[d]
                    ms.append((cb[d] * jnp.exp2(seg)).astype(BF16))
                m2 = jnp.concatenate(ms, axis=1)
                xp = xs16[d][:, (g * E + hp * 2) * P:(g * E + hp * 2 + 2) * P]
                zero = jnp.zeros_like(xp)
                rhs = jnp.concatenate([jnp.where(first_head, xp, zero), jnp.where(first_head, zero, xp)], axis=0)
                pairs[d].append(jnp.dot(m2, rhs, preferred_element_type=F32))
        for d in DIRS:
            y_groups[d].append(y_g[d] + jnp.concatenate(pairs[d], axis=1))
    return [(jnp.concatenate(y_groups[d], axis=1), xs16[d]) for d in DIRS]


def _ssd_kernel(xsf_ref, bcf_ref, dtf_ref, zf_ref, xsb_ref, bcb_ref, dtb_raw_ref, zb_ref,
                dtb_ref, alog_ref, dexp_ref, nw_ref, t_ref, nm_ref, rexp_ref, ecol_ref, wup_ref,
                lo_ref, hi_ref, wup16_ref,
                y_ref, h_ref):
    Q = SSD_CHUNK
    GW = SSD_HEADS_PER_GROUP * SSD_HEAD_DIM
    i = pl.program_id(0)
    nc = pl.num_programs(0)

    wup16_ref[...] = wup_ref[...].astype(BF16)

    @pl.when(i == 0)
    def _():
        h_ref[...] = jnp.zeros_like(h_ref)

    (y_f, xs_f), (y_b, xs_b) = _scan_both((xsf_ref, xsb_ref), (bcf_ref, bcb_ref), (dtf_ref, dtb_raw_ref),
                                          dtb_ref, alog_ref, t_ref, nm_ref, rexp_ref, ecol_ref, h_ref)
    row_f = pl.multiple_of(i * Q, Q)
    row_b = pl.multiple_of((nc - 1 - i) * Q, Q)

    @pl.when(i < nc // 2)
    def _():
        y_ref[pl.ds(row_f, Q), :] = y_f
        y_ref[pl.ds(row_b, Q), :] = y_b

    def finish(y, xs, z_ref, o_ref):
        yy = y + xs.astype(F32) * dexp_ref[...]
        z = z_ref[...]
        yy = yy * (z * jax.nn.sigmoid(z))
        outs = [_rms(yy[:, g * GW:(g + 1) * GW]) for g in range(SSD_GROUPS)]
        o_ref[...] = (jnp.concatenate(outs, axis=1) * nw_ref[...]).astype(o_ref.dtype)

    @pl.when(i >= nc // 2)
    def _():
        finish(y_f + y_ref[pl.ds(row_f, Q), :], xs_f, zf_ref, hi_ref)
        finish(y_b + y_ref[pl.ds(row_b, Q), :], xs_b, zb_ref, lo_ref)


def _ssd(xs, bc, dt_raw, z, dtb, alog, dexp, nw, tcat, negmask, rexp, ecol, w_up):
    L, DS = xs.shape
    Q = SSD_CHUNK
    nc = L // Q
    hc = nc // 2
    slab = pl.BlockSpec((w_up.shape[0] // nc, w_up.shape[1]), lambda i: (i, 0))
    fwd = lambda i: i
    bwd = lambda i: nc - 1 - i
    fwd_late = lambda i: jnp.maximum(i, hc)
    bwd_late = lambda i: jnp.minimum(nc - 1 - i, hc - 1)
    in_specs = [
        pl.BlockSpec((Q, DS), lambda i: (fwd(i), 0)),
        pl.BlockSpec((Q, bc.shape[1]), lambda i: (fwd(i), 0)),
        pl.BlockSpec((Q, V7X_LANES), lambda i: (fwd(i), 0)),
        pl.BlockSpec((Q, DS), lambda i: (fwd_late(i), 0)),
        pl.BlockSpec((Q, DS), lambda i: (bwd(i), 0)),
        pl.BlockSpec((Q, bc.shape[1]), lambda i: (bwd(i), 0)),
        pl.BlockSpec((Q, V7X_LANES), lambda i: (bwd(i), 0)),
        pl.BlockSpec((Q, DS), lambda i: (bwd_late(i), 0)),
    ] + [_const_spec(a.shape) for a in (dtb, alog, dexp, nw, tcat, negmask, rexp, ecol)] + [slab]
    est = (L * DS * 4 + 2 * SSD_GROUPS * SSD_STATE * DS * 4 + 8 * Q * (DS + bc.shape[1]) * 4
           + (rexp.size + ecol.size) * 2 + 4 * Q * DS * 4 * 2 + 40 * Q * DS * 4 + 12 * w_up.size // nc)
    return pl.pallas_call(
        _ssd_kernel,
        grid=(nc,),
        in_specs=in_specs,
        out_specs=[pl.BlockSpec((Q, DS), lambda i: (bwd_late(i), 0)),
                   pl.BlockSpec((Q, DS), lambda i: (fwd_late(i) - hc, 0)), slab],
        out_shape=[jax.ShapeDtypeStruct((L // 2, DS), BF16), jax.ShapeDtypeStruct((L // 2, DS), BF16),
                   jax.ShapeDtypeStruct(w_up.shape, BF16)],
        scratch_shapes=[
            pltpu.VMEM((L, DS), F32),
            pltpu.VMEM((2, SSD_GROUPS, SSD_STATE, DS // SSD_GROUPS), F32),
        ],
        compiler_params=pltpu.CompilerParams(dimension_semantics=("arbitrary",), vmem_limit_bytes=_vmem_limit(est)),
        name="ssd",
    )(xs, bc, dt_raw, z, xs, bc, dt_raw, z, dtb, alog, dexp, nw, tcat, negmask, rexp, ecol, w_up)


def _mixout_kernel(up_ref, um_ref, un_ref, ylo_ref, yhi_ref, x_ref, pw_ref, ps_ref, wo_ref, g_ref, gn_ref,
                   o_ref, on_ref, ext_ref, *, seq):
    i = pl.program_id(0)
    n = pl.num_programs(0)
    tm = um_ref.shape[0]
    cg = um_ref.shape[1] // len(POOL_WINDOWS)
    ext_ref[0:8, :] = jnp.where(i > 0, up_ref[...], 0.0)
    ext_ref[8:8 + tm, :] = um_ref[...]
    ext_ref[8 + tm:16 + tm, :] = jnp.where(i < n - 1, un_ref[...], 0.0)
    t = i * tm + lax.broadcasted_iota(jnp.int32, (tm, cg), 0)
    pooled = []
    for gi, k in enumerate(POOL_WINDOWS):
        cols = slice(gi * cg, (gi + 1) * cg)
        lo_off = 8 - k // 2
        acc = ext_ref[lo_off:lo_off + tm, cols]
        for j in range(1, k):
            acc = acc + ext_ref[lo_off + j:lo_off + j + tm, cols]
        cnt = (jnp.minimum(t + (k - k // 2), seq) - jnp.maximum(t - k // 2, 0)).astype(F32)
        mixed = acc / cnt - um_ref[:, cols]
        yp = jnp.dot(mixed.astype(BF16), pw_ref[gi], preferred_element_type=F32) * ps_ref[:, cols]
        pooled.append(yp.astype(BF16))
    ypool = jnp.concatenate(pooled, axis=1)
    ys = jnp.where(i < n // 2, ylo_ref[...], yhi_ref[...])
    ds = ys.shape[1]
    mix = jnp.dot(ys, wo_ref[0:ds, :], preferred_element_type=F32)
    mix = mix + jnp.dot(ypool, wo_ref[ds:, :], preferred_element_type=F32)
    h = x_ref[...] + _rms(mix) * g_ref[...]
    o_ref[...] = h
    on_ref[...] = (_rms(h) * gn_ref[...]).astype(on_ref.dtype)


def _mixout(u, y_lo, y_hi, x, pool_w, pool_scale, w_out, g, g_next, *, tm):
    L, DP = u.shape
    D = x.shape[1]
    DS = y_lo.shape[1]
    n = L // tm
    row = lambda n_: pl.BlockSpec((tm, n_), lambda i: (i, 0))
    in_specs = (_halo_specs(tm, V7X_SUBLANES, L, DP)
                + [pl.BlockSpec((tm, DS), lambda i: (jnp.minimum(i, n // 2 - 1), 0)),
                   pl.BlockSpec((tm, DS), lambda i: (jnp.maximum(i - n // 2, 0), 0)), row(D)]
                + [_const_spec(a.shape) for a in (pool_w, pool_scale, w_out, g, g_next)])
    est = (2 * tm * (DP * 4 + 2 * DS * 2 + D * 4 + D * 4 + D * 2) + (w_out.size + pool_w.size) * 2
           + (tm + 16) * DP * 4 + 8 * tm * D * 4)
    return pl.pallas_call(
        functools.partial(_mixout_kernel, seq=L),
        grid=(n,),
        in_specs=in_specs,
        out_specs=[row(D), row(D)],
        out_shape=[jax.ShapeDtypeStruct((L, D), F32), jax.ShapeDtypeStruct((L, D), BF16)],
        scratch_shapes=[pltpu.VMEM((tm + 16, DP), F32)],
        compiler_params=pltpu.CompilerParams(dimension_semantics=("arbitrary",), vmem_limit_bytes=_vmem_limit(est)),
        name="mixout",
    )(u, u, u, y_lo, y_hi, x, pool_w, pool_scale, w_out, g, g_next)


def _ffn_kernel(hp_ref, hm_ref, hx_ref, wg_ref, wv_ref, cw_ref, cb_ref, wd_ref, gpost_ref,
                o_ref, hn_ref, gate_ref):
    i = pl.program_id(0)
    f = pl.program_id(1)
    n = pl.num_programs(0)
    nf = pl.num_programs(1)
    tm = hm_ref.shape[0]
    H = BF16_ROWS

    @pl.when(f == 0)
    def _():
        hn_ref[0:H, :] = jnp.where(i > 0, hp_ref[...], jnp.zeros_like(hp_ref))
        hn_ref[H:H + tm, :] = hm_ref[...]
        hn_ref[H + tm:2 * H + tm, :] = jnp.where(i < n - 1, hx_ref[...], jnp.zeros_like(hx_ref))
        o_ref[...] = jnp.zeros_like(o_ref)

    gate_ref[...] = jnp.dot(hn_ref[...], wg_ref[...], preferred_element_type=F32)
    val = jnp.dot(hm_ref[...], wv_ref[...], preferred_element_type=F32)
    half = FFN_CONV // 2
    gc = cw_ref[0:1, :] * gate_ref[H - half:H - half + tm, :]
    for j in range(1, FFN_CONV):
        gc = gc + cw_ref[j:j + 1, :] * gate_ref[H - half + j:H - half + j + tm, :]
    gc = gc + cb_ref[...]
    act = (jax.nn.gelu(gc, approximate=True) * val).astype(BF16)
    o_ref[...] += jnp.dot(act, wd_ref[...], preferred_element_type=F32)

    @pl.when(f == nf - 1)
    def _():
        o_ref[...] = _rms(o_ref[...]) * gpost_ref[...]


def _ffn(hn, w_up, cw, cb, w_down, gpost, *, tm, tf):
    L, D = hn.shape
    DF = w_down.shape[0]
    nf = DF // tf
    H = BF16_ROWS
    in_specs = _halo_specs(tm, H, L, D) + [
        pl.BlockSpec((D, tf), lambda i, f: (0, f)),
        pl.BlockSpec((D, tf), lambda i, f: (0, nf + f)),
        pl.BlockSpec((FFN_CONV, tf), lambda i, f: (0, f)),
        pl.BlockSpec((1, tf), lambda i, f: (0, f)),
        pl.BlockSpec((tf, D), lambda i, f: (f, 0)),
        pl.BlockSpec(gpost.shape, lambda i, f: (0, 0)),
    ]
    est = (2 * (tm + 2 * H) * D * 2 + 2 * tm * D * 4 + (tm + 2 * H) * D * 2 + (tm + 2 * H) * tf * 4
           + 2 * 3 * D * tf * 2 + 8 * tm * tf * 4 + tm * D * 4)
    return pl.pallas_call(
        _ffn_kernel,
        grid=(L // tm, nf),
        in_specs=in_specs,
        out_specs=pl.BlockSpec((tm, D), lambda i, f: (i, 0)),
        out_shape=jax.ShapeDtypeStruct((L, D), F32),
        scratch_shapes=[
            pltpu.VMEM((tm + 2 * H, D), BF16),
            pltpu.VMEM((tm + 2 * H, tf), F32),
        ],
        compiler_params=pltpu.CompilerParams(dimension_semantics=("arbitrary", "arbitrary"),
                                             vmem_limit_bytes=_vmem_limit(est)),
        name="ffn",
    )(hn, hn, hn, w_up, w_up, cw, cb, w_down, gpost)


def _ple_kernel(h_ref, r_ref, p_ref, gpre_ref, wg_ref, wp_ref, gpost_ref, o_ref):
    h = h_ref[...] + r_ref[...]
    hn = (_rms(h) * gpre_ref[...]).astype(BF16)
    gate = jax.nn.sigmoid(jnp.dot(hn, wg_ref[...], preferred_element_type=F32))
    pe = jnp.dot(p_ref[...].astype(BF16), wp_ref[...], preferred_element_type=F32)
    o_ref[...] = h + _rms(gate * pe) * gpost_ref[...]


def _ple(h, r, p, gpre, w_gate, w_ple, gpost, *, tm):
    L, D = h.shape
    row = lambda n: pl.BlockSpec((tm, n), lambda i: (i, 0))
    est = 2 * tm * (3 * D + p.shape[1]) * 4 + (w_gate.size + w_ple.size) * 2 + 8 * tm * D * 4
    return pl.pallas_call(
        _ple_kernel,
        grid=(L // tm,),
        in_specs=[row(D), row(D), row(p.shape[1])] + [_const_spec(a.shape) for a in (gpre, w_gate, w_ple, gpost)],
        out_specs=row(D),
        out_shape=jax.ShapeDtypeStruct((L, D), F32),
        compiler_params=pltpu.CompilerParams(dimension_semantics=("arbitrary",), vmem_limit_bytes=_vmem_limit(est)),
        name="ple",
    )(h, r, p, gpre, w_gate, w_ple, gpost)


def _ssd_constants():
    Q = SSD_CHUNK
    r = jnp.arange(Q)
    lower = (r[:, None] >= r[None, :])
    tri = jnp.stack([lower, lower.T]).astype(BF16)
    tcat = jnp.concatenate([tri, tri, tri], axis=2)
    negmask = jnp.where(jnp.stack([lower, lower.T]), 0.0, NEG_BIG).astype(F32)
    k = jnp.arange(V7X_LANES)
    rexp, ecol = [], []
    for d in range(2):
        packed = jnp.logical_and(k >= d * SSD_HEADS, k < d * SSD_HEADS + 3 * SSD_PACK)
        head_of_lane = jnp.where(packed, (k - d * SSD_HEADS) % SSD_PACK, -1)
        rexp.append(head_of_lane[:, None] == (jnp.arange(SSD_HEADS * SSD_HEAD_DIM) // SSD_HEAD_DIM)[None, :])
        ecol.append(head_of_lane[:, None] == (jnp.arange(SSD_HEADS * Q) // Q)[None, :])
    return tcat, negmask, jnp.stack(rexp).astype(BF16), jnp.stack(ecol).astype(BF16)


def _dir_lanes(v, fill):
    out = jnp.full((1, V7X_LANES), fill, F32)
    return out.at[0, :v.size].set(v.astype(F32).reshape(-1))


def kernel(x, p, mix_norm_pre, mix_norm_post, w_in, ssd_conv_w, ssd_conv_b, ssd_dt_bias, ssd_a_log, ssd_d,
           ssd_norm, pool_w, pool_scale, w_out, ffn_norm_pre, ffn_norm_post, w_ffn_up, ffn_conv_w, ffn_conv_b,
           w_ffn_down, ple_norm_pre, w_ple_gate, w_ple, ple_norm_post):
    B, L, D = x.shape
    depth = w_in.shape[0]
    d_ssd = SSD_HEADS * SSD_HEAD_DIM
    n_bc = 2 * SSD_GROUPS * SSD_STATE
    o_dt = 2 * d_ssd + n_bc
    o_u = o_dt + 2 * SSD_HEADS
    d_pool = w_in.shape[2] - o_u
    tcat, negmask, rexp, ecol = _ssd_constants()
    row = lambda v: v.reshape(1, -1).astype(F32)

    outs = []
    for b in range(B):
        h = x[b]
        for i in range(depth):
            xs, bc, z, u, dt_raw, w_down16, w_out16, w_gate16 = _inproj(
                h, row(mix_norm_pre[i]), w_in[i].astype(BF16), ssd_conv_w[i].astype(F32), row(ssd_conv_b[i]),
                (d_ssd, n_bc, d_ssd, d_pool, V7X_LANES), w_ffn_down[i], w_out[i], w_ple_gate[i], tm=INPROJ_ROWS)
            y_lo, y_hi, w_up16 = _ssd(
                xs, bc, dt_raw, z, _dir_lanes(ssd_dt_bias[i], 0.0), _dir_lanes(ssd_a_log[i], NEG_BIG),
                jnp.repeat(ssd_d[i].astype(F32), SSD_HEAD_DIM).reshape(1, -1), row(ssd_norm[i]),
                tcat, negmask, rexp, ecol, w_ffn_up[i])
            h, hn = _mixout(u, y_lo, y_hi, h, pool_w[i].astype(BF16), row(pool_scale[i]), w_out16,
                            row(mix_norm_post[i]), row(ffn_norm_pre[i]), tm=MIXOUT_ROWS)
            r = _ffn(hn, w_up16, ffn_conv_w[i].astype(F32), row(ffn_conv_b[i]), w_down16, row(ffn_norm_post[i]),
                     tm=FFN_ROWS, tf=FFN_COLS)
            h = _ple(h, r, p[i, b], row(ple_norm_pre[i]), w_gate16, w_ple[i].astype(BF16),
                     row(ple_norm_post[i]), tm=PLE_ROWS)
        outs.append(h)
    return jnp.stack(outs)
```

```python
import functools

import jax
import jax.numpy as jnp
from jax import lax
from jax.experimental import pallas as pl
from jax.experimental.pallas import tpu as pltpu

F32 = jnp.float32
BF16 = jnp.bfloat16
EPS = 1e-6

V7X_VMEM_BYTES = 64 * 1024 * 1024
V7X_LANES = 128
V7X_SUBLANES = 8
BF16_ROWS = 16

SSD_HEAD_DIM = 64
SSD_HEADS = 16
SSD_GROUPS = 2
SSD_HEADS_PER_GROUP = SSD_HEADS // SSD_GROUPS
SSD_STATE = 128
SSD_CONV = 5
SSD_CHUNK = 128
POOL_WINDOWS = (2, 4, 8, 16)
FFN_CONV = 3
NEG_BIG = -1e30

INPROJ_ROWS = 512
MIXOUT_ROWS = 512
FFN_ROWS = 1024
FFN_COLS = 512
PLE_ROWS = 512
CONV_PIECE = 256


def _vmem_limit(nbytes):
    return int(min(nbytes, V7X_VMEM_BYTES - 6 * 1024 * 1024))


def _rms(x):
    return x * lax.rsqrt(jnp.mean(x * x, axis=-1, keepdims=True) + EPS)


def _split3(x):
    hi = x.astype(BF16).astype(F32)
    r1 = x - hi
    mid = r1.astype(BF16).astype(F32)
    lo = (r1 - mid).astype(BF16).astype(F32)
    return hi, mid, lo


def _const_spec(shape):
    nd = len(shape)
    return pl.BlockSpec(shape, lambda *_: (0,) * nd, pipeline_mode=pl.Buffered(1))


def _halo_specs(rows, halo, total_rows, ncols):
    hb = rows // halo
    nhb = total_rows // halo
    return [
        pl.BlockSpec((halo, ncols), lambda i, *_: (jnp.maximum(i * hb - 1, 0), 0)),
        pl.BlockSpec((rows, ncols), lambda i, *_: (i, 0)),
        pl.BlockSpec((halo, ncols), lambda i, *_: (jnp.minimum(i * hb + hb, nhb - 1), 0)),
    ]


def _inproj_kernel(xp_ref, xm_ref, xn_ref, g_ref, w_ref, cw_ref, cb_ref, wd_ref, wo_ref, wg_ref,
                   xs_ref, bc_ref, z_ref, u_ref, dt_ref, wd16_ref, wo16_ref, wg16_ref,
                   hn_ref, *ext_refs):
    i = pl.program_id(0)
    n = pl.num_programs(0)
    tm = xm_ref.shape[0]
    H = BF16_ROWS
    g = g_ref[...]
    hn_ref[0:H, :] = jnp.where(i > 0, _rms(xp_ref[...]) * g, 0.0).astype(BF16)
    hn_ref[H:H + tm, :] = (_rms(xm_ref[...]) * g).astype(BF16)
    hn_ref[H + tm:2 * H + tm, :] = jnp.where(i < n - 1, _rms(xn_ref[...]) * g, 0.0).astype(BF16)
    for src, dst in ((wd_ref, wd16_ref), (wo_ref, wo16_ref), (wg_ref, wg16_ref)):
        dst[...] = src[...].astype(BF16)

    cs = CONV_PIECE
    nz, nxs, nbc, nu = z_ref.shape[1], xs_ref.shape[1], bc_ref.shape[1], u_ref.shape[1]
    o_dt = nz + nxs + nbc
    n_dt = w_ref.shape[1] - o_dt - nu
    conv_outs = [(xs_ref, c, nz + c, c) for c in range(0, nxs, cs)]
    conv_outs += [(bc_ref, c, nz + nxs + c, nxs + c) for c in range(0, nbc, cs)]

    def project(c0, c1):
        return jnp.dot(hn_ref[H:H + tm, :], w_ref[:, c0:c1], preferred_element_type=F32)

    def z_chunk(c):
        z_ref[:, c:c + cs] = project(c, c + cs)

    tail = []
    split = o_dt + 2 * cs
    plain = [functools.partial(z_chunk, c) for c in range(0, nz, cs)]
    plain += [lambda: tail.append(project(o_dt, split)), lambda: tail.append(project(split, w_ref.shape[1]))]

    half = SSD_CONV // 2
    for k, (o_ref, c0, wc, cc) in enumerate(conv_outs):
        ext_ref = ext_refs[k % len(ext_refs)]
        ext_ref[...] = jnp.dot(hn_ref[...], w_ref[:, wc:wc + cs], preferred_element_type=F32)
        if k < len(plain):
            plain[k]()
        acc = cw_ref[0:1, cc:cc + cs] * ext_ref[H - half:H - half + tm, :]
        for j in range(1, SSD_CONV):
            acc = acc + cw_ref[j:j + 1, cc:cc + cs] * ext_ref[H - half + j:H - half + j + tm, :]
        acc = acc + cb_ref[:, cc:cc + cs]
        o_ref[:, c0:c0 + cs] = (acc * jax.nn.sigmoid(acc)).astype(o_ref.dtype)
    for fn in plain[len(conv_outs):]:
        fn()
    dt_ref[...] = tail[0][:, 0:dt_ref.shape[1]]
    u_ref[...] = jnp.concatenate([tail[0][:, n_dt:], tail[1]], axis=1)


def _inproj(x, g, w, cw, cb, out_cols, w_down, w_out, w_gate, *, tm):
    L, D = x.shape
    H = BF16_ROWS
    n = L // tm
    dts = (BF16, BF16, F32, F32, F32)
    row = lambda c: pl.BlockSpec((tm, c), lambda i: (i, 0))
    slab = lambda a: pl.BlockSpec((a.shape[0] // n, a.shape[1]), lambda i: (i, 0))
    casts = (w_down, w_out, w_gate)
    est = (2 * (tm + 2 * H) * D * 4 + w.size * 2 + 2 * tm * sum(out_cols) * 4
           + (tm + 2 * H) * D * 2 + 4 * (tm + 2 * H) * CONV_PIECE * 4 + 2 * sum(a.size for a in casts) // n * 6
           + 16 * tm * CONV_PIECE * 4)
    return pl.pallas_call(
        _inproj_kernel,
        grid=(n,),
        in_specs=_halo_specs(tm, H, L, D) + [_const_spec(a.shape) for a in (g, w, cw, cb)]
                 + [slab(a) for a in casts],
        out_specs=[row(c) for c in out_cols] + [slab(a) for a in casts],
        out_shape=[jax.ShapeDtypeStruct((L, c), dt) for c, dt in zip(out_cols, dts)]
                  + [jax.ShapeDtypeStruct(a.shape, BF16) for a in casts],
        scratch_shapes=[pltpu.VMEM((tm + 2 * H, D), BF16)]
                       + [pltpu.VMEM((tm + 2 * H, CONV_PIECE), F32) for _ in range(2)],
        compiler_params=pltpu.CompilerParams(dimension_semantics=("arbitrary",), vmem_limit_bytes=_vmem_limit(est)),
        name="inproj",
    )(x, x, x, g, w, cw, cb, *casts)


SSD_PACK = 16


LOG2E = 1.4426950408889634


def _pack3(v, head_lanes):
    hi, mid, lo = _split3(jnp.where(head_lanes, v, jnp.zeros_like(v)))
    packed = hi + pltpu.roll(mid, SSD_PACK, axis=1) + pltpu.roll(lo, 2 * SSD_PACK, axis=1)
    return packed.astype(BF16)


def _scan_both(xs_refs, bc_refs, dt_refs, dtb_ref, alog_ref, t_ref, nm_ref, rexp_ref, ecol_ref, h_ref):
    Q = SSD_CHUNK
    P = SSD_HEAD_DIM
    E = SSD_HEADS_PER_GROUP
    GW = E * P
    DIRS = (0, 1)
    last = (Q - 1, 0)
    xs16 = [xs_refs[d][...] for d in DIRS]
    bc =[bc_refs[d][...] for d in DIRS]

    lane = lax.broadcasted_iota(jnp.int32, (Q, V7X_LANES), 1)
    head_lanes = [jnp.logical_and(lane >= d * SSD_HEADS, lane < (d + 1) * SSD_HEADS) for d in DIRS]
    a = -jnp.exp(alog_ref[...])
    dt = [jax.nn.softplus(dt_refs[d][...] + dtb_ref[...]) for d in DIRS]
    da = [dt[d] * a for d in DIRS]
    da3 = [jnp.concatenate([v.astype(BF16) for v in _split3(da[d])], axis=0) for d in DIRS]
    acum = [jnp.dot(t_ref[d], da3[d], preferred_element_type=F32) * LOG2E for d in DIRS]
    tot = [acum[d][last[d]:last[d] + 1, :] for d in DIRS]
    src_t = [(acum[d] - jnp.log2(dt[d])).T for d in DIRS]
    e_a = [jnp.exp2(acum[d]) for d in DIRS]
    w_end = [jnp.exp2(tot[d] - acum[d]) for d in DIRS]

    lhs = [jnp.concatenate([_pack3(v, head_lanes[d]) for v in (dt[d] * w_end[d], e_a[d])], axis=0)
           for d in DIRS]
    exp2x = [jnp.dot(lhs[d], rexp_ref[d], preferred_element_type=F32) for d in DIRS]
    acol = [jnp.dot(_pack3(acum[d], head_lanes[d]), ecol_ref[d], preferred_element_type=F32) for d in DIRS]
    xw = [(xs16[d].astype(F32) * exp2x[d][0:Q]).astype(BF16) for d in DIRS]
    ea_x = [exp2x[d][Q:2 * Q] for d in DIRS]
    negmask = [nm_ref[d] for d in DIRS]
    first_head = lax.broadcasted_iota(jnp.int32, (Q, 2 * P), 1) < P

    y_groups = [[], []]
    for g in range(SSD_GROUPS):
        c0 = g * GW
        bm = [bc[d][:, g * SSD_STATE:(g + 1) * SSD_STATE] for d in DIRS]
        cm = [bc[d][:, (SSD_GROUPS + g) * SSD_STATE:(SSD_GROUPS + g + 1) * SSD_STATE] for d in DIRS]
        cb = [lax.dot_general(cm[d], bm[d], (((1,), (1,)), ((), ())), preferred_element_type=F32)
              for d in DIRS]
        st = [lax.dot_general(bm[d], xw[d][:, c0:c0 + GW], (((0,), (0,)), ((), ())), preferred_element_type=F32)
              for d in DIRS]
        h_in = [h_ref[d, g] for d in DIRS]
        y_off = [jnp.dot(cm[d], h_in[d].astype(BF16), preferred_element_type=F32) for d in DIRS]
        y_g = [y_off[d] * ea_x[d][:, c0:c0 + GW] for d in DIRS]
        for d in DIRS:
            h_ref[d, g] = h_in[d] * ea_x[d][last[d]:last[d] + 1, c0:c0 + GW] + st[d]
        pairs = [[], []]
        for hp in range(E // 2):
            for d in DIRS:
                h0 = g * E + hp * 2
                ms = []
                for k in range(2):
                    r = d * SSD_HEADS + h0 + k
                    seg = acol[d][:, (h0 + k) * Q:(h0 + k + 1) * Q] - src_t[d][r:r + 1, :] + negmask[d]
                    ms.append((cb[d] * jnp.exp2(seg)).astype(BF16))
                m2 = jnp.concatenate(ms, axis=1)
                xp = xs16[d][:, (g * E + hp * 2) * P:(g * E + hp * 2 + 2) * P]
                zero = jnp.zeros_like(xp)
                rhs = jnp.concatenate([jnp.where(first_head, xp, zero), jnp.where(first_head, zero, xp)], axis=0)
                pairs[d].append(jnp.dot(m2, rhs, preferred_element_type=F32))
        for d in DIRS:
            y_groups[d].append(y_g[d] + jnp.concatenate(pairs[d], axis=1))
    return [(jnp.concatenate(y_groups[d], axis=1), xs16[d]) for d in DIRS]


def _ssd_kernel(xsf_ref, bcf_ref, dtf_ref, zf_ref, xsb_ref, bcb_ref, dtb_raw_ref, zb_ref,
                dtb_ref, alog_ref, dexp_ref, nw_ref, t_ref, nm_ref, rexp_ref, ecol_ref, wup_ref,
                lo_ref, hi_ref, wup16_ref,
                y_ref, h_ref):
    Q = SSD_CHUNK
    GW = SSD_HEADS_PER_GROUP * SSD_HEAD_DIM
    i = pl.program_id(0)
    nc = pl.num_programs(0)

    wup16_ref[...] = wup_ref[...].astype(BF16)

    @pl.when(i == 0)
    def _():
        h_ref[...] = jnp.zeros_like(h_ref)

    (y_f, xs_f), (y_b, xs_b) = _scan_both((xsf_ref, xsb_ref), (bcf_ref, bcb_ref), (dtf_ref, dtb_raw_ref),
                                          dtb_ref, alog_ref, t_ref, nm_ref, rexp_ref, ecol_ref, h_ref)
    row_f = pl.multiple_of(i * Q, Q)
    row_b = pl.multiple_of((nc - 1 - i) * Q, Q)

    @pl.when(i < nc // 2)
    def _():
        y_ref[pl.ds(row_f, Q), :] = y_f
        y_ref[pl.ds(row_b, Q), :] = y_b

    def finish(y, xs, z_ref, o_ref):
        yy = y + xs.astype(F32) * dexp_ref[...]
        z = z_ref[...]
        yy = yy * (z * jax.nn.sigmoid(z))
        outs = [_rms(yy[:, g * GW:(g + 1) * GW]) for g in range(SSD_GROUPS)]
        o_ref[...] = (jnp.concatenate(outs, axis=1) * nw_ref[...]).astype(o_ref.dtype)

    @pl.when(i >= nc // 2)
    def _():
        finish(y_f + y_ref[pl.ds(row_f, Q), :], xs_f, zf_ref, hi_ref)
        finish(y_b + y_ref[pl.ds(row_b, Q), :], xs_b, zb_ref, lo_ref)


def _ssd(xs, bc, dt_raw, z, dtb, alog, dexp, nw, tcat, negmask, rexp, ecol, w_up):
    L, DS = xs.shape
    Q = SSD_CHUNK
    nc = L // Q
    hc = nc // 2
    slab = pl.BlockSpec((w_up.shape[0] // nc, w_up.shape[1]), lambda i: (i, 0))
    fwd = lambda i: i
    bwd = lambda i: nc - 1 - i
    fwd_late = lambda i: jnp.maximum(i, hc)
    bwd_late = lambda i: jnp.minimum(nc - 1 - i, hc - 1)
    in_specs = [
        pl.BlockSpec((Q, DS), lambda i: (fwd(i), 0)),
        pl.BlockSpec((Q, bc.shape[1]), lambda i: (fwd(i), 0)),
        pl.BlockSpec((Q, V7X_LANES), lambda i: (fwd(i), 0)),
        pl.BlockSpec((Q, DS), lambda i: (fwd_late(i), 0)),
        pl.BlockSpec((Q, DS), lambda i: (bwd(i), 0)),
        pl.BlockSpec((Q, bc.shape[1]), lambda i: (bwd(i), 0)),
        pl.BlockSpec((Q, V7X_LANES), lambda i: (bwd(i), 0)),
        pl.BlockSpec((Q, DS), lambda i: (bwd_late(i), 0)),
    ] + [_const_spec(a.shape) for a in (dtb, alog, dexp, nw, tcat, negmask, rexp, ecol)] + [slab]
    est = (L * DS * 4 + 2 * SSD_GROUPS * SSD_STATE * DS * 4 + 8 * Q * (DS + bc.shape[1]) * 4
           + (rexp.size + ecol.size) * 2 + 4 * Q * DS * 4 * 2 + 40 * Q * DS * 4 + 12 * w_up.size // nc)
    return pl.pallas_call(
        _ssd_kernel,
        grid=(nc,),
        in_specs=in_specs,
        out_specs=[pl.BlockSpec((Q, DS), lambda i: (bwd_late(i), 0)),
                   pl.BlockSpec((Q, DS), lambda i: (fwd_late(i) - hc, 0)), slab],
        out_shape=[jax.ShapeDtypeStruct((L // 2, DS), BF16), jax.ShapeDtypeStruct((L // 2, DS), BF16),
                   jax.ShapeDtypeStruct(w_up.shape, BF16)],
        scratch_shapes=[
            pltpu.VMEM((L, DS), F32),
            pltpu.VMEM((2, SSD_GROUPS, SSD_STATE, DS // SSD_GROUPS), F32),
        ],
        compiler_params=pltpu.CompilerParams(dimension_semantics=("arbitrary",), vmem_limit_bytes=_vmem_limit(est)),
        name="ssd",
    )(xs, bc, dt_raw, z, xs, bc, dt_raw, z, dtb, alog, dexp, nw, tcat, negmask, rexp, ecol, w_up)


def _mixout_kernel(up_ref, um_ref, un_ref, ylo_ref, yhi_ref, x_ref, pw_ref, ps_ref, wo_ref, g_ref, gn_ref,
                   o_ref, on_ref, ext_ref, *, seq):
    i = pl.program_id(0)
    n = pl.num_programs(0)
    tm = um_ref.shape[0]
    cg = um_ref.shape[1] // len(POOL_WINDOWS)
    ext_ref[0:8, :] = jnp.where(i > 0, up_ref[...], 0.0)
    ext_ref[8:8 + tm, :] = um_ref[...]
    ext_ref[8 + tm:16 + tm, :] = jnp.where(i < n - 1, un_ref[...], 0.0)
    ys = jnp.where(i < n // 2, ylo_ref[...], yhi_ref[...])
    ds = ys.shape[1]
    mix = jnp.dot(ys, wo_ref[0:ds, :], preferred_element_type=F32)
    t = i * tm + lax.broadcasted_iota(jnp.int32, (tm, cg), 0)
    rows = tm + 16

    def ahead(v, k):
        return pltpu.roll(v, (rows - k) % rows, axis=0)

    pooled = []
    for gi, k in enumerate(POOL_WINDOWS):
        cols = slice(gi * cg, (gi + 1) * cg)
        e = ext_ref[:, cols]
        half = k // 2
        run, length = e, 1
        while length < half:
            run = run + ahead(run, length)
            length *= 2
        before = run[0:tm] if half == 8 else ahead(run, rows - half)[8:8 + tm]
        acc = before + run[8:8 + tm]
        cnt = (jnp.minimum(t + (k - k // 2), seq) - jnp.maximum(t - k // 2, 0)).astype(F32)
        mixed = acc / cnt - um_ref[:, cols]
        yp = jnp.dot(mixed.astype(BF16), pw_ref[gi], preferred_element_type=F32) * ps_ref[:, cols]
        pooled.append(yp.astype(BF16))
    ypool = jnp.concatenate(pooled, axis=1)
    mix = mix + jnp.dot(ypool, wo_ref[ds:, :], preferred_element_type=F32)
    h = x_ref[...] + _rms(mix) * g_ref[...]
    o_ref[...] = h
    on_ref[...] = (_rms(h) * gn_ref[...]).astype(on_ref.dtype)


def _mixout(u, y_lo, y_hi, x, pool_w, pool_scale, w_out, g, g_next, *, tm):
    L, DP = u.shape
    D = x.shape[1]
    DS = y_lo.shape[1]
    n = L // tm
    row = lambda n_: pl.BlockSpec((tm, n_), lambda i: (i, 0))
    in_specs = (_halo_specs(tm, V7X_SUBLANES, L, DP)
                + [pl.BlockSpec((tm, DS), lambda i: (jnp.minimum(i, n // 2 - 1), 0)),
                   pl.BlockSpec((tm, DS), lambda i: (jnp.maximum(i - n // 2, 0), 0)), row(D)]
                + [_const_spec(a.shape) for a in (pool_w, pool_scale, w_out, g, g_next)])
    est = (2 * tm * (DP * 4 + 2 * DS * 2 + D * 4 + D * 4 + D * 2) + (w_out.size + pool_w.size) * 2
           + (tm + 16) * DP * 4 + 8 * tm * D * 4)
    return pl.pallas_call(
        functools.partial(_mixout_kernel, seq=L),
        grid=(n,),
        in_specs=in_specs,
        out_specs=[row(D), row(D)],
        out_shape=[jax.ShapeDtypeStruct((L, D), F32), jax.ShapeDtypeStruct((L, D), BF16)],
        scratch_shapes=[pltpu.VMEM((tm + 16, DP), F32)],
        compiler_params=pltpu.CompilerParams(dimension_semantics=("arbitrary",), vmem_limit_bytes=_vmem_limit(est)),
        name="mixout",
    )(u, u, u, y_lo, y_hi, x, pool_w, pool_scale, w_out, g, g_next)


def _ffn_kernel(hp_ref, hm_ref, hx_ref, wg_ref, wv_ref, cw_ref, cb_ref, wd_ref, gpost_ref,
                o_ref, hn_ref, gate_ref):
    i = pl.program_id(0)
    f = pl.program_id(1)
    n = pl.num_programs(0)
    nf = pl.num_programs(1)
    tm = hm_ref.shape[0]
    H = BF16_ROWS

    @pl.when(f == 0)
    def _():
        hn_ref[0:H, :] = jnp.where(i > 0, hp_ref[...], jnp.zeros_like(hp_ref))
        hn_ref[H:H + tm, :] = hm_ref[...]
        hn_ref[H + tm:2 * H + tm, :] = jnp.where(i < n - 1, hx_ref[...], jnp.zeros_like(hx_ref))
        o_ref[...] = jnp.zeros_like(o_ref)

    gate_ref[...] = jnp.dot(hn_ref[...], wg_ref[...], preferred_element_type=F32)
    val = jnp.dot(hm_ref[...], wv_ref[...], preferred_element_type=F32)
    half = FFN_CONV // 2
    gc = cw_ref[0:1, :] * gate_ref[H - half:H - half + tm, :]
    for j in range(1, FFN_CONV):
        gc = gc + cw_ref[j:j + 1, :] * gate_ref[H - half + j:H - half + j + tm, :]
    gc = gc + cb_ref[...]
    act = (jax.nn.gelu(gc, approximate=True) * val).astype(BF16)
    o_ref[...] += jnp.dot(act, wd_ref[...], preferred_element_type=F32)

    @pl.when(f == nf - 1)
    def _():
        o_ref[...] = _rms(o_ref[...]) * gpost_ref[...]


def _ffn(hn, w_up, cw, cb, w_down, gpost, *, tm, tf):
    L, D = hn.shape
    DF = w_down.shape[0]
    nf = DF // tf
    H = BF16_ROWS
    in_specs = _halo_specs(tm, H, L, D) + [
        pl.BlockSpec((D, tf), lambda i, f: (0, f)),
        pl.BlockSpec((D, tf), lambda i, f: (0, nf + f)),
        pl.BlockSpec((FFN_CONV, tf), lambda i, f: (0, f)),
        pl.BlockSpec((1, tf), lambda i, f: (0, f)),
        pl.BlockSpec((tf, D), lambda i, f: (f, 0)),
        pl.BlockSpec(gpost.shape, lambda i, f: (0, 0)),
    ]
    est = (2 * (tm + 2 * H) * D * 2 + 2 * tm * D * 4 + (tm + 2 * H) * D * 2 + (tm + 2 * H) * tf * 4
           + 2 * 3 * D * tf * 2 + 8 * tm * tf * 4 + tm * D * 4)
    return pl.pallas_call(
        _ffn_kernel,
        grid=(L // tm, nf),
        in_specs=in_specs,
        out_specs=pl.BlockSpec((tm, D), lambda i, f: (i, 0)),
        out_shape=jax.ShapeDtypeStruct((L, D), F32),
        scratch_shapes=[
            pltpu.VMEM((tm + 2 * H, D), BF16),
            pltpu.VMEM((tm + 2 * H, tf), F32),
        ],
        compiler_params=pltpu.CompilerParams(dimension_semantics=("arbitrary", "arbitrary"),
                                             vmem_limit_bytes=_vmem_limit(est)),
        name="ffn",
    )(hn, hn, hn, w_up, w_up, cw, cb, w_down, gpost)


def _ple_kernel(h_ref, r_ref, p_ref, gpre_ref, wg_ref, wp_ref, gpost_ref, o_ref):
    h = h_ref[...] + r_ref[...]
    hn = (_rms(h) * gpre_ref[...]).astype(BF16)
    gate = jax.nn.sigmoid(jnp.dot(hn, wg_ref[...], preferred_element_type=F32))
    pe = jnp.dot(p_ref[...].astype(BF16), wp_ref[...], preferred_element_type=F32)
    o_ref[...] = h + _rms(gate * pe) * gpost_ref[...]


def _ple(h, r, p, gpre, w_gate, w_ple, gpost, *, tm):
    L, D = h.shape
    row = lambda n: pl.BlockSpec((tm, n), lambda i: (i, 0))
    est = 2 * tm * (3 * D + p.shape[1]) * 4 + (w_gate.size + w_ple.size) * 2 + 8 * tm * D * 4
    return pl.pallas_call(
        _ple_kernel,
        grid=(L // tm,),
        in_specs=[row(D), row(D), row(p.shape[1])] + [_const_spec(a.shape) for a in (gpre, w_gate, w_ple, gpost)],
        out_specs=row(D),
        out_shape=jax.ShapeDtypeStruct((L, D), F32),
        compiler_params=pltpu.CompilerParams(dimension_semantics=("arbitrary",), vmem_limit_bytes=_vmem_limit(est)),
        name="ple",
    )(h, r, p, gpre, w_gate, w_ple, gpost)


def _ssd_constants():
    Q = SSD_CHUNK
    r = jnp.arange(Q)
    lower = (r[:, None] >= r[None, :])
    tri = jnp.stack([lower, lower.T]).astype(BF16)
    tcat = jnp.concatenate([tri, tri, tri], axis=2)
    negmask = jnp.where(jnp.stack([lower, lower.T]), 0.0, NEG_BIG).astype(F32)
    k = jnp.arange(V7X_LANES)
    rexp, ecol = [], []
    for d in range(2):
        packed = jnp.logical_and(k >= d * SSD_HEADS, k < d * SSD_HEADS + 3 * SSD_PACK)
        head_of_lane = jnp.where(packed, (k - d * SSD_HEADS) % SSD_PACK, -1)
        rexp.append(head_of_lane[:, None] == (jnp.arange(SSD_HEADS * SSD_HEAD_DIM) // SSD_HEAD_DIM)[None, :])
        ecol.append(head_of_lane[:, None] == (jnp.arange(SSD_HEADS * Q) // Q)[None, :])
    return tcat, negmask, jnp.stack(rexp).astype(BF16), jnp.stack(ecol).astype(BF16)


def _dir_lanes(v, fill):
    out = jnp.full((1, V7X_LANES), fill, F32)
    return out.at[0, :v.size].set(v.astype(F32).reshape(-1))


def kernel(x, p, mix_norm_pre, mix_norm_post, w_in, ssd_conv_w, ssd_conv_b, ssd_dt_bias, ssd_a_log, ssd_d,
           ssd_norm, pool_w, pool_scale, w_out, ffn_norm_pre, ffn_norm_post, w_ffn_up, ffn_conv_w, ffn_conv_b,
           w_ffn_down, ple_norm_pre, w_ple_gate, w_ple, ple_norm_post):
    B, L, D = x.shape
    depth = w_in.shape[0]
    d_ssd = SSD_HEADS * SSD_HEAD_DIM
    n_bc = 2 * SSD_GROUPS * SSD_STATE
    o_dt = 2 * d_ssd + n_bc
    o_u = o_dt + 2 * SSD_HEADS
    d_pool = w_in.shape[2] - o_u
    tcat, negmask, rexp, ecol = _ssd_constants()
    row = lambda v: v.reshape(1, -1).astype(F32)

    outs = []
    for b in range(B):
        h = x[b]
        for i in range(depth):
            xs, bc, z, u, dt_raw, w_down16, w_out16, w_gate16 = _inproj(
                h, row(mix_norm_pre[i]), w_in[i].astype(BF16), ssd_conv_w[i].astype(F32), row(ssd_conv_b[i]),
                (d_ssd, n_bc, d_ssd, d_pool, V7X_LANES), w_ffn_down[i], w_out[i], w_ple_gate[i], tm=INPROJ_ROWS)
            y_lo, y_hi, w_up16 = _ssd(
                xs, bc, dt_raw, z, _dir_lanes(ssd_dt_bias[i], 0.0), _dir_lanes(ssd_a_log[i], NEG_BIG),
                jnp.repeat(ssd_d[i].astype(F32), SSD_HEAD_DIM).reshape(1, -1), row(ssd_norm[i]),
                tcat, negmask, rexp, ecol, w_ffn_up[i])
            h, hn = _mixout(u, y_lo, y_hi, h, pool_w[i].astype(BF16), row(pool_scale[i]), w_out16,
                            row(mix_norm_post[i]), row(ffn_norm_pre[i]), tm=MIXOUT_ROWS)
            r = _ffn(hn, w_up16, ffn_conv_w[i].astype(F32), row(ffn_conv_b[i]), w_down16, row(ffn_norm_post[i]),
                     tm=FFN_ROWS, tf=FFN_COLS)
            h = _ple(h, r, p[i, b], row(ple_norm_pre[i]), w_gate16, w_ple[i].astype(BF16),
                     row(ple_norm_post[i]), tm=PLE_ROWS)
        outs.append(h)
    return jnp.stack(outs)
```

```python
import functools

import jax
import jax.numpy as jnp
from jax import lax
from jax.experimental import pallas as pl
from jax.experimental.pallas import tpu as pltpu

F32 = jnp.float32
BF16 = jnp.bfloat16
EPS = 1e-6

V7X_VMEM_BYTES = 64 * 1024 * 1024
V7X_LANES = 128
V7X_SUBLANES = 8
BF16_ROWS = 16

SSD_HEAD_DIM = 64
SSD_HEADS = 16
SSD_GROUPS = 2
SSD_HEADS_PER_GROUP = SSD_HEADS // SSD_GROUPS
SSD_STATE = 128
SSD_CONV = 5
SSD_CHUNK = 128
POOL_WINDOWS = (2, 4, 8, 16)
FFN_CONV = 3
NEG_BIG = -1e30

INPROJ_ROWS = 512
MIXOUT_ROWS = 512
FFN_ROWS = 1024
FFN_COLS = 512
PLE_ROWS = 512
CONV_PIECE = 256
CAST_ROWS = 256


def _vmem_limit(nbytes):
    return int(min(nbytes, V7X_VMEM_BYTES - 6 * 1024 * 1024))


def _rms(x):
    return x * lax.rsqrt(jnp.mean(x * x, axis=-1, keepdims=True) + EPS)


def _split3(x):
    hi = x.astype(BF16).astype(F32)
    r1 = x - hi
    mid = r1.astype(BF16).astype(F32)
    lo = (r1 - mid).astype(BF16).astype(F32)
    return hi, mid, lo


def _const_spec(shape):
    nd = len(shape)
    return pl.BlockSpec(shape, lambda *_: (0,) * nd, pipeline_mode=pl.Buffered(1))


def _halo_specs(rows, halo, total_rows, ncols):
    hb = rows // halo
    nhb = total_rows // halo
    return [
        pl.BlockSpec((halo, ncols), lambda i, *_: (jnp.maximum(i * hb - 1, 0), 0)),
        pl.BlockSpec((rows, ncols), lambda i, *_: (i, 0)),
        pl.BlockSpec((halo, ncols), lambda i, *_: (jnp.minimum(i * hb + hb, nhb - 1), 0)),
    ]


def _cast_kernel(w_ref, o_ref):
    o_ref[...] = w_ref[...].astype(o_ref.dtype)


def _cast_bf16(w, *, rows):
    R, C = w.shape
    spec = pl.BlockSpec((rows, C), lambda i: (i, 0))
    return pl.pallas_call(
        _cast_kernel,
        grid=(R // rows,),
        in_specs=[spec],
        out_specs=spec,
        out_shape=jax.ShapeDtypeStruct((R, C), BF16),
        compiler_params=pltpu.CompilerParams(dimension_semantics=("arbitrary",),
                                             vmem_limit_bytes=_vmem_limit(16 * rows * C)),
        name="cast_w_in",
    )(w)


def _inproj_kernel(xp_ref, xm_ref, xn_ref, g_ref, w_ref, cw_ref, cb_ref, wd_ref, wo_ref, wg_ref,
                   xs_ref, bc_ref, z_ref, u_ref, dt_ref, wd16_ref, wo16_ref, wg16_ref,
                   hn_ref, *ext_refs):
    i = pl.program_id(0)
    n = pl.num_programs(0)
    tm = xm_ref.shape[0]
    H = BF16_ROWS
    g = g_ref[...]
    hn_ref[0:H, :] = jnp.where(i > 0, _rms(xp_ref[...]) * g, 0.0).astype(BF16)
    hn_ref[H:H + tm, :] = (_rms(xm_ref[...]) * g).astype(BF16)
    hn_ref[H + tm:2 * H + tm, :] = jnp.where(i < n - 1, _rms(xn_ref[...]) * g, 0.0).astype(BF16)
    for src, dst in ((wd_ref, wd16_ref), (wo_ref, wo16_ref), (wg_ref, wg16_ref)):
        dst[...] = src[...].astype(BF16)

    cs = CONV_PIECE
    nz, nxs, nbc, nu = z_ref.shape[1], xs_ref.shape[1], bc_ref.shape[1], u_ref.shape[1]
    o_dt = nz + nxs + nbc
    n_dt = w_ref.shape[1] - o_dt - nu
    conv_outs = [(xs_ref, c, nz + c, c) for c in range(0, nxs, cs)]
    conv_outs += [(bc_ref, c, nz + nxs + c, nxs + c) for c in range(0, nbc, cs)]

    def project(c0, c1):
        return jnp.dot(hn_ref[H:H + tm, :], w_ref[:, c0:c1], preferred_element_type=F32)

    def z_chunk(c):
        z_ref[:, c:c + cs] = project(c, c + cs)

    tail = []
    split = o_dt + 2 * cs
    plain = [functools.partial(z_chunk, c) for c in range(0, nz, cs)]
    plain += [lambda: tail.append(project(o_dt, split)), lambda: tail.append(project(split, w_ref.shape[1]))]

    half = SSD_CONV // 2
    for k, (o_ref, c0, wc, cc) in enumerate(conv_outs):
        ext_ref = ext_refs[k % len(ext_refs)]
        ext_ref[...] = jnp.dot(hn_ref[...], w_ref[:, wc:wc + cs], preferred_element_type=F32)
        if k < len(plain):
            plain[k]()
        acc = cw_ref[0:1, cc:cc + cs] * ext_ref[H - half:H - half + tm, :]
        for j in range(1, SSD_CONV):
            acc = acc + cw_ref[j:j + 1, cc:cc + cs] * ext_ref[H - half + j:H - half + j + tm, :]
        acc = acc + cb_ref[:, cc:cc + cs]
        o_ref[:, c0:c0 + cs] = (acc * jax.nn.sigmoid(acc)).astype(o_ref.dtype)
    for fn in plain[len(conv_outs):]:
        fn()
    dt_ref[...] = tail[0][:, 0:dt_ref.shape[1]]
    u_ref[...] = jnp.concatenate([tail[0][:, n_dt:], tail[1]], axis=1)


def _inproj(x, g, w, cw, cb, out_cols, w_down, w_out, w_gate, *, tm):
    L, D = x.shape
    H = BF16_ROWS
    n = L // tm
    dts = (BF16, BF16, F32, F32, F32)
    row = lambda c: pl.BlockSpec((tm, c), lambda i: (i, 0))
    slab = lambda a: pl.BlockSpec((a.shape[0] // n, a.shape[1]), lambda i: (i, 0))
    casts = (w_down, w_out, w_gate)
    est = (2 * (tm + 2 * H) * D * 4 + w.size * 2 + 2 * tm * sum(out_cols) * 4
           + (tm + 2 * H) * D * 2 + 4 * (tm + 2 * H) * CONV_PIECE * 4 + 2 * sum(a.size for a in casts) // n * 6
           + 16 * tm * CONV_PIECE * 4)
    return pl.pallas_call(
        _inproj_kernel,
        grid=(n,),
        in_specs=_halo_specs(tm, H, L, D) + [_const_spec(a.shape) for a in (g, w, cw, cb)]
                 + [slab(a) for a in casts],
        out_specs=[row(c) for c in out_cols] + [slab(a) for a in casts],
        out_shape=[jax.ShapeDtypeStruct((L, c), dt) for c, dt in zip(out_cols, dts)]
                  + [jax.ShapeDtypeStruct(a.shape, BF16) for a in casts],
        scratch_shapes=[pltpu.VMEM((tm + 2 * H, D), BF16)]
                       + [pltpu.VMEM((tm + 2 * H, CONV_PIECE), F32) for _ in range(2)],
        compiler_params=pltpu.CompilerParams(dimension_semantics=("arbitrary",), vmem_limit_bytes=_vmem_limit(est)),
        name="inproj",
    )(x, x, x, g, w, cw, cb, *casts)


SSD_PACK = 16


LOG2E = 1.4426950408889634


def _pack3(v, head_lanes):
    hi, mid, lo = _split3(jnp.where(head_lanes, v, jnp.zeros_like(v)))
    packed = hi + pltpu.roll(mid, SSD_PACK, axis=1) + pltpu.roll(lo, 2 * SSD_PACK, axis=1)
    return packed.astype(BF16)


def _scan_both(xs_refs, bc_refs, dt_refs, dtb_ref, alog_ref, t_ref, nm_ref, rexp_ref, ecol_ref, h_ref):
    Q = SSD_CHUNK
    P = SSD_HEAD_DIM
    E = SSD_HEADS_PER_GROUP
    GW = E * P
    DIRS = (0, 1)
    last = (Q - 1, 0)
    xs16 = [xs_refs[d][...] for d in DIRS]
    bc =[bc_refs[d][...] for d in DIRS]

    lane = lax.broadcasted_iota(jnp.int32, (Q, V7X_LANES), 1)
    head_lanes = [jnp.logical_and(lane >= d * SSD_HEADS, lane < (d + 1) * SSD_HEADS) for d in DIRS]
    a = -jnp.exp(alog_ref[...])
    dt = [jax.nn.softplus(dt_refs[d][...] + dtb_ref[...]) for d in DIRS]
    da = [dt[d] * a for d in DIRS]
    da3 = [jnp.concatenate([v.astype(BF16) for v in _split3(da[d])], axis=0) for d in DIRS]
    acum = [jnp.dot(t_ref[d], da3[d], preferred_element_type=F32) * LOG2E for d in DIRS]
    tot = [acum[d][last[d]:last[d] + 1, :] for d in DIRS]
    src_t = [(acum[d] - jnp.log2(dt[d])).T for d in DIRS]
    e_a = [jnp.exp2(acum[d]) for d in DIRS]
    w_end = [jnp.exp2(tot[d] - acum[d]) for d in DIRS]

    lhs = [jnp.concatenate([_pack3(v, head_lanes[d]) for v in (dt[d] * w_end[d], e_a[d])], axis=0)
           for d in DIRS]
    exp2x = [jnp.dot(lhs[d], rexp_ref[d], preferred_element_type=F32) for d in DIRS]
    acol = [jnp.dot(_pack3(acum[d], head_lanes[d]), ecol_ref[d], preferred_element_type=F32) for d in DIRS]
    xw = [xs16[d] * exp2x[d][0:Q].astype(BF16) for d in DIRS]
    ea_x = [exp2x[d][Q:2 * Q] for d in DIRS]
    negmask = [nm_ref[d] for d in DIRS]
    first_head = lax.broadcasted_iota(jnp.int32, (Q, 2 * P), 1) < P

    y_groups = [[], []]
    for g in range(SSD_GROUPS):
        c0 = g * GW
        bm = [bc[d][:, g * SSD_STATE:(g + 1) * SSD_STATE] for d in DIRS]
        cm = [bc[d][:, (SSD_GROUPS + g) * SSD_STATE:(SSD_GROUPS + g + 1) * SSD_STATE] for d in DIRS]
        cb = [lax.dot_general(cm[d], bm[d], (((1,), (1,)), ((), ())), preferred_element_type=F32)
              for d in DIRS]
        st = [lax.dot_general(bm[d], xw[d][:, c0:c0 + GW], (((0,), (0,)), ((), ())), preferred_element_type=F32)
              for d in DIRS]
        h_in = [h_ref[d, g] for d in DIRS]
        y_off = [jnp.dot(cm[d], h_in[d].astype(BF16), preferred_element_type=F32) for d in DIRS]
        y_g = [y_off[d] * ea_x[d][:, c0:c0 + GW] for d in DIRS]
        for d in DIRS:
            h_ref[d, g] = h_in[d] * ea_x[d][last[d]:last[d] + 1, c0:c0 + GW] + st[d]
        pairs = [[], []]
        for hp in range(E // 2):
            for d in DIRS:
                h0 = g * E + hp * 2
                ms = []
                for k in range(2):
                    r = d * SSD_HEADS + h0 + k
                    seg = acol[d][:, (h0 + k) * Q:(h0 + k + 1) * Q] - src_t[d][r:r + 1, :] + negmask[d]
                    ms.append((cb[d] * jnp.exp2(seg)).astype(BF16))
                m2 = jnp.concatenate(ms, axis=1)
                xp = xs16[d][:, (g * E + hp * 2) * P:(g * E + hp * 2 + 2) * P]
                zero = jnp.zeros_like(xp)
                rhs = jnp.concatenate([jnp.where(first_head, xp, zero), jnp.where(first_head, zero, xp)], axis=0)
                pairs[d].append(jnp.dot(m2, rhs, preferred_element_type=F32))
        for d in DIRS:
            y_groups[d].append(y_g[d] + jnp.concatenate(pairs[d], axis=1))
    return [(jnp.concatenate(y_groups[d], axis=1), xs16[d]) for d in DIRS]


def _ssd_kernel(xsf_ref, bcf_ref, dtf_ref, zf_ref, xsb_ref, bcb_ref, dtb_raw_ref, zb_ref,
                dtb_ref, alog_ref, dexp_ref, nw_ref, t_ref, nm_ref, rexp_ref, ecol_ref, wup_ref,
                lo_ref, hi_ref, wup16_ref,
                y_ref, h_ref):
    Q = SSD_CHUNK
    GW = SSD_HEADS_PER_GROUP * SSD_HEAD_DIM
    i = pl.program_id(0)
    nc = pl.num_programs(0)

    wup16_ref[...] = wup_ref[...].astype(BF16)

    @pl.when(i == 0)
    def _():
        h_ref[...] = jnp.zeros_like(h_ref)

    (y_f, xs_f), (y_b, xs_b) = _scan_both((xsf_ref, xsb_ref), (bcf_ref, bcb_ref), (dtf_ref, dtb_raw_ref),
                                          dtb_ref, alog_ref, t_ref, nm_ref, rexp_ref, ecol_ref, h_ref)
    row_f = pl.multiple_of(i * Q, Q)
    row_b = pl.multiple_of((nc - 1 - i) * Q, Q)

    @pl.when(i < nc // 2)
    def _():
        y_ref[pl.ds(row_f, Q), :] = y_f
        y_ref[pl.ds(row_b, Q), :] = y_b

    def finish(y, xs, z_ref, o_ref):
        yy = y + xs.astype(F32) * dexp_ref[...]
        z = z_ref[...]
        yy = yy * (z * jax.nn.sigmoid(z))
        outs = [_rms(yy[:, g * GW:(g + 1) * GW]) for g in range(SSD_GROUPS)]
        o_ref[...] = (jnp.concatenate(outs, axis=1) * nw_ref[...]).astype(o_ref.dtype)

    @pl.when(i >= nc // 2)
    def _():
        finish(y_f + y_ref[pl.ds(row_f, Q), :], xs_f, zf_ref, hi_ref)
        finish(y_b + y_ref[pl.ds(row_b, Q), :], xs_b, zb_ref, lo_ref)


def _ssd(xs, bc, dt_raw, z, dtb, alog, dexp, nw, tcat, negmask, rexp, ecol, w_up):
    L, DS = xs.shape
    Q = SSD_CHUNK
    nc = L // Q
    hc = nc // 2
    slab = pl.BlockSpec((w_up.shape[0] // nc, w_up.shape[1]), lambda i: (i, 0))
    fwd = lambda i: i
    bwd = lambda i: nc - 1 - i
    fwd_late = lambda i: jnp.maximum(i, hc)
    bwd_late = lambda i: jnp.minimum(nc - 1 - i, hc - 1)
    in_specs = [
        pl.BlockSpec((Q, DS), lambda i: (fwd(i), 0)),
        pl.BlockSpec((Q, bc.shape[1]), lambda i: (fwd(i), 0)),
        pl.BlockSpec((Q, V7X_LANES), lambda i: (fwd(i), 0)),
        pl.BlockSpec((Q, DS), lambda i: (fwd_late(i), 0)),
        pl.BlockSpec((Q, DS), lambda i: (bwd(i), 0)),
        pl.BlockSpec((Q, bc.shape[1]), lambda i: (bwd(i), 0)),
        pl.BlockSpec((Q, V7X_LANES), lambda i: (bwd(i), 0)),
        pl.BlockSpec((Q, DS), lambda i: (bwd_late(i), 0)),
    ] + [_const_spec(a.shape) for a in (dtb, alog, dexp, nw, tcat, negmask, rexp, ecol)] + [slab]
    est = (L * DS * 4 + 2 * SSD_GROUPS * SSD_STATE * DS * 4 + 8 * Q * (DS + bc.shape[1]) * 4
           + (rexp.size + ecol.size) * 2 + 4 * Q * DS * 4 * 2 + 40 * Q * DS * 4 + 12 * w_up.size // nc)
    return pl.pallas_call(
        _ssd_kernel,
        grid=(nc,),
        in_specs=in_specs,
        out_specs=[pl.BlockSpec((Q, DS), lambda i: (bwd_late(i), 0)),
                   pl.BlockSpec((Q, DS), lambda i: (fwd_late(i) - hc, 0)), slab],
        out_shape=[jax.ShapeDtypeStruct((L // 2, DS), BF16), jax.ShapeDtypeStruct((L // 2, DS), BF16),
                   jax.ShapeDtypeStruct(w_up.shape, BF16)],
        scratch_shapes=[
            pltpu.VMEM((L, DS), F32),
            pltpu.VMEM((2, SSD_GROUPS, SSD_STATE, DS // SSD_GROUPS), F32),
        ],
        compiler_params=pltpu.CompilerParams(dimension_semantics=("arbitrary",), vmem_limit_bytes=_vmem_limit(est)),
        name="ssd",
    )(xs, bc, dt_raw, z, xs, bc, dt_raw, z, dtb, alog, dexp, nw, tcat, negmask, rexp, ecol, w_up)


def _mixout_kernel(up_ref, um_ref, un_ref, ylo_ref, yhi_ref, x_ref, pw_ref, ps_ref, wo_ref, g_ref, gn_ref,
                   o_ref, on_ref, ext_ref, *, seq):
    i = pl.program_id(0)
    n = pl.num_programs(0)
    tm = um_ref.shape[0]
    cg = um_ref.shape[1] // len(POOL_WINDOWS)
    ext_ref[0:8, :] = jnp.where(i > 0, up_ref[...], 0.0)
    ext_ref[8:8 + tm, :] = um_ref[...]
    ext_ref[8 + tm:16 + tm, :] = jnp.where(i < n - 1, un_ref[...], 0.0)
    ys = jnp.where(i < n // 2, ylo_ref[...], yhi_ref[...])
    ds = ys.shape[1]
    mix = jnp.dot(ys, wo_ref[0:ds, :], preferred_element_type=F32)
    t = i * tm + lax.broadcasted_iota(jnp.int32, (tm, cg), 0)
    rows = tm + 16

    def ahead(v, k):
        return pltpu.roll(v, (rows - k) % rows, axis=0)

    pooled = []
    for gi, k in enumerate(POOL_WINDOWS):
        cols = slice(gi * cg, (gi + 1) * cg)
        e = ext_ref[:, cols]
        half = k // 2
        run, length = e, 1
        while length < half:
            run = run + ahead(run, length)
            length *= 2
        before = run[0:tm] if half == 8 else ahead(run, rows - half)[8:8 + tm]
        acc = before + run[8:8 + tm]
        cnt = (jnp.minimum(t + (k - k // 2), seq) - jnp.maximum(t - k // 2, 0)).astype(F32)
        mixed = acc / cnt - um_ref[:, cols]
        yp = jnp.dot(mixed.astype(BF16), pw_ref[gi], preferred_element_type=F32) * ps_ref[:, cols]
        pooled.append(yp.astype(BF16))
    ypool = jnp.concatenate(pooled, axis=1)
    mix = mix + jnp.dot(ypool, wo_ref[ds:, :], preferred_element_type=F32)
    h = x_ref[...] + _rms(mix) * g_ref[...]
    o_ref[...] = h
    on_ref[...] = (_rms(h) * gn_ref[...]).astype(on_ref.dtype)


def _mixout(u, y_lo, y_hi, x, pool_w, pool_scale, w_out, g, g_next, *, tm):
    L, DP = u.shape
    D = x.shape[1]
    DS = y_lo.shape[1]
    n = L // tm
    row = lambda n_: pl.BlockSpec((tm, n_), lambda i: (i, 0))
    in_specs = (_halo_specs(tm, V7X_SUBLANES, L, DP)
                + [pl.BlockSpec((tm, DS), lambda i: (jnp.minimum(i, n // 2 - 1), 0)),
                   pl.BlockSpec((tm, DS), lambda i: (jnp.maximum(i - n // 2, 0), 0)), row(D)]
                + [_const_spec(a.shape) for a in (pool_w, pool_scale, w_out, g, g_next)])
    est = (2 * tm * (DP * 4 + 2 * DS * 2 + D * 4 + D * 4 + D * 2) + (w_out.size + pool_w.size) * 2
           + (tm + 16) * DP * 4 + 8 * tm * D * 4)
    return pl.pallas_call(
        functools.partial(_mixout_kernel, seq=L),
        grid=(n,),
        in_specs=in_specs,
        out_specs=[row(D), row(D)],
        out_shape=[jax.ShapeDtypeStruct((L, D), F32), jax.ShapeDtypeStruct((L, D), BF16)],
        scratch_shapes=[pltpu.VMEM((tm + 16, DP), F32)],
        compiler_params=pltpu.CompilerParams(dimension_semantics=("arbitrary",), vmem_limit_bytes=_vmem_limit(est)),
        name="mixout",
    )(u, u, u, y_lo, y_hi, x, pool_w, pool_scale, w_out, g, g_next)


def _ffn_kernel(hp_ref, hm_ref, hx_ref, wg_ref, wv_ref, cw_ref, cb_ref, wd_ref, gpost_ref,
                o_ref, hn_ref, gate_ref):
    i = pl.program_id(0)
    f = pl.program_id(1)
    n = pl.num_programs(0)
    nf = pl.num_programs(1)
    tm = hm_ref.shape[0]
    H = BF16_ROWS

    @pl.when(f == 0)
    def _():
        hn_ref[0:H, :] = jnp.where(i > 0, hp_ref[...], jnp.zeros_like(hp_ref))
        hn_ref[H:H + tm, :] = hm_ref[...]
        hn_ref[H + tm:2 * H + tm, :] = jnp.where(i < n - 1, hx_ref[...], jnp.zeros_like(hx_ref))
        o_ref[...] = jnp.zeros_like(o_ref)

    gate_ref[...] = jnp.dot(hn_ref[...], wg_ref[...], preferred_element_type=F32)
    val = jnp.dot(hm_ref[...], wv_ref[...], preferred_element_type=F32)
    half = FFN_CONV // 2
    gc = cw_ref[0:1, :] * gate_ref[H - half:H - half + tm, :]
    for j in range(1, FFN_CONV):
        gc = gc + cw_ref[j:j + 1, :] * gate_ref[H - half + j:H - half + j + tm, :]
    gc = gc + cb_ref[...]
    act = (jax.nn.gelu(gc, approximate=True) * val).astype(BF16)
    o_ref[...] += jnp.dot(act, wd_ref[...], preferred_element_type=F32)

    @pl.when(f == nf - 1)
    def _():
        o_ref[...] = _rms(o_ref[...]) * gpost_ref[...]


def _ffn(hn, w_up, cw, cb, w_down, gpost, *, tm, tf):
    L, D = hn.shape
    DF = w_down.shape[0]
    nf = DF // tf
    H = BF16_ROWS
    in_specs = _halo_specs(tm, H, L, D) + [
        pl.BlockSpec((D, tf), lambda i, f: (0, f)),
        pl.BlockSpec((D, tf), lambda i, f: (0, nf + f)),
        pl.BlockSpec((FFN_CONV, tf), lambda i, f: (0, f)),
        pl.BlockSpec((1, tf), lambda i, f: (0, f)),
        pl.BlockSpec((tf, D), lambda i, f: (f, 0)),
        pl.BlockSpec(gpost.shape, lambda i, f: (0, 0)),
    ]
    est = (2 * (tm + 2 * H) * D * 2 + 2 * tm * D * 4 + (tm + 2 * H) * D * 2 + (tm + 2 * H) * tf * 4
           + 2 * 3 * D * tf * 2 + 8 * tm * tf * 4 + tm * D * 4)
    return pl.pallas_call(
        _ffn_kernel,
        grid=(L // tm, nf),
        in_specs=in_specs,
        out_specs=pl.BlockSpec((tm, D), lambda i, f: (i, 0)),
        out_shape=jax.ShapeDtypeStruct((L, D), F32),
        scratch_shapes=[
            pltpu.VMEM((tm + 2 * H, D), BF16),
            pltpu.VMEM((tm + 2 * H, tf), F32),
        ],
        compiler_params=pltpu.CompilerParams(dimension_semantics=("arbitrary", "arbitrary"),
                                             vmem_limit_bytes=_vmem_limit(est)),
        name="ffn",
    )(hn, hn, hn, w_up, w_up, cw, cb, w_down, gpost)


def _ple_kernel(h_ref, r_ref, p_ref, gpre_ref, wg_ref, wp_ref, gpost_ref, o_ref):
    h = h_ref[...] + r_ref[...]
    hn = (_rms(h) * gpre_ref[...]).astype(BF16)
    gate = jax.nn.sigmoid(jnp.dot(hn, wg_ref[...], preferred_element_type=F32))
    pe = jnp.dot(p_ref[...].astype(BF16), wp_ref[...], preferred_element_type=F32)
    o_ref[...] = h + _rms(gate * pe) * gpost_ref[...]


def _ple(h, r, p, gpre, w_gate, w_ple, gpost, *, tm):
    L, D = h.shape
    row = lambda n: pl.BlockSpec((tm, n), lambda i: (i, 0))
    est = 2 * tm * (3 * D + p.shape[1]) * 4 + (w_gate.size + w_ple.size) * 2 + 8 * tm * D * 4
    return pl.pallas_call(
        _ple_kernel,
        grid=(L // tm,),
        in_specs=[row(D), row(D), row(p.shape[1])] + [_const_spec(a.shape) for a in (gpre, w_gate, w_ple, gpost)],
        out_specs=row(D),
        out_shape=jax.ShapeDtypeStruct((L, D), F32),
        compiler_params=pltpu.CompilerParams(dimension_semantics=("arbitrary",), vmem_limit_bytes=_vmem_limit(est)),
        name="ple",
    )(h, r, p, gpre, w_gate, w_ple, gpost)


def _ssd_constants():
    Q = SSD_CHUNK
    r = jnp.arange(Q)
    lower = (r[:, None] >= r[None, :])
    tri = jnp.stack([lower, lower.T]).astype(BF16)
    tcat = jnp.concatenate([tri, tri, tri], axis=2)
    negmask = jnp.where(jnp.stack([lower, lower.T]), 0.0, NEG_BIG).astype(F32)
    k = jnp.arange(V7X_LANES)
    rexp, ecol = [], []
    for d in range(2):
        packed = jnp.logical_and(k >= d * SSD_HEADS, k < d * SSD_HEADS + 3 * SSD_PACK)
        head_of_lane = jnp.where(packed, (k - d * SSD_HEADS) % SSD_PACK, -1)
        rexp.append(head_of_lane[:, None] == (jnp.arange(SSD_HEADS * SSD_HEAD_DIM) // SSD_HEAD_DIM)[None, :])
        ecol.append(head_of_lane[:, None] == (jnp.arange(SSD_HEADS * Q) // Q)[None, :])
    return tcat, negmask, jnp.stack(rexp).astype(BF16), jnp.stack(ecol).astype(BF16)


def _dir_lanes(v, fill):
    out = jnp.full((1, V7X_LANES), fill, F32)
    return out.at[0, :v.size].set(v.astype(F32).reshape(-1))


def kernel(x, p, mix_norm_pre, mix_norm_post, w_in, ssd_conv_w, ssd_conv_b, ssd_dt_bias, ssd_a_log, ssd_d,
           ssd_norm, pool_w, pool_scale, w_out, ffn_norm_pre, ffn_norm_post, w_ffn_up, ffn_conv_w, ffn_conv_b,
           w_ffn_down, ple_norm_pre, w_ple_gate, w_ple, ple_norm_post):
    B, L, D = x.shape
    depth = w_in.shape[0]
    d_ssd = SSD_HEADS * SSD_HEAD_DIM
    n_bc = 2 * SSD_GROUPS * SSD_STATE
    o_dt = 2 * d_ssd + n_bc
    o_u = o_dt + 2 * SSD_HEADS
    d_pool = w_in.shape[2] - o_u
    tcat, negmask, rexp, ecol = _ssd_constants()
    row = lambda v: v.reshape(1, -1).astype(F32)

    outs = []
    for b in range(B):
        h = x[b]
        for i in range(depth):
            xs, bc, z, u, dt_raw, w_down16, w_out16, w_gate16 = _inproj(
                h, row(mix_norm_pre[i]), _cast_bf16(w_in[i], rows=CAST_ROWS), ssd_conv_w[i].astype(F32),
                row(ssd_conv_b[i]),
                (d_ssd, n_bc, d_ssd, d_pool, V7X_LANES), w_ffn_down[i], w_out[i], w_ple_gate[i], tm=INPROJ_ROWS)
            y_lo, y_hi, w_up16 = _ssd(
                xs, bc, dt_raw, z, _dir_lanes(ssd_dt_bias[i], 0.0), _dir_lanes(ssd_a_log[i], NEG_BIG),
                jnp.repeat(ssd_d[i].astype(F32), SSD_HEAD_DIM).reshape(1, -1), row(ssd_norm[i]),
                tcat, negmask, rexp, ecol, w_ffn_up[i])
            h, hn = _mixout(u, y_lo, y_hi, h, pool_w[i].astype(BF16), row(pool_scale[i]), w_out16,
                            row(mix_norm_post[i]), row(ffn_norm_pre[i]), tm=MIXOUT_ROWS)
            r = _ffn(hn, w_up16, ffn_conv_w[i].astype(F32), row(ffn_conv_b[i]), w_down16, row(ffn_norm_post[i]),
                     tm=FFN_ROWS, tf=FFN_COLS)
            h = _ple(h, r, p[i, b], row(ple_norm_pre[i]), w_gate16, w_ple[i].astype(BF16),
                     row(ple_norm_post[i]), tm=PLE_ROWS)
        outs.append(h)
    return jnp.stack(outs)
```

```python
import functools

import jax
import jax.numpy as jnp
from jax import lax
from jax.experimental import pallas as pl
from jax.experimental.pallas import tpu as pltpu

F32 = jnp.float32
BF16 = jnp.bfloat16
EPS = 1e-6

V7X_VMEM_BYTES = 64 * 1024 * 1024
V7X_LANES = 128
V7X_SUBLANES = 8
BF16_ROWS = 16

SSD_HEAD_DIM = 64
SSD_HEADS = 16
SSD_GROUPS = 2
SSD_HEADS_PER_GROUP = SSD_HEADS // SSD_GROUPS
SSD_STATE = 128
SSD_CONV = 5
SSD_CHUNK = 128
POOL_WINDOWS = (2, 4, 8, 16)
FFN_CONV = 3
NEG_BIG = -1e30

INPROJ_ROWS = 512
MIXOUT_ROWS = 512
FFN_ROWS = 1024
FFN_COLS = 512
PLE_ROWS = 512
CONV_PIECE = 256


def _vmem_limit(nbytes):
    return int(min(nbytes, V7X_VMEM_BYTES - 6 * 1024 * 1024))


def _rms(x):
    return x * lax.rsqrt(jnp.mean(x * x, axis=-1, keepdims=True) + EPS)


def _split3(x):
    hi = x.astype(BF16).astype(F32)
    r1 = x - hi
    mid = r1.astype(BF16).astype(F32)
    lo = (r1 - mid).astype(BF16).astype(F32)
    return hi, mid, lo


def _const_spec(shape):
    nd = len(shape)
    return pl.BlockSpec(shape, lambda *_: (0,) * nd, pipeline_mode=pl.Buffered(1))


def _halo_specs(rows, halo, total_rows, ncols):
    hb = rows // halo
    nhb = total_rows // halo
    return [
        pl.BlockSpec((halo, ncols), lambda i, *_: (jnp.maximum(i * hb - 1, 0), 0)),
        pl.BlockSpec((rows, ncols), lambda i, *_: (i, 0)),
        pl.BlockSpec((halo, ncols), lambda i, *_: (jnp.minimum(i * hb + hb, nhb - 1), 0)),
    ]


def _cast_t_kernel(w_ref, o_ref):
    o_ref[...] = w_ref[...].T.astype(o_ref.dtype)


def _cast_transposed(w_t, *, cols):
    C, R = w_t.shape
    n = pl.cdiv(C, cols)
    return pl.pallas_call(
        _cast_t_kernel,
        grid=(n,),
        in_specs=[pl.BlockSpec((cols, R), lambda j: (j, 0))],
        out_specs=pl.BlockSpec((R, cols), lambda j: (0, j)),
        out_shape=jax.ShapeDtypeStruct((R, n * cols), BF16),
        compiler_params=pltpu.CompilerParams(dimension_semantics=("arbitrary",),
                                             vmem_limit_bytes=_vmem_limit(32 * cols * R)),
        name="cast_w_in",
    )(w_t)


def _inproj_kernel(xp_ref, xm_ref, xn_ref, g_ref, w_ref, cw_ref, cb_ref, wd_ref, wo_ref, wg_ref,
                   xs_ref, bc_ref, z_ref, u_ref, dt_ref, wd16_ref, wo16_ref, wg16_ref,
                   hn_ref, *ext_refs):
    i = pl.program_id(0)
    n = pl.num_programs(0)
    tm = xm_ref.shape[0]
    H = BF16_ROWS
    g = g_ref[...]
    hn_ref[0:H, :] = jnp.where(i > 0, _rms(xp_ref[...]) * g, 0.0).astype(BF16)
    hn_ref[H:H + tm, :] = (_rms(xm_ref[...]) * g).astype(BF16)
    hn_ref[H + tm:2 * H + tm, :] = jnp.where(i < n - 1, _rms(xn_ref[...]) * g, 0.0).astype(BF16)
    for src, dst in ((wd_ref, wd16_ref), (wo_ref, wo16_ref), (wg_ref, wg16_ref)):
        dst[...] = src[...].astype(BF16)

    cs = CONV_PIECE
    nz, nxs, nbc, nu = z_ref.shape[1], xs_ref.shape[1], bc_ref.shape[1], u_ref.shape[1]
    o_dt = nz + nxs + nbc
    n_dt = 2 * SSD_HEADS
    w_end = o_dt + n_dt + nu
    conv_outs = [(xs_ref, c, nz + c, c) for c in range(0, nxs, cs)]
    conv_outs += [(bc_ref, c, nz + nxs + c, nxs + c) for c in range(0, nbc, cs)]

    def project(c0, c1):
        return jnp.dot(hn_ref[H:H + tm, :], w_ref[:, c0:c1], preferred_element_type=F32)

    def z_chunk(c):
        z_ref[:, c:c + cs] = project(c, c + cs)

    tail = []
    split = o_dt + 2 * cs
    plain = [functools.partial(z_chunk, c) for c in range(0, nz, cs)]
    plain += [lambda: tail.append(project(o_dt, split)), lambda: tail.append(project(split, w_end))]

    half = SSD_CONV // 2
    for k, (o_ref, c0, wc, cc) in enumerate(conv_outs):
        ext_ref = ext_refs[k % len(ext_refs)]
        ext_ref[...] = jnp.dot(hn_ref[...], w_ref[:, wc:wc + cs], preferred_element_type=F32)
        if k < len(plain):
            plain[k]()
        acc = cw_ref[0:1, cc:cc + cs] * ext_ref[H - half:H - half + tm, :]
        for j in range(1, SSD_CONV):
            acc = acc + cw_ref[j:j + 1, cc:cc + cs] * ext_ref[H - half + j:H - half + j + tm, :]
        acc = acc + cb_ref[:, cc:cc + cs]
        o_ref[:, c0:c0 + cs] = (acc * jax.nn.sigmoid(acc)).astype(o_ref.dtype)
    for fn in plain[len(conv_outs):]:
        fn()
    dt_ref[...] = tail[0][:, 0:dt_ref.shape[1]]
    u_ref[...] = jnp.concatenate([tail[0][:, n_dt:], tail[1]], axis=1)


def _inproj(x, g, w, cw, cb, out_cols, w_down, w_out, w_gate, *, tm):
    L, D = x.shape
    H = BF16_ROWS
    n = L // tm
    dts = (BF16, BF16, F32, F32, F32)
    row = lambda c: pl.BlockSpec((tm, c), lambda i: (i, 0))
    slab = lambda a: pl.BlockSpec((a.shape[0] // n, a.shape[1]), lambda i: (i, 0))
    casts = (w_down, w_out, w_gate)
    est = (2 * (tm + 2 * H) * D * 4 + w.size * 2 + 2 * tm * sum(out_cols) * 4
           + (tm + 2 * H) * D * 2 + 4 * (tm + 2 * H) * CONV_PIECE * 4 + 2 * sum(a.size for a in casts) // n * 6
           + 16 * tm * CONV_PIECE * 4)
    return pl.pallas_call(
        _inproj_kernel,
        grid=(n,),
        in_specs=_halo_specs(tm, H, L, D) + [_const_spec(a.shape) for a in (g, w, cw, cb)]
                 + [slab(a) for a in casts],
        out_specs=[row(c) for c in out_cols] + [slab(a) for a in casts],
        out_shape=[jax.ShapeDtypeStruct((L, c), dt) for c, dt in zip(out_cols, dts)]
                  + [jax.ShapeDtypeStruct(a.shape, BF16) for a in casts],
        scratch_shapes=[pltpu.VMEM((tm + 2 * H, D), BF16)]
                       + [pltpu.VMEM((tm + 2 * H, CONV_PIECE), F32) for _ in range(2)],
        compiler_params=pltpu.CompilerParams(dimension_semantics=("arbitrary",), vmem_limit_bytes=_vmem_limit(est)),
        name="inproj",
    )(x, x, x, g, w, cw, cb, *casts)


SSD_PACK = 16


LOG2E = 1.4426950408889634


def _pack3(v, head_lanes):
    hi, mid, lo = _split3(jnp.where(head_lanes, v, jnp.zeros_like(v)))
    packed = hi + pltpu.roll(mid, SSD_PACK, axis=1) + pltpu.roll(lo, 2 * SSD_PACK, axis=1)
    return packed.astype(BF16)


def _scan_both(xs_refs, bc_refs, dt_refs, dtb_ref, alog_ref, t_ref, nm_ref, rexp_ref, ecol_ref, h_ref):
    Q = SSD_CHUNK
    P = SSD_HEAD_DIM
    E = SSD_HEADS_PER_GROUP
    GW = E * P
    DIRS = (0, 1)
    last = (Q - 1, 0)
    xs16 = [xs_refs[d][...] for d in DIRS]
    bc =[bc_refs[d][...] for d in DIRS]

    lane = lax.broadcasted_iota(jnp.int32, (Q, V7X_LANES), 1)
    head_lanes = [jnp.logical_and(lane >= d * SSD_HEADS, lane < (d + 1) * SSD_HEADS) for d in DIRS]
    a = -jnp.exp(alog_ref[...])
    dt = [jax.nn.softplus(dt_refs[d][...] + dtb_ref[...]) for d in DIRS]
    da = [dt[d] * a for d in DIRS]
    da3 = [jnp.concatenate([v.astype(BF16) for v in _split3(da[d])], axis=0) for d in DIRS]
    acum = [jnp.dot(t_ref[d], da3[d], preferred_element_type=F32) * LOG2E for d in DIRS]
    tot = [acum[d][last[d]:last[d] + 1, :] for d in DIRS]
    src_t = [(acum[d] - jnp.log2(dt[d])).T for d in DIRS]
    e_a = [jnp.exp2(acum[d]) for d in DIRS]
    w_end = [jnp.exp2(tot[d] - acum[d]) for d in DIRS]

    lhs = [jnp.concatenate([_pack3(v, head_lanes[d]) for v in (dt[d] * w_end[d], e_a[d])], axis=0)
           for d in DIRS]
    exp2x = [jnp.dot(lhs[d], rexp_ref[d], preferred_element_type=F32) for d in DIRS]
    acol = [jnp.dot(_pack3(acum[d], head_lanes[d]), ecol_ref[d], preferred_element_type=F32) for d in DIRS]
    xw = [xs16[d] * exp2x[d][0:Q].astype(BF16) for d in DIRS]
    ea_x = [exp2x[d][Q:2 * Q] for d in DIRS]
    negmask = [nm_ref[d] for d in DIRS]
    first_head = lax.broadcasted_iota(jnp.int32, (Q, 2 * P), 1) < P

    y_groups = [[], []]
    for g in range(SSD_GROUPS):
        c0 = g * GW
        bm = [bc[d][:, g * SSD_STATE:(g + 1) * SSD_STATE] for d in DIRS]
        cm = [bc[d][:, (SSD_GROUPS + g) * SSD_STATE:(SSD_GROUPS + g + 1) * SSD_STATE] for d in DIRS]
        cb = [lax.dot_general(cm[d], bm[d], (((1,), (1,)), ((), ())), preferred_element_type=F32)
              for d in DIRS]
        st = [lax.dot_general(bm[d], xw[d][:, c0:c0 + GW], (((0,), (0,)), ((), ())), preferred_element_type=F32)
              for d in DIRS]
        h_in = [h_ref[d, g] for d in DIRS]
        y_off = [jnp.dot(cm[d], h_in[d].astype(BF16), preferred_element_type=F32) for d in DIRS]
        y_g = [y_off[d] * ea_x[d][:, c0:c0 + GW] for d in DIRS]
        for d in DIRS:
            h_ref[d, g] = h_in[d] * ea_x[d][last[d]:last[d] + 1, c0:c0 + GW] + st[d]
        pairs = [[], []]
        for hp in range(E // 2):
            for d in DIRS:
                h0 = g * E + hp * 2
                ms = []
                for k in range(2):
                    r = d * SSD_HEADS + h0 + k
                    seg = acol[d][:, (h0 + k) * Q:(h0 + k + 1) * Q] - src_t[d][r:r + 1, :] + negmask[d]
                    ms.append((cb[d] * jnp.exp2(seg)).astype(BF16))
                m2 = jnp.concatenate(ms, axis=1)
                xp = xs16[d][:, (g * E + hp * 2) * P:(g * E + hp * 2 + 2) * P]
                zero = jnp.zeros_like(xp)
                rhs = jnp.concatenate([jnp.where(first_head, xp, zero), jnp.where(first_head, zero, xp)], axis=0)
                pairs[d].append(jnp.dot(m2, rhs, preferred_element_type=F32))
        for d in DIRS:
            y_groups[d].append(y_g[d] + jnp.concatenate(pairs[d], axis=1))
    return [(jnp.concatenate(y_groups[d], axis=1), xs16[d]) for d in DIRS]


def _ssd_kernel(xsf_ref, bcf_ref, dtf_ref, zf_ref, xsb_ref, bcb_ref, dtb_raw_ref, zb_ref,
                dtb_ref, alog_ref, dexp_ref, nw_ref, t_ref, nm_ref, rexp_ref, ecol_ref, wup_ref,
                lo_ref, hi_ref, wup16_ref,
                y_ref, h_ref):
    Q = SSD_CHUNK
    GW = SSD_HEADS_PER_GROUP * SSD_HEAD_DIM
    i = pl.program_id(0)
    nc = pl.num_programs(0)

    wup16_ref[...] = wup_ref[...].astype(BF16)

    @pl.when(i == 0)
    def _():
        h_ref[...] = jnp.zeros_like(h_ref)

    (y_f, xs_f), (y_b, xs_b) = _scan_both((xsf_ref, xsb_ref), (bcf_ref, bcb_ref), (dtf_ref, dtb_raw_ref),
                                          dtb_ref, alog_ref, t_ref, nm_ref, rexp_ref, ecol_ref, h_ref)
    row_f = pl.multiple_of(i * Q, Q)
    row_b = pl.multiple_of((nc - 1 - i) * Q, Q)

    @pl.when(i < nc // 2)
    def _():
        y_ref[pl.ds(row_f, Q), :] = y_f
        y_ref[pl.ds(row_b, Q), :] = y_b

    def finish(y, xs, z_ref, o_ref):
        yy = y + xs.astype(F32) * dexp_ref[...]
        z = z_ref[...]
        yy = yy * (z * jax.nn.sigmoid(z))
        outs = [_rms(yy[:, g * GW:(g + 1) * GW]) for g in range(SSD_GROUPS)]
        o_ref[...] = (jnp.concatenate(outs, axis=1) * nw_ref[...]).astype(o_ref.dtype)

    @pl.when(i >= nc // 2)
    def _():
        finish(y_f + y_ref[pl.ds(row_f, Q), :], xs_f, zf_ref, hi_ref)
        finish(y_b + y_ref[pl.ds(row_b, Q), :], xs_b, zb_ref, lo_ref)


def _ssd(xs, bc, dt_raw, z, dtb, alog, dexp, nw, tcat, negmask, rexp, ecol, w_up):
    L, DS = xs.shape
    Q = SSD_CHUNK
    nc = L // Q
    hc = nc // 2
    slab = pl.BlockSpec((w_up.shape[0] // nc, w_up.shape[1]), lambda i: (i, 0))
    fwd = lambda i: i
    bwd = lambda i: nc - 1 - i
    fwd_late = lambda i: jnp.maximum(i, hc)
    bwd_late = lambda i: jnp.minimum(nc - 1 - i, hc - 1)
    in_specs = [
        pl.BlockSpec((Q, DS), lambda i: (fwd(i), 0)),
        pl.BlockSpec((Q, bc.shape[1]), lambda i: (fwd(i), 0)),
        pl.BlockSpec((Q, V7X_LANES), lambda i: (fwd(i), 0)),
        pl.BlockSpec((Q, DS), lambda i: (fwd_late(i), 0)),
        pl.BlockSpec((Q, DS), lambda i: (bwd(i), 0)),
        pl.BlockSpec((Q, bc.shape[1]), lambda i: (bwd(i), 0)),
        pl.BlockSpec((Q, V7X_LANES), lambda i: (bwd(i), 0)),
        pl.BlockSpec((Q, DS), lambda i: (bwd_late(i), 0)),
    ] + [_const_spec(a.shape) for a in (dtb, alog, dexp, nw, tcat, negmask, rexp, ecol)] + [slab]
    est = (L * DS * 4 + 2 * SSD_GROUPS * SSD_STATE * DS * 4 + 8 * Q * (DS + bc.shape[1]) * 4
           + (rexp.size + ecol.size) * 2 + 4 * Q * DS * 4 * 2 + 40 * Q * DS * 4 + 12 * w_up.size // nc)
    return pl.pallas_call(
        _ssd_kernel,
        grid=(nc,),
        in_specs=in_specs,
        out_specs=[pl.BlockSpec((Q, DS), lambda i: (bwd_late(i), 0)),
                   pl.BlockSpec((Q, DS), lambda i: (fwd_late(i) - hc, 0)), slab],
        out_shape=[jax.ShapeDtypeStruct((L // 2, DS), BF16), jax.ShapeDtypeStruct((L // 2, DS), BF16),
                   jax.ShapeDtypeStruct(w_up.shape, BF16)],
        scratch_shapes=[
            pltpu.VMEM((L, DS), F32),
            pltpu.VMEM((2, SSD_GROUPS, SSD_STATE, DS // SSD_GROUPS), F32),
        ],
        compiler_params=pltpu.CompilerParams(dimension_semantics=("arbitrary",), vmem_limit_bytes=_vmem_limit(est)),
        name="ssd",
    )(xs, bc, dt_raw, z, xs, bc, dt_raw, z, dtb, alog, dexp, nw, tcat, negmask, rexp, ecol, w_up)


def _mixout_kernel(up_ref, um_ref, un_ref, ylo_ref, yhi_ref, x_ref, pw_ref, ps_ref, wo_ref, g_ref, gn_ref,
                   o_ref, on_ref, ext_ref, *, seq):
    i = pl.program_id(0)
    n = pl.num_programs(0)
    tm = um_ref.shape[0]
    cg = um_ref.shape[1] // len(POOL_WINDOWS)
    ext_ref[0:8, :] = jnp.where(i > 0, up_ref[...], 0.0)
    ext_ref[8:8 + tm, :] = um_ref[...]
    ext_ref[8 + tm:16 + tm, :] = jnp.where(i < n - 1, un_ref[...], 0.0)
    ys = jnp.where(i < n // 2, ylo_ref[...], yhi_ref[...])
    ds = ys.shape[1]
    mix = jnp.dot(ys, wo_ref[0:ds, :], preferred_element_type=F32)
    t = i * tm + lax.broadcasted_iota(jnp.int32, (tm, cg), 0)
    rows = tm + 16

    def ahead(v, k):
        return pltpu.roll(v, (rows - k) % rows, axis=0)

    pooled = []
    for gi, k in enumerate(POOL_WINDOWS):
        cols = slice(gi * cg, (gi + 1) * cg)
        e = ext_ref[:, cols]
        half = k // 2
        run, length = e, 1
        while length < half:
            run = run + ahead(run, length)
            length *= 2
        before = run[0:tm] if half == 8 else ahead(run, rows - half)[8:8 + tm]
        acc = before + run[8:8 + tm]
        cnt = (jnp.minimum(t + (k - k // 2), seq) - jnp.maximum(t - k // 2, 0)).astype(F32)
        mixed = acc / cnt - um_ref[:, cols]
        yp = jnp.dot(mixed.astype(BF16), pw_ref[gi], preferred_element_type=F32) * ps_ref[:, cols]
        pooled.append(yp.astype(BF16))
    ypool = jnp.concatenate(pooled, axis=1)
    mix = mix + jnp.dot(ypool, wo_ref[ds:, :], preferred_element_type=F32)
    h = x_ref[...] + _rms(mix) * g_ref[...]
    o_ref[...] = h
    on_ref[...] = (_rms(h) * gn_ref[...]).astype(on_ref.dtype)


def _mixout(u, y_lo, y_hi, x, pool_w, pool_scale, w_out, g, g_next, *, tm):
    L, DP = u.shape
    D = x.shape[1]
    DS = y_lo.shape[1]
    n = L // tm
    row = lambda n_: pl.BlockSpec((tm, n_), lambda i: (i, 0))
    in_specs = (_halo_specs(tm, V7X_SUBLANES, L, DP)
                + [pl.BlockSpec((tm, DS), lambda i: (jnp.minimum(i, n // 2 - 1), 0)),
                   pl.BlockSpec((tm, DS), lambda i: (jnp.maximum(i - n // 2, 0), 0)), row(D)]
                + [_const_spec(a.shape) for a in (pool_w, pool_scale, w_out, g, g_next)])
    est = (2 * tm * (DP * 4 + 2 * DS * 2 + D * 4 + D * 4 + D * 2) + (w_out.size + pool_w.size) * 2
           + (tm + 16) * DP * 4 + 8 * tm * D * 4)
    return pl.pallas_call(
        functools.partial(_mixout_kernel, seq=L),
        grid=(n,),
        in_specs=in_specs,
        out_specs=[row(D), row(D)],
        out_shape=[jax.ShapeDtypeStruct((L, D), F32), jax.ShapeDtypeStruct((L, D), BF16)],
        scratch_shapes=[pltpu.VMEM((tm + 16, DP), F32)],
        compiler_params=pltpu.CompilerParams(dimension_semantics=("arbitrary",), vmem_limit_bytes=_vmem_limit(est)),
        name="mixout",
    )(u, u, u, y_lo, y_hi, x, pool_w, pool_scale, w_out, g, g_next)


def _ffn_kernel(hp_ref, hm_ref, hx_ref, wg_ref, wv_ref, cw_ref, cb_ref, wd_ref, gpost_ref,
                o_ref, hn_ref, gate_ref):
    i = pl.program_id(0)
    f = pl.program_id(1)
    n = pl.num_programs(0)
    nf = pl.num_programs(1)
    tm = hm_ref.shape[0]
    H = BF16_ROWS

    @pl.when(f == 0)
    def _():
        hn_ref[0:H, :] = jnp.where(i > 0, hp_ref[...], jnp.zeros_like(hp_ref))
        hn_ref[H:H + tm, :] = hm_ref[...]
        hn_ref[H + tm:2 * H + tm, :] = jnp.where(i < n - 1, hx_ref[...], jnp.zeros_like(hx_ref))
        o_ref[...] = jnp.zeros_like(o_ref)

    gate_ref[...] = jnp.dot(hn_ref[...], wg_ref[...], preferred_element_type=F32)
    val = jnp.dot(hm_ref[...], wv_ref[...], preferred_element_type=F32)
    half = FFN_CONV // 2
    gc = cw_ref[0:1, :] * gate_ref[H - half:H - half + tm, :]
    for j in range(1, FFN_CONV):
        gc = gc + cw_ref[j:j + 1, :] * gate_ref[H - half + j:H - half + j + tm, :]
    gc = gc + cb_ref[...]
    act = (jax.nn.gelu(gc, approximate=True) * val).astype(BF16)
    o_ref[...] += jnp.dot(act, wd_ref[...], preferred_element_type=F32)

    @pl.when(f == nf - 1)
    def _():
        o_ref[...] = _rms(o_ref[...]) * gpost_ref[...]


def _ffn(hn, w_up, cw, cb, w_down, gpost, *, tm, tf):
    L, D = hn.shape
    DF = w_down.shape[0]
    nf = DF // tf
    H = BF16_ROWS
    in_specs = _halo_specs(tm, H, L, D) + [
        pl.BlockSpec((D, tf), lambda i, f: (0, f)),
        pl.BlockSpec((D, tf), lambda i, f: (0, nf + f)),
        pl.BlockSpec((FFN_CONV, tf), lambda i, f: (0, f)),
        pl.BlockSpec((1, tf), lambda i, f: (0, f)),
        pl.BlockSpec((tf, D), lambda i, f: (f, 0)),
        pl.BlockSpec(gpost.shape, lambda i, f: (0, 0)),
    ]
    est = (2 * (tm + 2 * H) * D * 2 + 2 * tm * D * 4 + (tm + 2 * H) * D * 2 + (tm + 2 * H) * tf * 4
           + 2 * 3 * D * tf * 2 + 8 * tm * tf * 4 + tm * D * 4)
    return pl.pallas_call(
        _ffn_kernel,
        grid=(L // tm, nf),
        in_specs=in_specs,
        out_specs=pl.BlockSpec((tm, D), lambda i, f: (i, 0)),
        out_shape=jax.ShapeDtypeStruct((L, D), F32),
        scratch_shapes=[
            pltpu.VMEM((tm + 2 * H, D), BF16),
            pltpu.VMEM((tm + 2 * H, tf), F32),
        ],
        compiler_params=pltpu.CompilerParams(dimension_semantics=("arbitrary", "arbitrary"),
                                             vmem_limit_bytes=_vmem_limit(est)),
        name="ffn",
    )(hn, hn, hn, w_up, w_up, cw, cb, w_down, gpost)


def _ple_kernel(h_ref, r_ref, p_ref, gpre_ref, wg_ref, wp_ref, gpost_ref, o_ref):
    h = h_ref[...] + r_ref[...]
    hn = (_rms(h) * gpre_ref[...]).astype(BF16)
    gate = jax.nn.sigmoid(jnp.dot(hn, wg_ref[...], preferred_element_type=F32))
    pe = jnp.dot(p_ref[...].astype(BF16), wp_ref[...], preferred_element_type=F32)
    o_ref[...] = h + _rms(gate * pe) * gpost_ref[...]


def _ple(h, r, p, gpre, w_gate, w_ple, gpost, *, tm):
    L, D = h.shape
    row = lambda n: pl.BlockSpec((tm, n), lambda i: (i, 0))
    est = 2 * tm * (3 * D + p.shape[1]) * 4 + (w_gate.size + w_ple.size) * 2 + 8 * tm * D * 4
    return pl.pallas_call(
        _ple_kernel,
        grid=(L // tm,),
        in_specs=[row(D), row(D), row(p.shape[1])] + [_const_spec(a.shape) for a in (gpre, w_gate, w_ple, gpost)],
        out_specs=row(D),
        out_shape=jax.ShapeDtypeStruct((L, D), F32),
        compiler_params=pltpu.CompilerParams(dimension_semantics=("arbitrary",), vmem_limit_bytes=_vmem_limit(est)),
        name="ple",
    )(h, r, p, gpre, w_gate, w_ple, gpost)


def _ssd_constants():
    Q = SSD_CHUNK
    r = jnp.arange(Q)
    lower = (r[:, None] >= r[None, :])
    tri = jnp.stack([lower, lower.T]).astype(BF16)
    tcat = jnp.concatenate([tri, tri, tri], axis=2)
    negmask = jnp.where(jnp.stack([lower, lower.T]), 0.0, NEG_BIG).astype(F32)
    k = jnp.arange(V7X_LANES)
    rexp, ecol = [], []
    for d in range(2):
        packed = jnp.logical_and(k >= d * SSD_HEADS, k < d * SSD_HEADS + 3 * SSD_PACK)
        head_of_lane = jnp.where(packed, (k - d * SSD_HEADS) % SSD_PACK, -1)
        rexp.append(head_of_lane[:, None] == (jnp.arange(SSD_HEADS * SSD_HEAD_DIM) // SSD_HEAD_DIM)[None, :])
        ecol.append(head_of_lane[:, None] == (jnp.arange(SSD_HEADS * Q) // Q)[None, :])
    return tcat, negmask, jnp.stack(rexp).astype(BF16), jnp.stack(ecol).astype(BF16)


def _dir_lanes(v, fill):
    out = jnp.full((1, V7X_LANES), fill, F32)
    return out.at[0, :v.size].set(v.astype(F32).reshape(-1))


def kernel(x, p, mix_norm_pre, mix_norm_post, w_in, ssd_conv_w, ssd_conv_b, ssd_dt_bias, ssd_a_log, ssd_d,
           ssd_norm, pool_w, pool_scale, w_out, ffn_norm_pre, ffn_norm_post, w_ffn_up, ffn_conv_w, ffn_conv_b,
           w_ffn_down, ple_norm_pre, w_ple_gate, w_ple, ple_norm_post):
    B, L, D = x.shape
    depth = w_in.shape[0]
    d_ssd = SSD_HEADS * SSD_HEAD_DIM
    n_bc = 2 * SSD_GROUPS * SSD_STATE
    o_dt = 2 * d_ssd + n_bc
    o_u = o_dt + 2 * SSD_HEADS
    d_pool = w_in.shape[2] - o_u
    tcat, negmask, rexp, ecol = _ssd_constants()
    row = lambda v: v.reshape(1, -1).astype(F32)

    outs = []
    for b in range(B):
        h = x[b]
        for i in range(depth):
            xs, bc, z, u, dt_raw, w_down16, w_out16, w_gate16 = _inproj(
                h, row(mix_norm_pre[i]), _cast_transposed(w_in[i].T, cols=V7X_LANES), ssd_conv_w[i].astype(F32),
                row(ssd_conv_b[i]),
                (d_ssd, n_bc, d_ssd, d_pool, V7X_LANES), w_ffn_down[i], w_out[i], w_ple_gate[i], tm=INPROJ_ROWS)
            y_lo, y_hi, w_up16 = _ssd(
                xs, bc, dt_raw, z, _dir_lanes(ssd_dt_bias[i], 0.0), _dir_lanes(ssd_a_log[i], NEG_BIG),
                jnp.repeat(ssd_d[i].astype(F32), SSD_HEAD_DIM).reshape(1, -1), row(ssd_norm[i]),
                tcat, negmask, rexp, ecol, w_ffn_up[i])
            h, hn = _mixout(u, y_lo, y_hi, h, pool_w[i].astype(BF16), row(pool_scale[i]), w_out16,
                            row(mix_norm_post[i]), row(ffn_norm_pre[i]), tm=MIXOUT_ROWS)
            r = _ffn(hn, w_up16, ffn_conv_w[i].astype(F32), row(ffn_conv_b[i]), w_down16, row(ffn_norm_post[i]),
                     tm=FFN_ROWS, tf=FFN_COLS)
            h = _ple(h, r, p[i, b], row(ple_norm_pre[i]), w_gate16, w_ple[i].astype(BF16),
                     row(ple_norm_post[i]), tm=PLE_ROWS)
        outs.append(h)
    return jnp.stack(outs)
```

```python
import functools

import jax
import jax.numpy as jnp
from jax import lax
from jax.experimental import pallas as pl
from jax.experimental.pallas import tpu as pltpu

F32 = jnp.float32
BF16 = jnp.bfloat16
EPS = 1e-6

V7X_VMEM_BYTES = 64 * 1024 * 1024
V7X_LANES = 128
V7X_SUBLANES = 8
BF16_ROWS = 16

SSD_HEAD_DIM = 64
SSD_HEADS = 16
SSD_GROUPS = 2
SSD_HEADS_PER_GROUP = SSD_HEADS // SSD_GROUPS
SSD_STATE = 128
SSD_CONV = 5
SSD_CHUNK = 128
POOL_WINDOWS = (2, 4, 8, 16)
FFN_CONV = 3
NEG_BIG = -1e30

INPROJ_ROWS = 512
MIXOUT_ROWS = 512
FFN_ROWS = 1024
FFN_COLS = 512
PLE_ROWS = 512
CONV_PIECE = 256


def _vmem_limit(nbytes):
    return int(min(nbytes, V7X_VMEM_BYTES - 6 * 1024 * 1024))


def _rms(x):
    return x * lax.rsqrt(jnp.mean(x * x, axis=-1, keepdims=True) + EPS)


def _split3(x):
    hi = x.astype(BF16).astype(F32)
    r1 = x - hi
    mid = r1.astype(BF16).astype(F32)
    lo = (r1 - mid).astype(BF16).astype(F32)
    return hi, mid, lo


def _const_spec(shape):
    nd = len(shape)
    return pl.BlockSpec(shape, lambda *_: (0,) * nd, pipeline_mode=pl.Buffered(1))


def _halo_specs(rows, halo, total_rows, ncols):
    hb = rows // halo
    nhb = total_rows // halo
    return [
        pl.BlockSpec((halo, ncols), lambda i, *_: (jnp.maximum(i * hb - 1, 0), 0)),
        pl.BlockSpec((rows, ncols), lambda i, *_: (i, 0)),
        pl.BlockSpec((halo, ncols), lambda i, *_: (jnp.minimum(i * hb + hb, nhb - 1), 0)),
    ]


def _cast_t_kernel(w_ref, o_ref):
    o_ref[...] = w_ref[...].T.astype(o_ref.dtype)


def _cast_transposed(w_t, *, cols):
    C, R = w_t.shape
    n = pl.cdiv(C, cols)
    return pl.pallas_call(
        _cast_t_kernel,
        grid=(n,),
        in_specs=[pl.BlockSpec((cols, R), lambda j: (j, 0))],
        out_specs=pl.BlockSpec((R, cols), lambda j: (0, j)),
        out_shape=jax.ShapeDtypeStruct((R, n * cols), BF16),
        compiler_params=pltpu.CompilerParams(dimension_semantics=("arbitrary",),
                                             vmem_limit_bytes=_vmem_limit(V7X_VMEM_BYTES)),
        name="cast_w_in",
    )(w_t)


def _inproj_kernel(xp_ref, xm_ref, xn_ref, g_ref, w_ref, cw_ref, cb_ref, wd_ref, wo_ref, wg_ref,
                   xs_ref, bc_ref, z_ref, u_ref, dt_ref, wd16_ref, wo16_ref, wg16_ref,
                   hn_ref, *ext_refs):
    i = pl.program_id(0)
    n = pl.num_programs(0)
    tm = xm_ref.shape[0]
    H = BF16_ROWS
    g = g_ref[...]
    hn_ref[0:H, :] = jnp.where(i > 0, _rms(xp_ref[...]) * g, 0.0).astype(BF16)
    hn_ref[H:H + tm, :] = (_rms(xm_ref[...]) * g).astype(BF16)
    hn_ref[H + tm:2 * H + tm, :] = jnp.where(i < n - 1, _rms(xn_ref[...]) * g, 0.0).astype(BF16)
    for src, dst in ((wd_ref, wd16_ref), (wo_ref, wo16_ref), (wg_ref, wg16_ref)):
        dst[...] = src[...].astype(BF16)

    cs = CONV_PIECE
    nz, nxs, nbc, nu = z_ref.shape[1], xs_ref.shape[1], bc_ref.shape[1], u_ref.shape[1]
    o_dt = nz + nxs + nbc
    n_dt = 2 * SSD_HEADS
    w_end = o_dt + n_dt + nu
    conv_outs = [(xs_ref, c, nz + c, c) for c in range(0, nxs, cs)]
    conv_outs += [(bc_ref, c, nz + nxs + c, nxs + c) for c in range(0, nbc, cs)]

    def project(c0, c1):
        return jnp.dot(hn_ref[H:H + tm, :], w_ref[:, c0:c1], preferred_element_type=F32)

    def z_chunk(c):
        z_ref[:, c:c + cs] = project(c, c + cs)

    tail = []
    split = o_dt + 2 * cs
    plain = [functools.partial(z_chunk, c) for c in range(0, nz, cs)]
    plain += [lambda: tail.append(project(o_dt, split)), lambda: tail.append(project(split, w_end))]

    half = SSD_CONV // 2
    for k, (o_ref, c0, wc, cc) in enumerate(conv_outs):
        ext_ref = ext_refs[k % len(ext_refs)]
        ext_ref[...] = jnp.dot(hn_ref[...], w_ref[:, wc:wc + cs], preferred_element_type=F32)
        if k < len(plain):
            plain[k]()
        acc = cw_ref[0:1, cc:cc + cs] * ext_ref[H - half:H - half + tm, :]
        for j in range(1, SSD_CONV):
            acc = acc + cw_ref[j:j + 1, cc:cc + cs] * ext_ref[H - half + j:H - half + j + tm, :]
        acc = acc + cb_ref[:, cc:cc + cs]
        o_ref[:, c0:c0 + cs] = (acc * jax.nn.sigmoid(acc)).astype(o_ref.dtype)
    for fn in plain[len(conv_outs):]:
        fn()
    dt_ref[...] = tail[0][:, 0:dt_ref.shape[1]]
    u_ref[...] = jnp.concatenate([tail[0][:, n_dt:], tail[1]], axis=1)


def _inproj(x, g, w, cw, cb, out_cols, w_down, w_out, w_gate, *, tm):
    L, D = x.shape
    H = BF16_ROWS
    n = L // tm
    dts = (BF16, BF16, F32, F32, F32)
    row = lambda c: pl.BlockSpec((tm, c), lambda i: (i, 0))
    slab = lambda a: pl.BlockSpec((a.shape[0] // n, a.shape[1]), lambda i: (i, 0))
    casts = (w_down, w_out, w_gate)
    est = (2 * (tm + 2 * H) * D * 4 + w.size * 2 + 2 * tm * sum(out_cols) * 4
           + (tm + 2 * H) * D * 2 + 4 * (tm + 2 * H) * CONV_PIECE * 4 + 2 * sum(a.size for a in casts) // n * 6
           + 16 * tm * CONV_PIECE * 4)
    return pl.pallas_call(
        _inproj_kernel,
        grid=(n,),
        in_specs=_halo_specs(tm, H, L, D) + [_const_spec(a.shape) for a in (g, w, cw, cb)]
                 + [slab(a) for a in casts],
        out_specs=[row(c) for c in out_cols] + [slab(a) for a in casts],
        out_shape=[jax.ShapeDtypeStruct((L, c), dt) for c, dt in zip(out_cols, dts)]
                  + [jax.ShapeDtypeStruct(a.shape, BF16) for a in casts],
        scratch_shapes=[pltpu.VMEM((tm + 2 * H, D), BF16)]
                       + [pltpu.VMEM((tm + 2 * H, CONV_PIECE), F32) for _ in range(2)],
        compiler_params=pltpu.CompilerParams(dimension_semantics=("arbitrary",), vmem_limit_bytes=_vmem_limit(est)),
        name="inproj",
    )(x, x, x, g, w, cw, cb, *casts)


SSD_PACK = 16


LOG2E = 1.4426950408889634


def _pack3(v, head_lanes):
    hi, mid, lo = _split3(jnp.where(head_lanes, v, jnp.zeros_like(v)))
    packed = hi + pltpu.roll(mid, SSD_PACK, axis=1) + pltpu.roll(lo, 2 * SSD_PACK, axis=1)
    return packed.astype(BF16)


def _scan_both(xs_refs, bc_refs, dt_refs, dtb_ref, alog_ref, t_ref, nm_ref, rexp_ref, ecol_ref, h_ref):
    Q = SSD_CHUNK
    P = SSD_HEAD_DIM
    E = SSD_HEADS_PER_GROUP
    GW = E * P
    DIRS = (0, 1)
    last = (Q - 1, 0)
    xs16 = [xs_refs[d][...] for d in DIRS]
    bc =[bc_refs[d][...] for d in DIRS]

    lane = lax.broadcasted_iota(jnp.int32, (Q, V7X_LANES), 1)
    head_lanes = [jnp.logical_and(lane >= d * SSD_HEADS, lane < (d + 1) * SSD_HEADS) for d in DIRS]
    a = -jnp.exp(alog_ref[...])
    dt = [jax.nn.softplus(dt_refs[d][...] + dtb_ref[...]) for d in DIRS]
    da = [dt[d] * a for d in DIRS]
    da3 = [jnp.concatenate([v.astype(BF16) for v in _split3(da[d])], axis=0) for d in DIRS]
    acum = [jnp.dot(t_ref[d], da3[d], preferred_element_type=F32) * LOG2E for d in DIRS]
    tot = [acum[d][last[d]:last[d] + 1, :] for d in DIRS]
    src_t = [(acum[d] - jnp.log2(dt[d])).T for d in DIRS]
    e_a = [jnp.exp2(acum[d]) for d in DIRS]
    w_end = [jnp.exp2(tot[d] - acum[d]) for d in DIRS]

    lhs = [jnp.concatenate([_pack3(v, head_lanes[d]) for v in (dt[d] * w_end[d], e_a[d])], axis=0)
           for d in DIRS]
    exp2x = [jnp.dot(lhs[d], rexp_ref[d], preferred_element_type=F32) for d in DIRS]
    acol = [jnp.dot(_pack3(acum[d], head_lanes[d]), ecol_ref[d], preferred_element_type=F32) for d in DIRS]
    xw = [xs16[d] * exp2x[d][0:Q].astype(BF16) for d in DIRS]
    ea_x = [exp2x[d][Q:2 * Q] for d in DIRS]
    negmask = [nm_ref[d] for d in DIRS]
    first_head = lax.broadcasted_iota(jnp.int32, (Q, 2 * P), 1) < P

    y_groups = [[], []]
    for g in range(SSD_GROUPS):
        c0 = g * GW
        bm = [bc[d][:, g * SSD_STATE:(g + 1) * SSD_STATE] for d in DIRS]
        cm = [bc[d][:, (SSD_GROUPS + g) * SSD_STATE:(SSD_GROUPS + g + 1) * SSD_STATE] for d in DIRS]
        cb = [lax.dot_general(cm[d], bm[d], (((1,), (1,)), ((), ())), preferred_element_type=F32)
              for d in DIRS]
        st = [lax.dot_general(bm[d], xw[d][:, c0:c0 + GW], (((0,), (0,)), ((), ())), preferred_element_type=F32)
              for d in DIRS]
        h_in = [h_ref[d, g] for d in DIRS]
        y_off = [jnp.dot(cm[d], h_in[d].astype(BF16), preferred_element_type=F32) for d in DIRS]
        y_g = [y_off[d] * ea_x[d][:, c0:c0 + GW] for d in DIRS]
        for d in DIRS:
            h_ref[d, g] = h_in[d] * ea_x[d][last[d]:last[d] + 1, c0:c0 + GW] + st[d]
        pairs = [[], []]
        for hp in range(E // 2):
            for d in DIRS:
                h0 = g * E + hp * 2
                ms = []
                for k in range(2):
                    r = d * SSD_HEADS + h0 + k
                    seg = acol[d][:, (h0 + k) * Q:(h0 + k + 1) * Q] - src_t[d][r:r + 1, :] + negmask[d]
                    ms.append((cb[d] * jnp.exp2(seg)).astype(BF16))
                m2 = jnp.concatenate(ms, axis=1)
                xp = xs16[d][:, (g * E + hp * 2) * P:(g * E + hp * 2 + 2) * P]
                zero = jnp.zeros_like(xp)
                rhs = jnp.concatenate([jnp.where(first_head, xp, zero), jnp.where(first_head, zero, xp)], axis=0)
                pairs[d].append(jnp.dot(m2, rhs, preferred_element_type=F32))
        for d in DIRS:
            y_groups[d].append(y_g[d] + jnp.concatenate(pairs[d], axis=1))
    return [(jnp.concatenate(y_groups[d], axis=1), xs16[d]) for d in DIRS]


def _ssd_kernel(xsf_ref, bcf_ref, dtf_ref, zf_ref, xsb_ref, bcb_ref, dtb_raw_ref, zb_ref,
                dtb_ref, alog_ref, dexp_ref, nw_ref, t_ref, nm_ref, rexp_ref, ecol_ref, wup_ref,
                lo_ref, hi_ref, wup16_ref,
                y_ref, h_ref):
    Q = SSD_CHUNK
    GW = SSD_HEADS_PER_GROUP * SSD_HEAD_DIM
    i = pl.program_id(0)
    nc = pl.num_programs(0)

    wup16_ref[...] = wup_ref[...].astype(BF16)

    @pl.when(i == 0)
    def _():
        h_ref[...] = jnp.zeros_like(h_ref)

    (y_f, xs_f), (y_b, xs_b) = _scan_both((xsf_ref, xsb_ref), (bcf_ref, bcb_ref), (dtf_ref, dtb_raw_ref),
                                          dtb_ref, alog_ref, t_ref, nm_ref, rexp_ref, ecol_ref, h_ref)
    row_f = pl.multiple_of(i * Q, Q)
    row_b = pl.multiple_of((nc - 1 - i) * Q, Q)

    @pl.when(i < nc // 2)
    def _():
        y_ref[pl.ds(row_f, Q), :] = y_f
        y_ref[pl.ds(row_b, Q), :] = y_b

    def finish(y, xs, z_ref, o_ref):
        yy = y + xs.astype(F32) * dexp_ref[...]
        z = z_ref[...]
        yy = yy * (z * jax.nn.sigmoid(z))
        outs = [_rms(yy[:, g * GW:(g + 1) * GW]) for g in range(SSD_GROUPS)]
        o_ref[...] = (jnp.concatenate(outs, axis=1) * nw_ref[...]).astype(o_ref.dtype)

    @pl.when(i >= nc // 2)
    def _():
        finish(y_f + y_ref[pl.ds(row_f, Q), :], xs_f, zf_ref, hi_ref)
        finish(y_b + y_ref[pl.ds(row_b, Q), :], xs_b, zb_ref, lo_ref)


def _ssd(xs, bc, dt_raw, z, dtb, alog, dexp, nw, tcat, negmask, rexp, ecol, w_up):
    L, DS = xs.shape
    Q = SSD_CHUNK
    nc = L // Q
    hc = nc // 2
    slab = pl.BlockSpec((w_up.shape[0] // nc, w_up.shape[1]), lambda i: (i, 0))
    fwd = lambda i: i
    bwd = lambda i: nc - 1 - i
    fwd_late = lambda i: jnp.maximum(i, hc)
    bwd_late = lambda i: jnp.minimum(nc - 1 - i, hc - 1)
    in_specs = [
        pl.BlockSpec((Q, DS), lambda i: (fwd(i), 0)),
        pl.BlockSpec((Q, bc.shape[1]), lambda i: (fwd(i), 0)),
        pl.BlockSpec((Q, V7X_LANES), lambda i: (fwd(i), 0)),
        pl.BlockSpec((Q, DS), lambda i: (fwd_late(i), 0)),
        pl.BlockSpec((Q, DS), lambda i: (bwd(i), 0)),
        pl.BlockSpec((Q, bc.shape[1]), lambda i: (bwd(i), 0)),
        pl.BlockSpec((Q, V7X_LANES), lambda i: (bwd(i), 0)),
        pl.BlockSpec((Q, DS), lambda i: (bwd_late(i), 0)),
    ] + [_const_spec(a.shape) for a in (dtb, alog, dexp, nw, tcat, negmask, rexp, ecol)] + [slab]
    est = (L * DS * 4 + 2 * SSD_GROUPS * SSD_STATE * DS * 4 + 8 * Q * (DS + bc.shape[1]) * 4
           + (rexp.size + ecol.size) * 2 + 4 * Q * DS * 4 * 2 + 40 * Q * DS * 4 + 12 * w_up.size // nc)
    return pl.pallas_call(
        _ssd_kernel,
        grid=(nc,),
        in_specs=in_specs,
        out_specs=[pl.BlockSpec((Q, DS), lambda i: (bwd_late(i), 0)),
                   pl.BlockSpec((Q, DS), lambda i: (fwd_late(i) - hc, 0)), slab],
        out_shape=[jax.ShapeDtypeStruct((L // 2, DS), BF16), jax.ShapeDtypeStruct((L // 2, DS), BF16),
                   jax.ShapeDtypeStruct(w_up.shape, BF16)],
        scratch_shapes=[
            pltpu.VMEM((L, DS), F32),
            pltpu.VMEM((2, SSD_GROUPS, SSD_STATE, DS // SSD_GROUPS), F32),
        ],
        compiler_params=pltpu.CompilerParams(dimension_semantics=("arbitrary",), vmem_limit_bytes=_vmem_limit(est)),
        name="ssd",
    )(xs, bc, dt_raw, z, xs, bc, dt_raw, z, dtb, alog, dexp, nw, tcat, negmask, rexp, ecol, w_up)


def _mixout_kernel(up_ref, um_ref, un_ref, ylo_ref, yhi_ref, x_ref, pw_ref, ps_ref, wo_ref, g_ref, gn_ref,
                   o_ref, on_ref, ext_ref, *, seq):
    i = pl.program_id(0)
    n = pl.num_programs(0)
    tm = um_ref.shape[0]
    cg = um_ref.shape[1] // len(POOL_WINDOWS)
    ext_ref[0:8, :] = jnp.where(i > 0, up_ref[...], 0.0)
    ext_ref[8:8 + tm, :] = um_ref[...]
    ext_ref[8 + tm:16 + tm, :] = jnp.where(i < n - 1, un_ref[...], 0.0)
    ys = jnp.where(i < n // 2, ylo_ref[...], yhi_ref[...])
    ds = ys.shape[1]
    mix = jnp.dot(ys, wo_ref[0:ds, :], preferred_element_type=F32)
    t = i * tm + lax.broadcasted_iota(jnp.int32, (tm, cg), 0)
    rows = tm + 16

    def ahead(v, k):
        return pltpu.roll(v, (rows - k) % rows, axis=0)

    pooled = []
    for gi, k in enumerate(POOL_WINDOWS):
        cols = slice(gi * cg, (gi + 1) * cg)
        e = ext_ref[:, cols]
        half = k // 2
        run, length = e, 1
        while length < half:
            run = run + ahead(run, length)
            length *= 2
        before = run[0:tm] if half == 8 else ahead(run, rows - half)[8:8 + tm]
        acc = before + run[8:8 + tm]
        cnt = (jnp.minimum(t + (k - k // 2), seq) - jnp.maximum(t - k // 2, 0)).astype(F32)
        mixed = acc / cnt - um_ref[:, cols]
        yp = jnp.dot(mixed.astype(BF16), pw_ref[gi], preferred_element_type=F32) * ps_ref[:, cols]
        pooled.append(yp.astype(BF16))
    ypool = jnp.concatenate(pooled, axis=1)
    mix = mix + jnp.dot(ypool, wo_ref[ds:, :], preferred_element_type=F32)
    h = x_ref[...] + _rms(mix) * g_ref[...]
    o_ref[...] = h
    on_ref[...] = (_rms(h) * gn_ref[...]).astype(on_ref.dtype)


def _mixout(u, y_lo, y_hi, x, pool_w, pool_scale, w_out, g, g_next, *, tm):
    L, DP = u.shape
    D = x.shape[1]
    DS = y_lo.shape[1]
    n = L // tm
    row = lambda n_: pl.BlockSpec((tm, n_), lambda i: (i, 0))
    in_specs = (_halo_specs(tm, V7X_SUBLANES, L, DP)
                + [pl.BlockSpec((tm, DS), lambda i: (jnp.minimum(i, n // 2 - 1), 0)),
                   pl.BlockSpec((tm, DS), lambda i: (jnp.maximum(i - n // 2, 0), 0)), row(D)]
                + [_const_spec(a.shape) for a in (pool_w, pool_scale, w_out, g, g_next)])
    est = (2 * tm * (DP * 4 + 2 * DS * 2 + D * 4 + D * 4 + D * 2) + (w_out.size + pool_w.size) * 2
           + (tm + 16) * DP * 4 + 8 * tm * D * 4)
    return pl.pallas_call(
        functools.partial(_mixout_kernel, seq=L),
        grid=(n,),
        in_specs=in_specs,
        out_specs=[row(D), row(D)],
        out_shape=[jax.ShapeDtypeStruct((L, D), F32), jax.ShapeDtypeStruct((L, D), BF16)],
        scratch_shapes=[pltpu.VMEM((tm + 16, DP), F32)],
        compiler_params=pltpu.CompilerParams(dimension_semantics=("arbitrary",), vmem_limit_bytes=_vmem_limit(est)),
        name="mixout",
    )(u, u, u, y_lo, y_hi, x, pool_w, pool_scale, w_out, g, g_next)


def _ffn_kernel(hp_ref, hm_ref, hx_ref, wg_ref, wv_ref, cw_ref, cb_ref, wd_ref, gpost_ref,
                o_ref, hn_ref, gate_ref):
    i = pl.program_id(0)
    f = pl.program_id(1)
    n = pl.num_programs(0)
    nf = pl.num_programs(1)
    tm = hm_ref.shape[0]
    H = BF16_ROWS

    @pl.when(f == 0)
    def _():
        hn_ref[0:H, :] = jnp.where(i > 0, hp_ref[...], jnp.zeros_like(hp_ref))
        hn_ref[H:H + tm, :] = hm_ref[...]
        hn_ref[H + tm:2 * H + tm, :] = jnp.where(i < n - 1, hx_ref[...], jnp.zeros_like(hx_ref))
        o_ref[...] = jnp.zeros_like(o_ref)

    gate_ref[...] = jnp.dot(hn_ref[...], wg_ref[...], preferred_element_type=F32)
    val = jnp.dot(hm_ref[...], wv_ref[...], preferred_element_type=F32)
    half = FFN_CONV // 2
    gc = cw_ref[0:1, :] * gate_ref[H - half:H - half + tm, :]
    for j in range(1, FFN_CONV):
        gc = gc + cw_ref[j:j + 1, :] * gate_ref[H - half + j:H - half + j + tm, :]
    gc = gc + cb_ref[...]
    act = (jax.nn.gelu(gc, approximate=True) * val).astype(BF16)
    o_ref[...] += jnp.dot(act, wd_ref[...], preferred_element_type=F32)

    @pl.when(f == nf - 1)
    def _():
        o_ref[...] = _rms(o_ref[...]) * gpost_ref[...]


def _ffn(hn, w_up, cw, cb, w_down, gpost, *, tm, tf):
    L, D = hn.shape
    DF = w_down.shape[0]
    nf = DF // tf
    H = BF16_ROWS
    in_specs = _halo_specs(tm, H, L, D) + [
        pl.BlockSpec((D, tf), lambda i, f: (0, f)),
        pl.BlockSpec((D, tf), lambda i, f: (0, nf + f)),
        pl.BlockSpec((FFN_CONV, tf), lambda i, f: (0, f)),
        pl.BlockSpec((1, tf), lambda i, f: (0, f)),
        pl.BlockSpec((tf, D), lambda i, f: (f, 0)),
        pl.BlockSpec(gpost.shape, lambda i, f: (0, 0)),
    ]
    est = (2 * (tm + 2 * H) * D * 2 + 2 * tm * D * 4 + (tm + 2 * H) * D * 2 + (tm + 2 * H) * tf * 4
           + 2 * 3 * D * tf * 2 + 8 * tm * tf * 4 + tm * D * 4)
    return pl.pallas_call(
        _ffn_kernel,
        grid=(L // tm, nf),
        in_specs=in_specs,
        out_specs=pl.BlockSpec((tm, D), lambda i, f: (i, 0)),
        out_shape=jax.ShapeDtypeStruct((L, D), F32),
        scratch_shapes=[
            pltpu.VMEM((tm + 2 * H, D), BF16),
            pltpu.VMEM((tm + 2 * H, tf), F32),
        ],
        compiler_params=pltpu.CompilerParams(dimension_semantics=("arbitrary", "arbitrary"),
                                             vmem_limit_bytes=_vmem_limit(est)),
        name="ffn",
    )(hn, hn, hn, w_up, w_up, cw, cb, w_down, gpost)


def _ple_kernel(h_ref, r_ref, p_ref, gpre_ref, wg_ref, wp_ref, gpost_ref, o_ref):
    h = h_ref[...] + r_ref[...]
    hn = (_rms(h) * gpre_ref[...]).astype(BF16)
    gate = jax.nn.sigmoid(jnp.dot(hn, wg_ref[...], preferred_element_type=F32))
    pe = jnp.dot(p_ref[...].astype(BF16), wp_ref[...], preferred_element_type=F32)
    o_ref[...] = h + _rms(gate * pe) * gpost_ref[...]


def _ple(h, r, p, gpre, w_gate, w_ple, gpost, *, tm):
    L, D = h.shape
    row = lambda n: pl.BlockSpec((tm, n), lambda i: (i, 0))
    est = 2 * tm * (3 * D + p.shape[1]) * 4 + (w_gate.size + w_ple.size) * 2 + 8 * tm * D * 4
    return pl.pallas_call(
        _ple_kernel,
        grid=(L // tm,),
        in_specs=[row(D), row(D), row(p.shape[1])] + [_const_spec(a.shape) for a in (gpre, w_gate, w_ple, gpost)],
        out_specs=row(D),
        out_shape=jax.ShapeDtypeStruct((L, D), F32),
        compiler_params=pltpu.CompilerParams(dimension_semantics=("arbitrary",), vmem_limit_bytes=_vmem_limit(est)),
        name="ple",
    )(h, r, p, gpre, w_gate, w_ple, gpost)


def _ssd_constants():
    Q = SSD_CHUNK
    r = jnp.arange(Q)
    lower = (r[:, None] >= r[None, :])
    tri = jnp.stack([lower, lower.T]).astype(BF16)
    tcat = jnp.concatenate([tri, tri, tri], axis=2)
    negmask = jnp.where(jnp.stack([lower, lower.T]), 0.0, NEG_BIG).astype(F32)
    k = jnp.arange(V7X_LANES)
    rexp, ecol = [], []
    for d in range(2):
        packed = jnp.logical_and(k >= d * SSD_HEADS, k < d * SSD_HEADS + 3 * SSD_PACK)
        head_of_lane = jnp.where(packed, (k - d * SSD_HEADS) % SSD_PACK, -1)
        rexp.append(head_of_lane[:, None] == (jnp.arange(SSD_HEADS * SSD_HEAD_DIM) // SSD_HEAD_DIM)[None, :])
        ecol.append(head_of_lane[:, None] == (jnp.arange(SSD_HEADS * Q) // Q)[None, :])
    return tcat, negmask, jnp.stack(rexp).astype(BF16), jnp.stack(ecol).astype(BF16)


def _dir_lanes(v, fill):
    out = jnp.full((1, V7X_LANES), fill, F32)
    return out.at[0, :v.size].set(v.astype(F32).reshape(-1))


def kernel(x, p, mix_norm_pre, mix_norm_post, w_in, ssd_conv_w, ssd_conv_b, ssd_dt_bias, ssd_a_log, ssd_d,
           ssd_norm, pool_w, pool_scale, w_out, ffn_norm_pre, ffn_norm_post, w_ffn_up, ffn_conv_w, ffn_conv_b,
           w_ffn_down, ple_norm_pre, w_ple_gate, w_ple, ple_norm_post):
    B, L, D = x.shape
    depth = w_in.shape[0]
    d_ssd = SSD_HEADS * SSD_HEAD_DIM
    n_bc = 2 * SSD_GROUPS * SSD_STATE
    o_dt = 2 * d_ssd + n_bc
    o_u = o_dt + 2 * SSD_HEADS
    d_pool = w_in.shape[2] - o_u
    tcat, negmask, rexp, ecol = _ssd_constants()
    row = lambda v: v.reshape(1, -1).astype(F32)

    outs = []
    for b in range(B):
        h = x[b]
        for i in range(depth):
            xs, bc, z, u, dt_raw, w_down16, w_out16, w_gate16 = _inproj(
                h, row(mix_norm_pre[i]), _cast_transposed(w_in[i].T, cols=CONV_PIECE), ssd_conv_w[i].astype(F32),
                row(ssd_conv_b[i]),
                (d_ssd, n_bc, d_ssd, d_pool, V7X_LANES), w_ffn_down[i], w_out[i], w_ple_gate[i], tm=INPROJ_ROWS)
            y_lo, y_hi, w_up16 = _ssd(
                xs, bc, dt_raw, z, _dir_lanes(ssd_dt_bias[i], 0.0), _dir_lanes(ssd_a_log[i], NEG_BIG),
                jnp.repeat(ssd_d[i].astype(F32), SSD_HEAD_DIM).reshape(1, -1), row(ssd_norm[i]),
                tcat, negmask, rexp, ecol, w_ffn_up[i])
            h, hn = _mixout(u, y_lo, y_hi, h, pool_w[i].astype(BF16), row(pool_scale[i]), w_out16,
                            row(mix_norm_post[i]), row(ffn_norm_pre[i]), tm=MIXOUT_ROWS)
            r = _ffn(hn, w_up16, ffn_conv_w[i].astype(F32), row(ffn_conv_b[i]), w_down16, row(ffn_norm_post[i]),
                     tm=FFN_ROWS, tf=FFN_COLS)
            h = _ple(h, r, p[i, b], row(ple_norm_pre[i]), w_gate16, w_ple[i].astype(BF16),
                     row(ple_norm_post[i]), tm=PLE_ROWS)
        outs.append(h)
    return jnp.stack(outs)
```

```python
import functools

import jax
import jax.numpy as jnp
from jax import lax
from jax.experimental import pallas as pl
from jax.experimental.pallas import tpu as pltpu

F32 = jnp.float32
BF16 = jnp.bfloat16
EPS = 1e-6

V7X_VMEM_BYTES = 64 * 1024 * 1024
V7X_LANES = 128
V7X_SUBLANES = 8
BF16_ROWS = 16

SSD_HEAD_DIM = 64
SSD_HEADS = 16
SSD_GROUPS = 2
SSD_HEADS_PER_GROUP = SSD_HEADS // SSD_GROUPS
SSD_STATE = 128
SSD_CONV = 5
SSD_CHUNK = 128
POOL_WINDOWS = (2, 4, 8, 16)
FFN_CONV = 3
NEG_BIG = -1e30

INPROJ_ROWS = 512
MIXOUT_ROWS = 512
FFN_ROWS = 1024
FFN_COLS = 512
PLE_ROWS = 512
CONV_PIECE = 256


def _vmem_limit(nbytes):
    return int(min(nbytes, V7X_VMEM_BYTES - 6 * 1024 * 1024))


def _rms(x):
    return x * lax.rsqrt(jnp.mean(x * x, axis=-1, keepdims=True) + EPS)


def _split3(x):
    hi = x.astype(BF16).astype(F32)
    r1 = x - hi
    mid = r1.astype(BF16).astype(F32)
    lo = (r1 - mid).astype(BF16).astype(F32)
    return hi, mid, lo


def _const_spec(shape):
    nd = len(shape)
    return pl.BlockSpec(shape, lambda *_: (0,) * nd, pipeline_mode=pl.Buffered(1))


def _halo_specs(rows, halo, total_rows, ncols):
    hb = rows // halo
    nhb = total_rows // halo
    return [
        pl.BlockSpec((halo, ncols), lambda i, *_: (jnp.maximum(i * hb - 1, 0), 0)),
        pl.BlockSpec((rows, ncols), lambda i, *_: (i, 0)),
        pl.BlockSpec((halo, ncols), lambda i, *_: (jnp.minimum(i * hb + hb, nhb - 1), 0)),
    ]


def _cast_t_kernel(w_ref, o_ref, *, valid_rows):
    row = pl.program_id(0) * w_ref.shape[0] + lax.broadcasted_iota(jnp.int32, w_ref.shape, 0)
    w = jnp.where(row < valid_rows, w_ref[...], 0.0)
    o_ref[...] = w.T.astype(o_ref.dtype)


def _cast_transposed(w_t, *, cols):
    C, R = w_t.shape
    n = pl.cdiv(C, cols)
    return pl.pallas_call(
        functools.partial(_cast_t_kernel, valid_rows=C),
        grid=(n,),
        in_specs=[pl.BlockSpec((cols, R), lambda j: (j, 0))],
        out_specs=pl.BlockSpec((R, cols), lambda j: (0, j)),
        out_shape=jax.ShapeDtypeStruct((R, n * cols), BF16),
        compiler_params=pltpu.CompilerParams(dimension_semantics=("arbitrary",),
                                             vmem_limit_bytes=_vmem_limit(V7X_VMEM_BYTES)),
        name="cast_w_in",
    )(w_t)


def _scan_prep(raw, dtb_ref, alog_ref, t_ref, pk_ref, sk_ref):
    Q = SSD_CHUNK
    lane = lax.broadcasted_iota(jnp.int32, (Q, V7X_LANES), 1)
    head_lanes = [jnp.logical_and(lane >= d * SSD_HEADS, lane < (d + 1) * SSD_HEADS) for d in (0, 1)]
    a = -jnp.exp(alog_ref[...])
    for c in range(raw.shape[0] // Q):
        dt = jax.nn.softplus(raw[c * Q:(c + 1) * Q, :] + dtb_ref[...])
        da3 = jnp.concatenate([v.astype(BF16) for v in _split3(dt * a)], axis=0)
        acum = jnp.where(head_lanes[0], jnp.dot(t_ref[0], da3, preferred_element_type=F32),
                         jnp.dot(t_ref[1], da3, preferred_element_type=F32)) * LOG2E
        tot = jnp.where(head_lanes[0][0:1], acum[Q - 1:Q, :], acum[0:1, :])
        sk_ref[c] = (acum - jnp.log2(dt)).T[0:2 * SSD_HEADS, :]
        e_a = jnp.exp2(acum)
        dtw = dt * jnp.exp2(tot - acum)
        for d in (0, 1):
            pk_ref[c, d, 0:Q, :] = _pack3(dtw, head_lanes[d])
            pk_ref[c, d, Q:2 * Q, :] = _pack3(e_a, head_lanes[d])
            pk_ref[c, d, 2 * Q:3 * Q, :] = _pack3(acum, head_lanes[d])


def _inproj_kernel(xp_ref, xm_ref, xn_ref, g_ref, w_ref, cw_ref, cb_ref, dtb_ref, alog_ref, t_ref,
                   wd_ref, wo_ref, wg_ref,
                   xs_ref, bc_ref, z_ref, u_ref, pk_ref, sk_ref, wd16_ref, wo16_ref, wg16_ref,
                   hn_ref, *ext_refs):
    i = pl.program_id(0)
    n = pl.num_programs(0)
    tm = xm_ref.shape[0]
    H = BF16_ROWS
    g = g_ref[...]
    hn_ref[0:H, :] = jnp.where(i > 0, _rms(xp_ref[...]) * g, 0.0).astype(BF16)
    hn_ref[H:H + tm, :] = (_rms(xm_ref[...]) * g).astype(BF16)
    hn_ref[H + tm:2 * H + tm, :] = jnp.where(i < n - 1, _rms(xn_ref[...]) * g, 0.0).astype(BF16)
    for src, dst in ((wd_ref, wd16_ref), (wo_ref, wo16_ref), (wg_ref, wg16_ref)):
        dst[...] = src[...].astype(BF16)

    cs = CONV_PIECE
    nz, nxs, nbc, nu = z_ref.shape[1], xs_ref.shape[1], bc_ref.shape[1], u_ref.shape[1]
    o_dt = nz + nxs + nbc
    n_dt = 2 * SSD_HEADS
    w_end = o_dt + n_dt + nu
    conv_outs = [(xs_ref, c, nz + c, c) for c in range(0, nxs, cs)]
    conv_outs += [(bc_ref, c, nz + nxs + c, nxs + c) for c in range(0, nbc, cs)]

    def project(c0, c1):
        return jnp.dot(hn_ref[H:H + tm, :], w_ref[:, c0:c1], preferred_element_type=F32)

    def z_chunk(c):
        z_ref[:, c:c + cs] = project(c, c + cs)

    split = o_dt + 2 * cs
    tail = [project(o_dt, split)]
    _scan_prep(tail[0][:, 0:V7X_LANES], dtb_ref, alog_ref, t_ref, pk_ref, sk_ref)
    plain = [functools.partial(z_chunk, c) for c in range(0, nz, cs)]
    plain += [lambda: tail.append(project(split, w_end))]

    half = SSD_CONV // 2
    for k, (o_ref, c0, wc, cc) in enumerate(conv_outs):
        ext_ref = ext_refs[k % len(ext_refs)]
        ext_ref[...] = jnp.dot(hn_ref[...], w_ref[:, wc:wc + cs], preferred_element_type=F32)
        if k < len(plain):
            plain[k]()
        acc = cw_ref[0:1, cc:cc + cs] * ext_ref[H - half:H - half + tm, :]
        for j in range(1, SSD_CONV):
            acc = acc + cw_ref[j:j + 1, cc:cc + cs] * ext_ref[H - half + j:H - half + j + tm, :]
        acc = acc + cb_ref[:, cc:cc + cs]
        o_ref[:, c0:c0 + cs] = (acc * jax.nn.sigmoid(acc)).astype(o_ref.dtype)
    for fn in plain[len(conv_outs):]:
        fn()
    u_ref[...] = jnp.concatenate([tail[0][:, n_dt:], tail[1]], axis=1)


def _inproj(x, g, w, cw, cb, dtb, alog, tcat, out_cols, w_down, w_out, w_gate, *, tm):
    L, D = x.shape
    H = BF16_ROWS
    Q = SSD_CHUNK
    n = L // tm
    dts = (BF16, BF16, F32, F32)
    row = lambda c: pl.BlockSpec((tm, c), lambda i: (i, 0))
    slab = lambda a: pl.BlockSpec((a.shape[0] // n, a.shape[1]), lambda i: (i, 0))
    casts = (w_down, w_out, w_gate)
    consts = (g, w, cw, cb, dtb, alog, tcat)
    prep_shapes = [((L // Q, 2, 3 * Q, V7X_LANES), BF16), ((L // Q, 2 * SSD_HEADS, Q), F32)]
    prep_specs = [pl.BlockSpec((tm // Q,) + s[1:], lambda i, nd=len(s): (i,) + (0,) * (nd - 1)) for s, _ in prep_shapes]
    est = (2 * (tm + 2 * H) * D * 4 + w.size * 2 + 2 * tm * sum(out_cols) * 4
           + (tm + 2 * H) * D * 2 + 4 * (tm + 2 * H) * CONV_PIECE * 4 + 2 * sum(a.size for a in casts) // n * 6
           + 24 * tm * CONV_PIECE * 4)
    return pl.pallas_call(
        _inproj_kernel,
        grid=(n,),
        in_specs=_halo_specs(tm, H, L, D) + [_const_spec(a.shape) for a in consts] + [slab(a) for a in casts],
        out_specs=[row(c) for c in out_cols] + prep_specs + [slab(a) for a in casts],
        out_shape=[jax.ShapeDtypeStruct((L, c), dt) for c, dt in zip(out_cols, dts)]
                  + [jax.ShapeDtypeStruct(s, dt) for s, dt in prep_shapes]
                  + [jax.ShapeDtypeStruct(a.shape, BF16) for a in casts],
        scratch_shapes=[pltpu.VMEM((tm + 2 * H, D), BF16)]
                       + [pltpu.VMEM((tm + 2 * H, CONV_PIECE), F32) for _ in range(2)],
        compiler_params=pltpu.CompilerParams(dimension_semantics=("arbitrary",), vmem_limit_bytes=_vmem_limit(est)),
        name="inproj",
    )(x, x, x, *consts, *casts)


SSD_PACK = 16


LOG2E = 1.4426950408889634


def _pack3(v, head_lanes):
    hi, mid, lo = _split3(jnp.where(head_lanes, v, jnp.zeros_like(v)))
    packed = hi + pltpu.roll(mid, SSD_PACK, axis=1) + pltpu.roll(lo, 2 * SSD_PACK, axis=1)
    return packed.astype(BF16)


def _scan_both(xs_refs, bc_refs, pk_refs, sk_refs, nm_ref, rexp_ref, ecol_ref, h_ref):
    Q = SSD_CHUNK
    P = SSD_HEAD_DIM
    E = SSD_HEADS_PER_GROUP
    GW = E * P
    DIRS = (0, 1)
    last = (Q - 1, 0)
    xs16 = [xs_refs[d][...] for d in DIRS]
    bc = [bc_refs[d][...] for d in DIRS]
    src_t = [sk_refs[d][...] for d in DIRS]

    exp2x = [jnp.dot(pk_refs[d][0:2 * Q, :], rexp_ref[d], preferred_element_type=F32) for d in DIRS]
    acol = [jnp.dot(pk_refs[d][2 * Q:3 * Q, :], ecol_ref[d], preferred_element_type=F32) for d in DIRS]
    xw = [xs16[d] * exp2x[d][0:Q].astype(BF16) for d in DIRS]
    ea_x = [exp2x[d][Q:2 * Q] for d in DIRS]
    negmask = [nm_ref[d] for d in DIRS]
    first_head = lax.broadcasted_iota(jnp.int32, (Q, 2 * P), 1) < P

    y_groups = [[], []]
    for g in range(SSD_GROUPS):
        c0 = g * GW
        bm = [bc[d][:, g * SSD_STATE:(g + 1) * SSD_STATE] for d in DIRS]
        cm = [bc[d][:, (SSD_GROUPS + g) * SSD_STATE:(SSD_GROUPS + g + 1) * SSD_STATE] for d in DIRS]
        cb = [lax.dot_general(cm[d], bm[d], (((1,), (1,)), ((), ())), preferred_element_type=F32)
              for d in DIRS]
        st = [lax.dot_general(bm[d], xw[d][:, c0:c0 + GW], (((0,), (0,)), ((), ())), preferred_element_type=F32)
              for d in DIRS]
        h_in = [h_ref[d, g] for d in DIRS]
        y_off = [jnp.dot(cm[d], h_in[d].astype(BF16), preferred_element_type=F32) for d in DIRS]
        y_g = [y_off[d] * ea_x[d][:, c0:c0 + GW] for d in DIRS]
        for d in DIRS:
            h_ref[d, g] = h_in[d] * ea_x[d][last[d]:last[d] + 1, c0:c0 + GW] + st[d]
        pairs = [[], []]
        for hp in range(E // 2):
            for d in DIRS:
                h0 = g * E + hp * 2
                ms = []
                for k in range(2):
                    r = d * SSD_HEADS + h0 + k
                    seg = acol[d][:, (h0 + k) * Q:(h0 + k + 1) * Q] - src_t[d][r:r + 1, :] + negmask[d]
                    ms.append((cb[d] * jnp.exp2(seg)).astype(BF16))
                m2 = jnp.concatenate(ms, axis=1)
                xp = xs16[d][:, (g * E + hp * 2) * P:(g * E + hp * 2 + 2) * P]
                zero = jnp.zeros_like(xp)
                rhs = jnp.concatenate([jnp.where(first_head, xp, zero), jnp.where(first_head, zero, xp)], axis=0)
                pairs[d].append(jnp.dot(m2, rhs, preferred_element_type=F32))
        for d in DIRS:
            y_groups[d].append(y_g[d] + jnp.concatenate(pairs[d], axis=1))
    return [(jnp.concatenate(y_groups[d], axis=1), xs16[d]) for d in DIRS]


def _ssd_kernel(xsf_ref, bcf_ref, pkf_ref, skf_ref, zf_ref, xsb_ref, bcb_ref, pkb_ref, skb_ref, zb_ref,
                dexp_ref, nw_ref, nm_ref, rexp_ref, ecol_ref, wup_ref,
                lo_ref, hi_ref, wup16_ref,
                y_ref, h_ref):
    Q = SSD_CHUNK
    GW = SSD_HEADS_PER_GROUP * SSD_HEAD_DIM
    i = pl.program_id(0)
    nc = pl.num_programs(0)

    wup16_ref[...] = wup_ref[...].astype(BF16)

    @pl.when(i == 0)
    def _():
        h_ref[...] = jnp.zeros_like(h_ref)

    (y_f, xs_f), (y_b, xs_b) = _scan_both((xsf_ref, xsb_ref), (bcf_ref, bcb_ref), (pkf_ref, pkb_ref),
                                          (skf_ref, skb_ref), nm_ref, rexp_ref, ecol_ref, h_ref)
    row_f = pl.multiple_of(i * Q, Q)
    row_b = pl.multiple_of((nc - 1 - i) * Q, Q)

    @pl.when(i < nc // 2)
    def _():
        y_ref[pl.ds(row_f, Q), :] = y_f
        y_ref[pl.ds(row_b, Q), :] = y_b

    def finish(y, xs, z_ref, o_ref):
        yy = y + xs.astype(F32) * dexp_ref[...]
        z = z_ref[...]
        yy = yy * (z * jax.nn.sigmoid(z))
        outs = [_rms(yy[:, g * GW:(g + 1) * GW]) for g in range(SSD_GROUPS)]
        o_ref[...] = (jnp.concatenate(outs, axis=1) * nw_ref[...]).astype(o_ref.dtype)

    @pl.when(i >= nc // 2)
    def _():
        finish(y_f + y_ref[pl.ds(row_f, Q), :], xs_f, zf_ref, hi_ref)
        finish(y_b + y_ref[pl.ds(row_b, Q), :], xs_b, zb_ref, lo_ref)


def _ssd(xs, bc, pk, sk, z, dexp, nw, negmask, rexp, ecol, w_up):
    L, DS = xs.shape
    Q = SSD_CHUNK
    nc = L // Q
    hc = nc // 2
    slab = pl.BlockSpec((w_up.shape[0] // nc, w_up.shape[1]), lambda i: (i, 0))
    fwd = lambda i: i
    bwd = lambda i: nc - 1 - i
    fwd_late = lambda i: jnp.maximum(i, hc)
    bwd_late = lambda i: jnp.minimum(nc - 1 - i, hc - 1)
    in_specs = [
        pl.BlockSpec((Q, DS), lambda i: (fwd(i), 0)),
        pl.BlockSpec((Q, bc.shape[1]), lambda i: (fwd(i), 0)),
        pl.BlockSpec((None, None) + pk.shape[2:], lambda i: (fwd(i), 0, 0, 0)),
        pl.BlockSpec((None,) + sk.shape[1:], lambda i: (fwd(i), 0, 0)),
        pl.BlockSpec((Q, DS), lambda i: (fwd_late(i), 0)),
        pl.BlockSpec((Q, DS), lambda i: (bwd(i), 0)),
        pl.BlockSpec((Q, bc.shape[1]), lambda i: (bwd(i), 0)),
        pl.BlockSpec((None, None) + pk.shape[2:], lambda i: (bwd(i), 1, 0, 0)),
        pl.BlockSpec((None,) + sk.shape[1:], lambda i: (bwd(i), 0, 0)),
        pl.BlockSpec((Q, DS), lambda i: (bwd_late(i), 0)),
    ] + [_const_spec(a.shape) for a in (dexp, nw, negmask, rexp, ecol)] + [slab]
    est = (L * DS * 4 + 2 * SSD_GROUPS * SSD_STATE * DS * 4 + 8 * Q * (DS + bc.shape[1]) * 4
           + (rexp.size + ecol.size) * 2 + 4 * Q * DS * 4 * 2 + 40 * Q * DS * 4 + 12 * w_up.size // nc)
    return pl.pallas_call(
        _ssd_kernel,
        grid=(nc,),
        in_specs=in_specs,
        out_specs=[pl.BlockSpec((Q, DS), lambda i: (bwd_late(i), 0)),
                   pl.BlockSpec((Q, DS), lambda i: (fwd_late(i) - hc, 0)), slab],
        out_shape=[jax.ShapeDtypeStruct((L // 2, DS), BF16), jax.ShapeDtypeStruct((L // 2, DS), BF16),
                   jax.ShapeDtypeStruct(w_up.shape, BF16)],
        scratch_shapes=[
            pltpu.VMEM((L, DS), F32),
            pltpu.VMEM((2, SSD_GROUPS, SSD_STATE, DS // SSD_GROUPS), F32),
        ],
        compiler_params=pltpu.CompilerParams(dimension_semantics=("arbitrary",), vmem_limit_bytes=_vmem_limit(est)),
        name="ssd",
    )(xs, bc, pk, sk, z, xs, bc, pk, sk, z, dexp, nw, negmask, rexp, ecol, w_up)


def _mixout_kernel(up_ref, um_ref, un_ref, ylo_ref, yhi_ref, x_ref, pw_ref, ps_ref, wo_ref, g_ref, gn_ref,
                   o_ref, on_ref, ext_ref, *, seq):
    i = pl.program_id(0)
    n = pl.num_programs(0)
    tm = um_ref.shape[0]
    cg = um_ref.shape[1] // len(POOL_WINDOWS)
    ext_ref[0:8, :] = jnp.where(i > 0, up_ref[...], 0.0)
    ext_ref[8:8 + tm, :] = um_ref[...]
    ext_ref[8 + tm:16 + tm, :] = jnp.where(i < n - 1, un_ref[...], 0.0)
    ys = jnp.where(i < n // 2, ylo_ref[...], yhi_ref[...])
    ds = ys.shape[1]
    mix = jnp.dot(ys, wo_ref[0:ds, :], preferred_element_type=F32)
    t = i * tm + lax.broadcasted_iota(jnp.int32, (tm, cg), 0)
    rows = tm + 16

    def ahead(v, k):
        return pltpu.roll(v, (rows - k) % rows, axis=0)

    pooled = []
    for gi, k in enumerate(POOL_WINDOWS):
        cols = slice(gi * cg, (gi + 1) * cg)
        e = ext_ref[:, cols]
        half = k // 2
        run, length = e, 1
        while length < half:
            run = run + ahead(run, length)
            length *= 2
        before = run[0:tm] if half == 8 else ahead(run, rows - half)[8:8 + tm]
        acc = before + run[8:8 + tm]
        cnt = (jnp.minimum(t + (k - k // 2), seq) - jnp.maximum(t - k // 2, 0)).astype(F32)
        mixed = acc / cnt - um_ref[:, cols]
        yp = jnp.dot(mixed.astype(BF16), pw_ref[gi], preferred_element_type=F32) * ps_ref[:, cols]
        pooled.append(yp.astype(BF16))
    ypool = jnp.concatenate(pooled, axis=1)
    mix = mix + jnp.dot(ypool, wo_ref[ds:, :], preferred_element_type=F32)
    h = x_ref[...] + _rms(mix) * g_ref[...]
    o_ref[...] = h
    on_ref[...] = (_rms(h) * gn_ref[...]).astype(on_ref.dtype)


def _mixout(u, y_lo, y_hi, x, pool_w, pool_scale, w_out, g, g_next, *, tm):
    L, DP = u.shape
    D = x.shape[1]
    DS = y_lo.shape[1]
    n = L // tm
    row = lambda n_: pl.BlockSpec((tm, n_), lambda i: (i, 0))
    in_specs = (_halo_specs(tm, V7X_SUBLANES, L, DP)
                + [pl.BlockSpec((tm, DS), lambda i: (jnp.minimum(i, n // 2 - 1), 0)),
                   pl.BlockSpec((tm, DS), lambda i: (jnp.maximum(i - n // 2, 0), 0)), row(D)]
                + [_const_spec(a.shape) for a in (pool_w, pool_scale, w_out, g, g_next)])
    est = (2 * tm * (DP * 4 + 2 * DS * 2 + D * 4 + D * 4 + D * 2) + (w_out.size + pool_w.size) * 2
           + (tm + 16) * DP * 4 + 8 * tm * D * 4)
    return pl.pallas_call(
        functools.partial(_mixout_kernel, seq=L),
        grid=(n,),
        in_specs=in_specs,
        out_specs=[row(D), row(D)],
        out_shape=[jax.ShapeDtypeStruct((L, D), F32), jax.ShapeDtypeStruct((L, D), BF16)],
        scratch_shapes=[pltpu.VMEM((tm + 16, DP), F32)],
        compiler_params=pltpu.CompilerParams(dimension_semantics=("arbitrary",), vmem_limit_bytes=_vmem_limit(est)),
        name="mixout",
    )(u, u, u, y_lo, y_hi, x, pool_w, pool_scale, w_out, g, g_next)


def _ffn_kernel(hp_ref, hm_ref, hx_ref, wg_ref, wv_ref, cw_ref, cb_ref, wd_ref, gpost_ref,
                o_ref, hn_ref, gate_ref):
    i = pl.program_id(0)
    f = pl.program_id(1)
    n = pl.num_programs(0)
    nf = pl.num_programs(1)
    tm = hm_ref.shape[0]
    H = BF16_ROWS

    @pl.when(f == 0)
    def _():
        hn_ref[0:H, :] = jnp.where(i > 0, hp_ref[...], jnp.zeros_like(hp_ref))
        hn_ref[H:H + tm, :] = hm_ref[...]
        hn_ref[H + tm:2 * H + tm, :] = jnp.where(i < n - 1, hx_ref[...], jnp.zeros_like(hx_ref))
        o_ref[...] = jnp.zeros_like(o_ref)

    gate_ref[...] = jnp.dot(hn_ref[...], wg_ref[...], preferred_element_type=F32)
    val = jnp.dot(hm_ref[...], wv_ref[...], preferred_element_type=F32)
    half = FFN_CONV // 2
    gc = cw_ref[0:1, :] * gate_ref[H - half:H - half + tm, :]
    for j in range(1, FFN_CONV):
        gc = gc + cw_ref[j:j + 1, :] * gate_ref[H - half + j:H - half + j + tm, :]
    gc = gc + cb_ref[...]
    act = (jax.nn.gelu(gc, approximate=True) * val).astype(BF16)
    o_ref[...] += jnp.dot(act, wd_ref[...], preferred_element_type=F32)

    @pl.when(f == nf - 1)
    def _():
        o_ref[...] = _rms(o_ref[...]) * gpost_ref[...]


def _ffn(hn, w_up, cw, cb, w_down, gpost, *, tm, tf):
    L, D = hn.shape
    DF = w_down.shape[0]
    nf = DF // tf
    H = BF16_ROWS
    in_specs = _halo_specs(tm, H, L, D) + [
        pl.BlockSpec((D, tf), lambda i, f: (0, f)),
        pl.BlockSpec((D, tf), lambda i, f: (0, nf + f)),
        pl.BlockSpec((FFN_CONV, tf), lambda i, f: (0, f)),
        pl.BlockSpec((1, tf), lambda i, f: (0, f)),
        pl.BlockSpec((tf, D), lambda i, f: (f, 0)),
        pl.BlockSpec(gpost.shape, lambda i, f: (0, 0)),
    ]
    est = (2 * (tm + 2 * H) * D * 2 + 2 * tm * D * 4 + (tm + 2 * H) * D * 2 + (tm + 2 * H) * tf * 4
           + 2 * 3 * D * tf * 2 + 8 * tm * tf * 4 + tm * D * 4)
    return pl.pallas_call(
        _ffn_kernel,
        grid=(L // tm, nf),
        in_specs=in_specs,
        out_specs=pl.BlockSpec((tm, D), lambda i, f: (i, 0)),
        out_shape=jax.ShapeDtypeStruct((L, D), F32),
        scratch_shapes=[
            pltpu.VMEM((tm + 2 * H, D), BF16),
            pltpu.VMEM((tm + 2 * H, tf), F32),
        ],
        compiler_params=pltpu.CompilerParams(dimension_semantics=("arbitrary", "arbitrary"),
                                             vmem_limit_bytes=_vmem_limit(est)),
        name="ffn",
    )(hn, hn, hn, w_up, w_up, cw, cb, w_down, gpost)


def _ple_kernel(h_ref, r_ref, p_ref, gpre_ref, wg_ref, wp_ref, gpost_ref, o_ref):
    h = h_ref[...] + r_ref[...]
    hn = (_rms(h) * gpre_ref[...]).astype(BF16)
    gate = jax.nn.sigmoid(jnp.dot(hn, wg_ref[...], preferred_element_type=F32))
    pe = jnp.dot(p_ref[...].astype(BF16), wp_ref[...], preferred_element_type=F32)
    o_ref[...] = h + _rms(gate * pe) * gpost_ref[...]


def _ple(h, r, p, gpre, w_gate, w_ple, gpost, *, tm):
    L, D = h.shape
    row = lambda n: pl.BlockSpec((tm, n), lambda i: (i, 0))
    est = 2 * tm * (3 * D + p.shape[1]) * 4 + (w_gate.size + w_ple.size) * 2 + 8 * tm * D * 4
    return pl.pallas_call(
        _ple_kernel,
        grid=(L // tm,),
        in_specs=[row(D), row(D), row(p.shape[1])] + [_const_spec(a.shape) for a in (gpre, w_gate, w_ple, gpost)],
        out_specs=row(D),
        out_shape=jax.ShapeDtypeStruct((L, D), F32),
        compiler_params=pltpu.CompilerParams(dimension_semantics=("arbitrary",), vmem_limit_bytes=_vmem_limit(est)),
        name="ple",
    )(h, r, p, gpre, w_gate, w_ple, gpost)


def _ssd_constants():
    Q = SSD_CHUNK
    r = jnp.arange(Q)
    lower = (r[:, None] >= r[None, :])
    tri = jnp.stack([lower, lower.T]).astype(BF16)
    tcat = jnp.concatenate([tri, tri, tri], axis=2)
    negmask = jnp.where(jnp.stack([lower, lower.T]), 0.0, NEG_BIG).astype(F32)
    k = jnp.arange(V7X_LANES)
    rexp, ecol = [], []
    for d in range(2):
        packed = jnp.logical_and(k >= d * SSD_HEADS, k < d * SSD_HEADS + 3 * SSD_PACK)
        head_of_lane = jnp.where(packed, (k - d * SSD_HEADS) % SSD_PACK, -1)
        rexp.append(head_of_lane[:, None] == (jnp.arange(SSD_HEADS * SSD_HEAD_DIM) // SSD_HEAD_DIM)[None, :])
        ecol.append(head_of_lane[:, None] == (jnp.arange(SSD_HEADS * Q) // Q)[None, :])
    return tcat, negmask, jnp.stack(rexp).astype(BF16), jnp.stack(ecol).astype(BF16)


def _dir_lanes(v, fill):
    out = jnp.full((1, V7X_LANES), fill, F32)
    return out.at[0, :v.size].set(v.astype(F32).reshape(-1))


def kernel(x, p, mix_norm_pre, mix_norm_post, w_in, ssd_conv_w, ssd_conv_b, ssd_dt_bias, ssd_a_log, ssd_d,
           ssd_norm, pool_w, pool_scale, w_out, ffn_norm_pre, ffn_norm_post, w_ffn_up, ffn_conv_w, ffn_conv_b,
           w_ffn_down, ple_norm_pre, w_ple_gate, w_ple, ple_norm_post):
    B, L, D = x.shape
    depth = w_in.shape[0]
    d_ssd = SSD_HEADS * SSD_HEAD_DIM
    n_bc = 2 * SSD_GROUPS * SSD_STATE
    o_dt = 2 * d_ssd + n_bc
    o_u = o_dt + 2 * SSD_HEADS
    d_pool = w_in.shape[2] - o_u
    tcat, negmask, rexp, ecol = _ssd_constants()
    row = lambda v: v.reshape(1, -1).astype(F32)

    outs = []
    for b in range(B):
        h = x[b]
        for i in range(depth):
            xs, bc, z, u, pk, sk, w_down16, w_out16, w_gate16 = _inproj(
                h, row(mix_norm_pre[i]), _cast_transposed(w_in[i].T, cols=CONV_PIECE), ssd_conv_w[i].astype(F32),
                row(ssd_conv_b[i]), _dir_lanes(ssd_dt_bias[i], 0.0), _dir_lanes(ssd_a_log[i], NEG_BIG), tcat,
                (d_ssd, n_bc, d_ssd, d_pool), w_ffn_down[i], w_out[i], w_ple_gate[i], tm=INPROJ_ROWS)
            y_lo, y_hi, w_up16 = _ssd(
                xs, bc, pk, sk, z, jnp.repeat(ssd_d[i].astype(F32), SSD_HEAD_DIM).reshape(1, -1), row(ssd_norm[i]),
                negmask, rexp, ecol, w_ffn_up[i])
            h, hn = _mixout(u, y_lo, y_hi, h, pool_w[i].astype(BF16), row(pool_scale[i]), w_out16,
                            row(mix_norm_post[i]), row(ffn_norm_pre[i]), tm=MIXOUT_ROWS)
            r = _ffn(hn, w_up16, ffn_conv_w[i].astype(F32), row(ffn_conv_b[i]), w_down16, row(ffn_norm_post[i]),
                     tm=FFN_ROWS, tf=FFN_COLS)
            h = _ple(h, r, p[i, b], row(ple_norm_pre[i]), w_gate16, w_ple[i].astype(BF16),
                     row(ple_norm_post[i]), tm=PLE_ROWS)
        outs.append(h)
    return jnp.stack(outs)
```

```python
import functools

import jax
import jax.numpy as jnp
from jax import lax
from jax.experimental import pallas as pl
from jax.experimental.pallas import tpu as pltpu

F32 = jnp.float32
BF16 = jnp.bfloat16
EPS = 1e-6

V7X_VMEM_BYTES = 64 * 1024 * 1024
V7X_LANES = 128
V7X_SUBLANES = 8
BF16_ROWS = 16

SSD_HEAD_DIM = 64
SSD_HEADS = 16
SSD_GROUPS = 2
SSD_HEADS_PER_GROUP = SSD_HEADS // SSD_GROUPS
SSD_STATE = 128
SSD_CONV = 5
SSD_CHUNK = 128
POOL_WINDOWS = (2, 4, 8, 16)
FFN_CONV = 3
NEG_BIG = -1e30

INPROJ_ROWS = 512
MIXOUT_ROWS = 512
FFN_ROWS = 1024
FFN_COLS = 512
PLE_ROWS = 512
CONV_PIECE = 256


def _vmem_limit(nbytes):
    return int(min(nbytes, V7X_VMEM_BYTES - 6 * 1024 * 1024))


def _rms(x):
    return x * lax.rsqrt(jnp.mean(x * x, axis=-1, keepdims=True) + EPS)


def _split3(x):
    hi = x.astype(BF16).astype(F32)
    r1 = x - hi
    mid = r1.astype(BF16).astype(F32)
    lo = (r1 - mid).astype(BF16).astype(F32)
    return hi, mid, lo


def _const_spec(shape):
    nd = len(shape)
    return pl.BlockSpec(shape, lambda *_: (0,) * nd, pipeline_mode=pl.Buffered(1))


def _halo_specs(rows, halo, total_rows, ncols):
    hb = rows // halo
    nhb = total_rows // halo
    return [
        pl.BlockSpec((halo, ncols), lambda i, *_: (jnp.maximum(i * hb - 1, 0), 0)),
        pl.BlockSpec((rows, ncols), lambda i, *_: (i, 0)),
        pl.BlockSpec((halo, ncols), lambda i, *_: (jnp.minimum(i * hb + hb, nhb - 1), 0)),
    ]


def _cast_rows_kernel(w_ref, o_ref, *, valid_rows):
    row = pl.program_id(0) * w_ref.shape[0] + lax.broadcasted_iota(jnp.int32, w_ref.shape, 0)
    o_ref[...] = jnp.where(row < valid_rows, w_ref[...], 0.0).astype(o_ref.dtype)


def _cast_rows(w_t, *, rows):
    C, R = w_t.shape
    n = pl.cdiv(C, rows)
    spec = pl.BlockSpec((rows, R), lambda j: (j, 0))
    return pl.pallas_call(
        functools.partial(_cast_rows_kernel, valid_rows=C),
        grid=(n,),
        in_specs=[spec],
        out_specs=spec,
        out_shape=jax.ShapeDtypeStruct((n * rows, R), BF16),
        compiler_params=pltpu.CompilerParams(dimension_semantics=("arbitrary",),
                                             vmem_limit_bytes=_vmem_limit(V7X_VMEM_BYTES)),
        name="cast_w_in",
    )(w_t)


def _scan_prep(raw, dtb_ref, alog_ref, t_ref, pk_ref, sk_ref, ac_ref):
    Q = SSD_CHUNK
    lane = lax.broadcasted_iota(jnp.int32, (Q, V7X_LANES), 1)
    head_lanes = [jnp.logical_and(lane >= d * SSD_HEADS, lane < (d + 1) * SSD_HEADS) for d in (0, 1)]
    a = -jnp.exp(alog_ref[...])
    for c in range(raw.shape[0] // Q):
        dt = jax.nn.softplus(raw[c * Q:(c + 1) * Q, :] + dtb_ref[...])
        da3 = jnp.concatenate([v.astype(BF16) for v in _split3(dt * a)], axis=0)
        acum = jnp.where(head_lanes[0], jnp.dot(t_ref[0], da3, preferred_element_type=F32),
                         jnp.dot(t_ref[1], da3, preferred_element_type=F32)) * LOG2E
        tot = jnp.where(head_lanes[0][0:1], acum[Q - 1:Q, :], acum[0:1, :])
        ac_ref[c * Q:(c + 1) * Q, :] = acum
        sk_ref[c] = (acum - jnp.log2(dt)).T[0:2 * SSD_HEADS, :]
        e_a = jnp.exp2(acum)
        dtw = dt * jnp.exp2(tot - acum)
        for d in (0, 1):
            pk_ref[c, d, 0:Q, :] = _pack3(dtw, head_lanes[d])
            pk_ref[c, d, Q:2 * Q, :] = _pack3(e_a, head_lanes[d])


def _inproj_kernel(xp_ref, xm_ref, xn_ref, g_ref, w_ref, cw_ref, cb_ref, dtb_ref, alog_ref, t_ref,
                   wd_ref, wo_ref, wg_ref,
                   xs_ref, bc_ref, z_ref, u_ref, pk_ref, sk_ref, ac_ref, wd16_ref, wo16_ref, wg16_ref,
                   hn_ref, *ext_refs):
    i = pl.program_id(0)
    n = pl.num_programs(0)
    tm = xm_ref.shape[0]
    H = BF16_ROWS
    g = g_ref[...]
    hn_ref[0:H, :] = jnp.where(i > 0, _rms(xp_ref[...]) * g, 0.0).astype(BF16)
    hn_ref[H:H + tm, :] = (_rms(xm_ref[...]) * g).astype(BF16)
    hn_ref[H + tm:2 * H + tm, :] = jnp.where(i < n - 1, _rms(xn_ref[...]) * g, 0.0).astype(BF16)
    for src, dst in ((wd_ref, wd16_ref), (wo_ref, wo16_ref), (wg_ref, wg16_ref)):
        dst[...] = src[...].astype(BF16)

    cs = CONV_PIECE
    nz, nxs, nbc, nu = z_ref.shape[1], xs_ref.shape[1], bc_ref.shape[1], u_ref.shape[1]
    o_dt = nz + nxs + nbc
    n_dt = 2 * SSD_HEADS
    conv_outs = [(xs_ref, c, nz + c, c) for c in range(0, nxs, cs)]
    conv_outs += [(bc_ref, c, nz + nxs + c, nxs + c) for c in range(0, nbc, cs)]

    nt = (((1,), (1,)), ((), ()))

    def project(c0, c1):
        return lax.dot_general(hn_ref[H:H + tm, :], w_ref[c0:c1, :], nt, preferred_element_type=F32)

    def chunk(o_ref, c, w0):
        o_ref[:, c:c + cs] = project(w0 + c, w0 + c + cs)

    _scan_prep(project(o_dt, o_dt + V7X_LANES), dtb_ref, alog_ref, t_ref, pk_ref, sk_ref, ac_ref)
    plain = [functools.partial(chunk, z_ref, c, 0) for c in range(0, nz, cs)]
    plain += [functools.partial(chunk, u_ref, c, o_dt + n_dt) for c in range(0, nu, cs)]

    half = SSD_CONV // 2
    for k, (o_ref, c0, wc, cc) in enumerate(conv_outs):
        ext_ref = ext_refs[k % len(ext_refs)]
        ext_ref[...] = lax.dot_general(hn_ref[...], w_ref[wc:wc + cs, :], nt, preferred_element_type=F32)
        if k < len(plain):
            plain[k]()
        acc = cw_ref[0:1, cc:cc + cs] * ext_ref[H - half:H - half + tm, :]
        for j in range(1, SSD_CONV):
            acc = acc + cw_ref[j:j + 1, cc:cc + cs] * ext_ref[H - half + j:H - half + j + tm, :]
        acc = acc + cb_ref[:, cc:cc + cs]
        o_ref[:, c0:c0 + cs] = (acc * jax.nn.sigmoid(acc)).astype(o_ref.dtype)
    for fn in plain[len(conv_outs):]:
        fn()


def _inproj(x, g, w, cw, cb, dtb, alog, tcat, out_cols, w_down, w_out, w_gate, *, tm):
    L, D = x.shape
    H = BF16_ROWS
    Q = SSD_CHUNK
    n = L // tm
    dts = (BF16, BF16, F32, F32)
    row = lambda c: pl.BlockSpec((tm, c), lambda i: (i, 0))
    slab = lambda a: pl.BlockSpec((a.shape[0] // n, a.shape[1]), lambda i: (i, 0))
    casts = (w_down, w_out, w_gate)
    consts = (g, w, cw, cb, dtb, alog, tcat)
    prep_shapes = [((L // Q, 2, 2 * Q, V7X_LANES), BF16), ((L // Q, 2 * SSD_HEADS, Q), F32), ((L, V7X_LANES), F32)]
    prep_specs = [pl.BlockSpec((tm // Q,) + s[1:], lambda i, nd=len(s): (i,) + (0,) * (nd - 1))
                  for s, _ in prep_shapes[:2]] + [row(V7X_LANES)]
    est = (2 * (tm + 2 * H) * D * 4 + w.size * 2 + 2 * tm * sum(out_cols) * 4
           + (tm + 2 * H) * D * 2 + 4 * (tm + 2 * H) * CONV_PIECE * 4 + 2 * sum(a.size for a in casts) // n * 6
           + 24 * tm * CONV_PIECE * 4)
    return pl.pallas_call(
        _inproj_kernel,
        grid=(n,),
        in_specs=_halo_specs(tm, H, L, D) + [_const_spec(a.shape) for a in consts] + [slab(a) for a in casts],
        out_specs=[row(c) for c in out_cols] + prep_specs + [slab(a) for a in casts],
        out_shape=[jax.ShapeDtypeStruct((L, c), dt) for c, dt in zip(out_cols, dts)]
                  + [jax.ShapeDtypeStruct(s, dt) for s, dt in prep_shapes]
                  + [jax.ShapeDtypeStruct(a.shape, BF16) for a in casts],
        scratch_shapes=[pltpu.VMEM((tm + 2 * H, D), BF16)]
                       + [pltpu.VMEM((tm + 2 * H, CONV_PIECE), F32) for _ in range(2)],
        compiler_params=pltpu.CompilerParams(dimension_semantics=("arbitrary",), vmem_limit_bytes=_vmem_limit(est)),
        name="inproj",
    )(x, x, x, *consts, *casts)


SSD_PACK = 16


LOG2E = 1.4426950408889634


def _pack3(v, head_lanes):
    hi, mid, lo = _split3(jnp.where(head_lanes, v, jnp.zeros_like(v)))
    packed = hi + pltpu.roll(mid, SSD_PACK, axis=1) + pltpu.roll(lo, 2 * SSD_PACK, axis=1)
    return packed.astype(BF16)


def _scan_both(xs_refs, bc_refs, pk_refs, sk_refs, ac_refs, nm_ref, rexp_ref, h_ref):
    Q = SSD_CHUNK
    P = SSD_HEAD_DIM
    E = SSD_HEADS_PER_GROUP
    GW = E * P
    DIRS = (0, 1)
    last = (Q - 1, 0)
    xs16 = [xs_refs[d][...] for d in DIRS]
    bc = [bc_refs[d][...] for d in DIRS]
    src_t = [sk_refs[d][...] for d in DIRS]

    exp2x = [jnp.dot(pk_refs[d][0:2 * Q, :], rexp_ref[d], preferred_element_type=F32) for d in DIRS]
    acum = [ac_refs[d][...] for d in DIRS]
    xw = [xs16[d] * exp2x[d][0:Q].astype(BF16) for d in DIRS]
    ea_x = [exp2x[d][Q:2 * Q] for d in DIRS]
    negmask = [nm_ref[d] for d in DIRS]
    first_head = lax.broadcasted_iota(jnp.int32, (Q, 2 * P), 1) < P

    y_groups = [[], []]
    for g in range(SSD_GROUPS):
        c0 = g * GW
        bm = [bc[d][:, g * SSD_STATE:(g + 1) * SSD_STATE] for d in DIRS]
        cm = [bc[d][:, (SSD_GROUPS + g) * SSD_STATE:(SSD_GROUPS + g + 1) * SSD_STATE] for d in DIRS]
        cb = [lax.dot_general(cm[d], bm[d], (((1,), (1,)), ((), ())), preferred_element_type=F32)
              for d in DIRS]
        st = [lax.dot_general(bm[d], xw[d][:, c0:c0 + GW], (((0,), (0,)), ((), ())), preferred_element_type=F32)
              for d in DIRS]
        h_in = [h_ref[d, g] for d in DIRS]
        y_off = [jnp.dot(cm[d], h_in[d].astype(BF16), preferred_element_type=F32) for d in DIRS]
        y_g = [y_off[d] * ea_x[d][:, c0:c0 + GW] for d in DIRS]
        for d in DIRS:
            h_ref[d, g] = h_in[d] * ea_x[d][last[d]:last[d] + 1, c0:c0 + GW] + st[d]
        pairs = [[], []]
        for hp in range(E // 2):
            for d in DIRS:
                h0 = g * E + hp * 2
                ms = []
                for k in range(2):
                    r = d * SSD_HEADS + h0 + k
                    seg = jnp.broadcast_to(acum[d][:, r:r + 1], (Q, Q)) - src_t[d][r:r + 1, :] + negmask[d]
                    ms.append((cb[d] * jnp.exp2(seg)).astype(BF16))
                m2 = jnp.concatenate(ms, axis=1)
                xp = xs16[d][:, (g * E + hp * 2) * P:(g * E + hp * 2 + 2) * P]
                zero = jnp.zeros_like(xp)
                rhs = jnp.concatenate([jnp.where(first_head, xp, zero), jnp.where(first_head, zero, xp)], axis=0)
                pairs[d].append(jnp.dot(m2, rhs, preferred_element_type=F32))
        for d in DIRS:
            y_groups[d].append(y_g[d] + jnp.concatenate(pairs[d], axis=1))
    return [(jnp.concatenate(y_groups[d], axis=1), xs16[d]) for d in DIRS]


def _ssd_kernel(xsf_ref, bcf_ref, pkf_ref, skf_ref, acf_ref, zf_ref, xsb_ref, bcb_ref, pkb_ref, skb_ref, acb_ref, zb_ref,
                dexp_ref, nw_ref, nm_ref, rexp_ref, wup_ref,
                lo_ref, hi_ref, wup16_ref,
                y_ref, h_ref):
    Q = SSD_CHUNK
    GW = SSD_HEADS_PER_GROUP * SSD_HEAD_DIM
    i = pl.program_id(0)
    nc = pl.num_programs(0)

    wup16_ref[...] = wup_ref[...].astype(BF16)

    @pl.when(i == 0)
    def _():
        h_ref[...] = jnp.zeros_like(h_ref)

    (y_f, xs_f), (y_b, xs_b) = _scan_both((xsf_ref, xsb_ref), (bcf_ref, bcb_ref), (pkf_ref, pkb_ref),
                                          (skf_ref, skb_ref), (acf_ref, acb_ref), nm_ref, rexp_ref, h_ref)
    row_f = pl.multiple_of(i * Q, Q)
    row_b = pl.multiple_of((nc - 1 - i) * Q, Q)

    @pl.when(i < nc // 2)
    def _():
        y_ref[pl.ds(row_f, Q), :] = y_f
        y_ref[pl.ds(row_b, Q), :] = y_b

    def finish(y, xs, z_ref, o_ref):
        yy = y + xs.astype(F32) * dexp_ref[...]
        z = z_ref[...]
        yy = yy * (z * jax.nn.sigmoid(z))
        outs = [_rms(yy[:, g * GW:(g + 1) * GW]) for g in range(SSD_GROUPS)]
        o_ref[...] = (jnp.concatenate(outs, axis=1) * nw_ref[...]).astype(o_ref.dtype)

    @pl.when(i >= nc // 2)
    def _():
        finish(y_f + y_ref[pl.ds(row_f, Q), :], xs_f, zf_ref, hi_ref)
        finish(y_b + y_ref[pl.ds(row_b, Q), :], xs_b, zb_ref, lo_ref)


def _ssd(xs, bc, pk, sk, ac, z, dexp, nw, negmask, rexp, w_up):
    L, DS = xs.shape
    Q = SSD_CHUNK
    nc = L // Q
    hc = nc // 2
    slab = pl.BlockSpec((w_up.shape[0] // nc, w_up.shape[1]), lambda i: (i, 0))
    fwd = lambda i: i
    bwd = lambda i: nc - 1 - i
    fwd_late = lambda i: jnp.maximum(i, hc)
    bwd_late = lambda i: jnp.minimum(nc - 1 - i, hc - 1)
    in_specs = [
        pl.BlockSpec((Q, DS), lambda i: (fwd(i), 0)),
        pl.BlockSpec((Q, bc.shape[1]), lambda i: (fwd(i), 0)),
        pl.BlockSpec((None, None) + pk.shape[2:], lambda i: (fwd(i), 0, 0, 0)),
        pl.BlockSpec((None,) + sk.shape[1:], lambda i: (fwd(i), 0, 0)),
        pl.BlockSpec((Q, V7X_LANES), lambda i: (fwd(i), 0)),
        pl.BlockSpec((Q, DS), lambda i: (fwd_late(i), 0)),
        pl.BlockSpec((Q, DS), lambda i: (bwd(i), 0)),
        pl.BlockSpec((Q, bc.shape[1]), lambda i: (bwd(i), 0)),
        pl.BlockSpec((None, None) + pk.shape[2:], lambda i: (bwd(i), 1, 0, 0)),
        pl.BlockSpec((None,) + sk.shape[1:], lambda i: (bwd(i), 0, 0)),
        pl.BlockSpec((Q, V7X_LANES), lambda i: (bwd(i), 0)),
        pl.BlockSpec((Q, DS), lambda i: (bwd_late(i), 0)),
    ] + [_const_spec(a.shape) for a in (dexp, nw, negmask, rexp)] + [slab]
    est = (L * DS * 4 + 2 * SSD_GROUPS * SSD_STATE * DS * 4 + 8 * Q * (DS + bc.shape[1]) * 4
           + rexp.size * 2 + 4 * Q * DS * 4 * 2 + 40 * Q * DS * 4 + 12 * w_up.size // nc)
    return pl.pallas_call(
        _ssd_kernel,
        grid=(nc,),
        in_specs=in_specs,
        out_specs=[pl.BlockSpec((Q, DS), lambda i: (bwd_late(i), 0)),
                   pl.BlockSpec((Q, DS), lambda i: (fwd_late(i) - hc, 0)), slab],
        out_shape=[jax.ShapeDtypeStruct((L // 2, DS), BF16), jax.ShapeDtypeStruct((L // 2, DS), BF16),
                   jax.ShapeDtypeStruct(w_up.shape, BF16)],
        scratch_shapes=[
            pltpu.VMEM((L, DS), F32),
            pltpu.VMEM((2, SSD_GROUPS, SSD_STATE, DS // SSD_GROUPS), F32),
        ],
        compiler_params=pltpu.CompilerParams(dimension_semantics=("arbitrary",), vmem_limit_bytes=_vmem_limit(est)),
        name="ssd",
    )(xs, bc, pk, sk, ac, z, xs, bc, pk, sk, ac, z, dexp, nw, negmask, rexp, w_up)


def _mixout_kernel(up_ref, um_ref, un_ref, ylo_ref, yhi_ref, x_ref, pw_ref, ps_ref, wo_ref, g_ref, gn_ref,
                   o_ref, on_ref, ext_ref, *, seq):
    i = pl.program_id(0)
    n = pl.num_programs(0)
    tm = um_ref.shape[0]
    cg = um_ref.shape[1] // len(POOL_WINDOWS)
    H = V7X_SUBLANES
    ext_ref[0:H, :] = jnp.where(i > 0, up_ref[...], 0.0)
    ext_ref[H:H + tm, :] = um_ref[...]
    ext_ref[H + tm:2 * H + tm, :] = jnp.where(i < n - 1, un_ref[...], 0.0)
    ys = jnp.where(i < n // 2, ylo_ref[...], yhi_ref[...])
    ds = ys.shape[1]
    mix = jnp.dot(ys, wo_ref[0:ds, :], preferred_element_type=F32)
    t = i * tm + lax.broadcasted_iota(jnp.int32, (tm, cg), 0)
    rows = tm + 2 * H

    def ahead(v, k):
        return pltpu.roll(v, (rows - k) % rows, axis=0)

    pooled = []
    for gi, k in enumerate(POOL_WINDOWS):
        cols = slice(gi * cg, (gi + 1) * cg)
        e = ext_ref[:, cols]
        half = k // 2
        run, length = e, 1
        while length < half:
            run = run + ahead(run, length)
            length *= 2
        before = run[0:tm] if half == H else ahead(run, rows - half)[H:H + tm]
        acc = before + run[H:H + tm]
        cnt = (jnp.minimum(t + (k - k // 2), seq) - jnp.maximum(t - k // 2, 0)).astype(F32)
        mixed = acc / cnt - um_ref[:, cols]
        yp = jnp.dot(mixed.astype(BF16), pw_ref[gi], preferred_element_type=F32) * ps_ref[:, cols]
        pooled.append(yp.astype(BF16))
    ypool = jnp.concatenate(pooled, axis=1)
    mix = mix + jnp.dot(ypool, wo_ref[ds:, :], preferred_element_type=F32)
    h = x_ref[...] + _rms(mix) * g_ref[...]
    o_ref[...] = h
    on_ref[...] = (_rms(h) * gn_ref[...]).astype(on_ref.dtype)


def _mixout(u, y_lo, y_hi, x, pool_w, pool_scale, w_out, g, g_next, *, tm):
    assert all(k % 2 == 0 and k // 2 <= V7X_SUBLANES and (k // 2) & (k // 2 - 1) == 0 for k in POOL_WINDOWS)
    L, DP = u.shape
    D = x.shape[1]
    DS = y_lo.shape[1]
    n = L // tm
    row = lambda n_: pl.BlockSpec((tm, n_), lambda i: (i, 0))
    in_specs = (_halo_specs(tm, V7X_SUBLANES, L, DP)
                + [pl.BlockSpec((tm, DS), lambda i: (jnp.minimum(i, n // 2 - 1), 0)),
                   pl.BlockSpec((tm, DS), lambda i: (jnp.maximum(i - n // 2, 0), 0)), row(D)]
                + [_const_spec(a.shape) for a in (pool_w, pool_scale, w_out, g, g_next)])
    est = (2 * tm * (DP * 4 + 2 * DS * 2 + D * 4 + D * 4 + D * 2) + (w_out.size + pool_w.size) * 2
           + (tm + 2 * V7X_SUBLANES) * DP * 4 + 8 * tm * D * 4)
    return pl.pallas_call(
        functools.partial(_mixout_kernel, seq=L),
        grid=(n,),
        in_specs=in_specs,
        out_specs=[row(D), row(D)],
        out_shape=[jax.ShapeDtypeStruct((L, D), F32), jax.ShapeDtypeStruct((L, D), BF16)],
        scratch_shapes=[pltpu.VMEM((tm + 2 * V7X_SUBLANES, DP), F32)],
        compiler_params=pltpu.CompilerParams(dimension_semantics=("arbitrary",), vmem_limit_bytes=_vmem_limit(est)),
        name="mixout",
    )(u, u, u, y_lo, y_hi, x, pool_w, pool_scale, w_out, g, g_next)


def _ffn_kernel(hp_ref, hm_ref, hx_ref, wg_ref, wv_ref, cw_ref, cb_ref, wd_ref, gpost_ref,
                o_ref, hn_ref, gate_ref):
    i = pl.program_id(0)
    f = pl.program_id(1)
    n = pl.num_programs(0)
    nf = pl.num_programs(1)
    tm = hm_ref.shape[0]
    H = BF16_ROWS

    @pl.when(f == 0)
    def _():
        hn_ref[0:H, :] = jnp.where(i > 0, hp_ref[...], jnp.zeros_like(hp_ref))
        hn_ref[H:H + tm, :] = hm_ref[...]
        hn_ref[H + tm:2 * H + tm, :] = jnp.where(i < n - 1, hx_ref[...], jnp.zeros_like(hx_ref))
        o_ref[...] = jnp.zeros_like(o_ref)

    gate_ref[...] = jnp.dot(hn_ref[...], wg_ref[...], preferred_element_type=F32)
    val = jnp.dot(hm_ref[...], wv_ref[...], preferred_element_type=F32)
    half = FFN_CONV // 2
    gc = cw_ref[0:1, :] * gate_ref[H - half:H - half + tm, :]
    for j in range(1, FFN_CONV):
        gc = gc + cw_ref[j:j + 1, :] * gate_ref[H - half + j:H - half + j + tm, :]
    gc = gc + cb_ref[...]
    act = (jax.nn.gelu(gc, approximate=True) * val).astype(BF16)
    o_ref[...] += jnp.dot(act, wd_ref[...], preferred_element_type=F32)

    @pl.when(f == nf - 1)
    def _():
        o_ref[...] = _rms(o_ref[...]) * gpost_ref[...]


def _ffn(hn, w_up, cw, cb, w_down, gpost, *, tm, tf):
    L, D = hn.shape
    DF = w_down.shape[0]
    nf = DF // tf
    H = BF16_ROWS
    in_specs = _halo_specs(tm, H, L, D) + [
        pl.BlockSpec((D, tf), lambda i, f: (0, f)),
        pl.BlockSpec((D, tf), lambda i, f: (0, nf + f)),
        pl.BlockSpec((FFN_CONV, tf), lambda i, f: (0, f)),
        pl.BlockSpec((1, tf), lambda i, f: (0, f)),
        pl.BlockSpec((tf, D), lambda i, f: (f, 0)),
        pl.BlockSpec(gpost.shape, lambda i, f: (0, 0)),
    ]
    est = (2 * (tm + 2 * H) * D * 2 + 2 * tm * D * 4 + (tm + 2 * H) * D * 2 + (tm + 2 * H) * tf * 4
           + 2 * 3 * D * tf * 2 + 8 * tm * tf * 4 + tm * D * 4)
    return pl.pallas_call(
        _ffn_kernel,
        grid=(L // tm, nf),
        in_specs=in_specs,
        out_specs=pl.BlockSpec((tm, D), lambda i, f: (i, 0)),
        out_shape=jax.ShapeDtypeStruct((L, D), F32),
        scratch_shapes=[
            pltpu.VMEM((tm + 2 * H, D), BF16),
            pltpu.VMEM((tm + 2 * H, tf), F32),
        ],
        compiler_params=pltpu.CompilerParams(dimension_semantics=("arbitrary", "arbitrary"),
                                             vmem_limit_bytes=_vmem_limit(est)),
        name="ffn",
    )(hn, hn, hn, w_up, w_up, cw, cb, w_down, gpost)


def _ple_kernel(h_ref, r_ref, p_ref, gpre_ref, wg_ref, wp_ref, gpost_ref, o_ref):
    h = h_ref[...] + r_ref[...]
    hn = (_rms(h) * gpre_ref[...]).astype(BF16)
    gate = jax.nn.sigmoid(jnp.dot(hn, wg_ref[...], preferred_element_type=F32))
    pe = jnp.dot(p_ref[...].astype(BF16), wp_ref[...], preferred_element_type=F32)
    o_ref[...] = h + _rms(gate * pe) * gpost_ref[...]


def _ple(h, r, p, gpre, w_gate, w_ple, gpost, *, tm):
    L, D = h.shape
    row = lambda n: pl.BlockSpec((tm, n), lambda i: (i, 0))
    est = 2 * tm * (3 * D + p.shape[1]) * 4 + (w_gate.size + w_ple.size) * 2 + 8 * tm * D * 4
    return pl.pallas_call(
        _ple_kernel,
        grid=(L // tm,),
        in_specs=[row(D), row(D), row(p.shape[1])] + [_const_spec(a.shape) for a in (gpre, w_gate, w_ple, gpost)],
        out_specs=row(D),
        out_shape=jax.ShapeDtypeStruct((L, D), F32),
        compiler_params=pltpu.CompilerParams(dimension_semantics=("arbitrary",), vmem_limit_bytes=_vmem_limit(est)),
        name="ple",
    )(h, r, p, gpre, w_gate, w_ple, gpost)


def _ssd_constants():
    Q = SSD_CHUNK
    r = jnp.arange(Q)
    lower = (r[:, None] >= r[None, :])
    tri = jnp.stack([lower, lower.T]).astype(BF16)
    tcat = jnp.concatenate([tri, tri, tri], axis=2)
    negmask = jnp.where(jnp.stack([lower, lower.T]), 0.0, NEG_BIG).astype(F32)
    k = jnp.arange(V7X_LANES)
    rexp = []
    for d in range(2):
        packed = jnp.logical_and(k >= d * SSD_HEADS, k < d * SSD_HEADS + 3 * SSD_PACK)
        head_of_lane = jnp.where(packed, (k - d * SSD_HEADS) % SSD_PACK, -1)
        rexp.append(head_of_lane[:, None] == (jnp.arange(SSD_HEADS * SSD_HEAD_DIM) // SSD_HEAD_DIM)[None, :])
    return tcat, negmask, jnp.stack(rexp).astype(BF16)


def _dir_lanes(v, fill):
    out = jnp.full((1, V7X_LANES), fill, F32)
    return out.at[0, :v.size].set(v.astype(F32).reshape(-1))


def kernel(x, p, mix_norm_pre, mix_norm_post, w_in, ssd_conv_w, ssd_conv_b, ssd_dt_bias, ssd_a_log, ssd_d,
           ssd_norm, pool_w, pool_scale, w_out, ffn_norm_pre, ffn_norm_post, w_ffn_up, ffn_conv_w, ffn_conv_b,
           w_ffn_down, ple_norm_pre, w_ple_gate, w_ple, ple_norm_post):
    B, L, D = x.shape
    depth = w_in.shape[0]
    d_ssd = SSD_HEADS * SSD_HEAD_DIM
    n_bc = 2 * SSD_GROUPS * SSD_STATE
    o_dt = 2 * d_ssd + n_bc
    o_u = o_dt + 2 * SSD_HEADS
    d_pool = w_in.shape[2] - o_u
    tcat, negmask, rexp = _ssd_constants()
    row = lambda v: v.reshape(1, -1).astype(F32)

    outs = []
    for b in range(B):
        h = x[b]
        for i in range(depth):
            xs, bc, z, u, pk, sk, ac, w_down16, w_out16, w_gate16 = _inproj(
                h, row(mix_norm_pre[i]), _cast_rows(w_in[i].T, rows=2 * CONV_PIECE), ssd_conv_w[i].astype(F32),
                row(ssd_conv_b[i]), _dir_lanes(ssd_dt_bias[i], 0.0), _dir_lanes(ssd_a_log[i], NEG_BIG), tcat,
                (d_ssd, n_bc, d_ssd, d_pool), w_ffn_down[i], w_out[i], w_ple_gate[i], tm=INPROJ_ROWS)
            y_lo, y_hi, w_up16 = _ssd(
                xs, bc, pk, sk, ac, z, jnp.repeat(ssd_d[i].astype(F32), SSD_HEAD_DIM).reshape(1, -1),
                row(ssd_norm[i]), negmask, rexp, w_ffn_up[i])
            h, hn = _mixout(u, y_lo, y_hi, h, pool_w[i].astype(BF16), row(pool_scale[i]), w_out16,
                            row(mix_norm_post[i]), row(ffn_norm_pre[i]), tm=MIXOUT_ROWS)
            r = _ffn(hn, w_up16, ffn_conv_w[i].astype(F32), row(ffn_conv_b[i]), w_down16, row(ffn_norm_post[i]),
                     tm=FFN_ROWS, tf=FFN_COLS)
            h = _ple(h, r, p[i, b], row(ple_norm_pre[i]), w_gate16, w_ple[i].astype(BF16),
                     row(ple_norm_post[i]), tm=PLE_ROWS)
        outs.append(h)
    return jnp.stack(outs)
```

```python
import functools

import jax
import jax.numpy as jnp
from jax import lax
from jax.experimental import pallas as pl
from jax.experimental.pallas import tpu as pltpu

F32 = jnp.float32
BF16 = jnp.bfloat16
EPS = 1e-6

V7X_VMEM_BYTES = 64 * 1024 * 1024
V7X_LANES = 128
V7X_SUBLANES = 8
BF16_ROWS = 16

SSD_HEAD_DIM = 64
SSD_HEADS = 16
SSD_GROUPS = 2
SSD_HEADS_PER_GROUP = SSD_HEADS // SSD_GROUPS
SSD_STATE = 128
SSD_CONV = 5
SSD_CHUNK = 128
POOL_WINDOWS = (2, 4, 8, 16)
FFN_CONV = 3
NEG_BIG = -1e30

INPROJ_ROWS = 512
MIXOUT_ROWS = 512
FFN_ROWS = 1024
FFN_COLS = 512
PLE_ROWS = 512
CONV_PIECE = 256


def _vmem_limit(nbytes):
    return int(min(nbytes, V7X_VMEM_BYTES - 6 * 1024 * 1024))


def _rms(x):
    return x * lax.rsqrt(jnp.mean(x * x, axis=-1, keepdims=True) + EPS)


def _split3(x):
    hi = x.astype(BF16).astype(F32)
    r1 = x - hi
    mid = r1.astype(BF16).astype(F32)
    lo = (r1 - mid).astype(BF16).astype(F32)
    return hi, mid, lo


def _const_spec(shape):
    nd = len(shape)
    return pl.BlockSpec(shape, lambda *_: (0,) * nd, pipeline_mode=pl.Buffered(1))


def _halo_specs(rows, halo, total_rows, ncols):
    hb = rows // halo
    nhb = total_rows // halo
    return [
        pl.BlockSpec((halo, ncols), lambda i, *_: (jnp.maximum(i * hb - 1, 0), 0)),
        pl.BlockSpec((rows, ncols), lambda i, *_: (i, 0)),
        pl.BlockSpec((halo, ncols), lambda i, *_: (jnp.minimum(i * hb + hb, nhb - 1), 0)),
    ]


def _cast_t_kernel(w_ref, o_ref, *, valid_rows):
    row = pl.program_id(0) * w_ref.shape[0] + lax.broadcasted_iota(jnp.int32, w_ref.shape, 0)
    w = jnp.where(row < valid_rows, w_ref[...], 0.0)
    o_ref[...] = w.T.astype(o_ref.dtype)


def _cast_transposed(w_t, *, cols):
    C, R = w_t.shape
    n = pl.cdiv(C, cols)
    return pl.pallas_call(
        functools.partial(_cast_t_kernel, valid_rows=C),
        grid=(n,),
        in_specs=[pl.BlockSpec((cols, R), lambda j: (j, 0))],
        out_specs=pl.BlockSpec((R, cols), lambda j: (0, j)),
        out_shape=jax.ShapeDtypeStruct((R, n * cols), BF16),
        compiler_params=pltpu.CompilerParams(dimension_semantics=("arbitrary",),
                                             vmem_limit_bytes=_vmem_limit(V7X_VMEM_BYTES)),
        name="cast_w_in",
    )(w_t)


def _scan_prep(raw, dtb_ref, alog_ref, t_ref, pk_ref, sk_ref, ac_ref):
    Q = SSD_CHUNK
    lane = lax.broadcasted_iota(jnp.int32, (Q, V7X_LANES), 1)
    head_lanes = [jnp.logical_and(lane >= d * SSD_HEADS, lane < (d + 1) * SSD_HEADS) for d in (0, 1)]
    a = -jnp.exp(alog_ref[...])
    for c in range(raw.shape[0] // Q):
        dt = jax.nn.softplus(raw[c * Q:(c + 1) * Q, :] + dtb_ref[...])
        da3 = jnp.concatenate([v.astype(BF16) for v in _split3(dt * a)], axis=0)
        acum = jnp.where(head_lanes[0], jnp.dot(t_ref[0], da3, preferred_element_type=F32),
                         jnp.dot(t_ref[1], da3, preferred_element_type=F32)) * LOG2E
        tot = jnp.where(head_lanes[0][0:1], acum[Q - 1:Q, :], acum[0:1, :])
        ac_ref[c * Q:(c + 1) * Q, :] = acum
        sk_ref[c] = (acum - jnp.log2(dt)).T[0:2 * SSD_HEADS, :]
        e_a = jnp.exp2(acum)
        dtw = dt * jnp.exp2(tot - acum)
        for d in (0, 1):
            pk_ref[c, d, 0:Q, :] = _pack3(dtw, head_lanes[d])
            pk_ref[c, d, Q:2 * Q, :] = _pack3(e_a, head_lanes[d])


def _inproj_kernel(xp_ref, xm_ref, xn_ref, g_ref, w_ref, cw_ref, cb_ref, dtb_ref, alog_ref, t_ref,
                   wd_ref, wo_ref, wg_ref,
                   xs_ref, bc_ref, z_ref, u_ref, pk_ref, sk_ref, ac_ref, wd16_ref, wo16_ref, wg16_ref,
                   hn_ref, *ext_refs):
    i = pl.program_id(0)
    n = pl.num_programs(0)
    tm = xm_ref.shape[0]
    H = BF16_ROWS
    g = g_ref[...]
    hn_ref[0:H, :] = jnp.where(i > 0, _rms(xp_ref[...]) * g, 0.0).astype(BF16)
    hn_ref[H:H + tm, :] = (_rms(xm_ref[...]) * g).astype(BF16)
    hn_ref[H + tm:2 * H + tm, :] = jnp.where(i < n - 1, _rms(xn_ref[...]) * g, 0.0).astype(BF16)
    for src, dst in ((wd_ref, wd16_ref), (wo_ref, wo16_ref), (wg_ref, wg16_ref)):
        dst[...] = src[...].astype(BF16)

    cs = CONV_PIECE
    nz, nxs, nbc, nu = z_ref.shape[1], xs_ref.shape[1], bc_ref.shape[1], u_ref.shape[1]
    o_dt = nz + nxs + nbc
    n_dt = 2 * SSD_HEADS
    w_end = o_dt + n_dt + nu
    conv_outs = [(xs_ref, c, nz + c, c) for c in range(0, nxs, cs)]
    conv_outs += [(bc_ref, c, nz + nxs + c, nxs + c) for c in range(0, nbc, cs)]

    def project(c0, c1):
        return jnp.dot(hn_ref[H:H + tm, :], w_ref[:, c0:c1], preferred_element_type=F32)

    def z_chunk(c):
        z_ref[:, c:c + cs] = project(c, c + cs)

    split = o_dt + 2 * cs
    tail = [project(o_dt, split)]
    _scan_prep(tail[0][:, 0:V7X_LANES], dtb_ref, alog_ref, t_ref, pk_ref, sk_ref, ac_ref)
    plain = [functools.partial(z_chunk, c) for c in range(0, nz, cs)]
    plain += [lambda: tail.append(project(split, w_end))]

    half = SSD_CONV // 2
    for k, (o_ref, c0, wc, cc) in enumerate(conv_outs):
        ext_ref = ext_refs[k % len(ext_refs)]
        ext_ref[...] = jnp.dot(hn_ref[...], w_ref[:, wc:wc + cs], preferred_element_type=F32)
        if k < len(plain):
            plain[k]()
        acc = cw_ref[0:1, cc:cc + cs] * ext_ref[H - half:H - half + tm, :]
        for j in range(1, SSD_CONV):
            acc = acc + cw_ref[j:j + 1, cc:cc + cs] * ext_ref[H - half + j:H - half + j + tm, :]
        acc = acc + cb_ref[:, cc:cc + cs]
        o_ref[:, c0:c0 + cs] = (acc * jax.nn.sigmoid(acc)).astype(o_ref.dtype)
    for fn in plain[len(conv_outs):]:
        fn()
    u_ref[...] = jnp.concatenate([tail[0][:, n_dt:], tail[1]], axis=1)


def _inproj(x, g, w, cw, cb, dtb, alog, tcat, out_cols, w_down, w_out, w_gate, *, tm):
    L, D = x.shape
    H = BF16_ROWS
    Q = SSD_CHUNK
    n = L // tm
    dts = (BF16, BF16, F32, F32)
    row = lambda c: pl.BlockSpec((tm, c), lambda i: (i, 0))
    slab = lambda a: pl.BlockSpec((a.shape[0] // n, a.shape[1]), lambda i: (i, 0))
    casts = (w_down, w_out, w_gate)
    consts = (g, w, cw, cb, dtb, alog, tcat)
    prep_shapes = [((L // Q, 2, 2 * Q, V7X_LANES), BF16), ((L // Q, 2 * SSD_HEADS, Q), F32), ((L, V7X_LANES), F32)]
    prep_specs = [pl.BlockSpec((tm // Q,) + s[1:], lambda i, nd=len(s): (i,) + (0,) * (nd - 1))
                  for s, _ in prep_shapes[:2]] + [row(V7X_LANES)]
    est = (2 * (tm + 2 * H) * D * 4 + w.size * 2 + 2 * tm * sum(out_cols) * 4
           + (tm + 2 * H) * D * 2 + 4 * (tm + 2 * H) * CONV_PIECE * 4 + 2 * sum(a.size for a in casts) // n * 6
           + 24 * tm * CONV_PIECE * 4)
    return pl.pallas_call(
        _inproj_kernel,
        grid=(n,),
        in_specs=_halo_specs(tm, H, L, D) + [_const_spec(a.shape) for a in consts] + [slab(a) for a in casts],
        out_specs=[row(c) for c in out_cols] + prep_specs + [slab(a) for a in casts],
        out_shape=[jax.ShapeDtypeStruct((L, c), dt) for c, dt in zip(out_cols, dts)]
                  + [jax.ShapeDtypeStruct(s, dt) for s, dt in prep_shapes]
                  + [jax.ShapeDtypeStruct(a.shape, BF16) for a in casts],
        scratch_shapes=[pltpu.VMEM((tm + 2 * H, D), BF16)]
                       + [pltpu.VMEM((tm + 2 * H, CONV_PIECE), F32) for _ in range(2)],
        compiler_params=pltpu.CompilerParams(dimension_semantics=("arbitrary",), vmem_limit_bytes=_vmem_limit(est)),
        name="inproj",
    )(x, x, x, *consts, *casts)


SSD_PACK = 16


LOG2E = 1.4426950408889634


def _pack3(v, head_lanes):
    hi, mid, lo = _split3(jnp.where(head_lanes, v, jnp.zeros_like(v)))
    packed = hi + pltpu.roll(mid, SSD_PACK, axis=1) + pltpu.roll(lo, 2 * SSD_PACK, axis=1)
    return packed.astype(BF16)


def _scan_both(xs_refs, bc_refs, pk_refs, sk_refs, ac_refs, nm_ref, rexp_ref, h_ref):
    Q = SSD_CHUNK
    P = SSD_HEAD_DIM
    E = SSD_HEADS_PER_GROUP
    GW = E * P
    DIRS = (0, 1)
    last = (Q - 1, 0)
    xs16 = [xs_refs[d][...] for d in DIRS]
    bc = [bc_refs[d][...] for d in DIRS]
    src_t = [sk_refs[d][...] for d in DIRS]

    exp2x = [jnp.dot(pk_refs[d][0:2 * Q, :], rexp_ref[d], preferred_element_type=F32) for d in DIRS]
    acum = [ac_refs[d][...] for d in DIRS]
    xw = [xs16[d] * exp2x[d][0:Q].astype(BF16) for d in DIRS]
    ea_x = [exp2x[d][Q:2 * Q] for d in DIRS]
    negmask = [nm_ref[d] for d in DIRS]
    first_head = lax.broadcasted_iota(jnp.int32, (Q, 2 * P), 1) < P

    y_groups = [[], []]
    for g in range(SSD_GROUPS):
        c0 = g * GW
        bm = [bc[d][:, g * SSD_STATE:(g + 1) * SSD_STATE] for d in DIRS]
        cm = [bc[d][:, (SSD_GROUPS + g) * SSD_STATE:(SSD_GROUPS + g + 1) * SSD_STATE] for d in DIRS]
        cb = [lax.dot_general(cm[d], bm[d], (((1,), (1,)), ((), ())), preferred_element_type=F32)
              for d in DIRS]
        st = [lax.dot_general(bm[d], xw[d][:, c0:c0 + GW], (((0,), (0,)), ((), ())), preferred_element_type=F32)
              for d in DIRS]
        h_in = [h_ref[d, g] for d in DIRS]
        y_off = [jnp.dot(cm[d], h_in[d].astype(BF16), preferred_element_type=F32) for d in DIRS]
        y_g = [y_off[d] * ea_x[d][:, c0:c0 + GW] for d in DIRS]
        for d in DIRS:
            h_ref[d, g] = h_in[d] * ea_x[d][last[d]:last[d] + 1, c0:c0 + GW] + st[d]
        pairs = [[], []]
        for hp in range(E // 2):
            for d in DIRS:
                h0 = g * E + hp * 2
                ms = []
                for k in range(2):
                    r = d * SSD_HEADS + h0 + k
                    seg = jnp.broadcast_to(acum[d][:, r:r + 1], (Q, Q)) - src_t[d][r:r + 1, :] + negmask[d]
                    ms.append((cb[d] * jnp.exp2(seg)).astype(BF16))
                m2 = jnp.concatenate(ms, axis=1)
                xp = xs16[d][:, (g * E + hp * 2) * P:(g * E + hp * 2 + 2) * P]
                zero = jnp.zeros_like(xp)
                rhs = jnp.concatenate([jnp.where(first_head, xp, zero), jnp.where(first_head, zero, xp)], axis=0)
                pairs[d].append(jnp.dot(m2, rhs, preferred_element_type=F32))
        for d in DIRS:
            y_groups[d].append(y_g[d] + jnp.concatenate(pairs[d], axis=1))
    return [(jnp.concatenate(y_groups[d], axis=1), xs16[d]) for d in DIRS]


def _ssd_kernel(xsf_ref, bcf_ref, pkf_ref, skf_ref, acf_ref, zf_ref, xsb_ref, bcb_ref, pkb_ref, skb_ref, acb_ref, zb_ref,
                dexp_ref, nw_ref, nm_ref, rexp_ref, wup_ref,
                lo_ref, hi_ref, wup16_ref,
                y_ref, h_ref):
    Q = SSD_CHUNK
    GW = SSD_HEADS_PER_GROUP * SSD_HEAD_DIM
    i = pl.program_id(0)
    nc = pl.num_programs(0)

    wup16_ref[...] = wup_ref[...].astype(BF16)

    @pl.when(i == 0)
    def _():
        h_ref[...] = jnp.zeros_like(h_ref)

    (y_f, xs_f), (y_b, xs_b) = _scan_both((xsf_ref, xsb_ref), (bcf_ref, bcb_ref), (pkf_ref, pkb_ref),
                                          (skf_ref, skb_ref), (acf_ref, acb_ref), nm_ref, rexp_ref, h_ref)
    row_f = pl.multiple_of(i * Q, Q)
    row_b = pl.multiple_of((nc - 1 - i) * Q, Q)

    @pl.when(i < nc // 2)
    def _():
        y_ref[pl.ds(row_f, Q), :] = y_f
        y_ref[pl.ds(row_b, Q), :] = y_b

    def finish(y, xs, z_ref, o_ref):
        yy = y + xs.astype(F32) * dexp_ref[...]
        z = z_ref[...]
        yy = yy * (z * jax.nn.sigmoid(z))
        outs = [_rms(yy[:, g * GW:(g + 1) * GW]) for g in range(SSD_GROUPS)]
        o_ref[...] = (jnp.concatenate(outs, axis=1) * nw_ref[...]).astype(o_ref.dtype)

    @pl.when(i >= nc // 2)
    def _():
        finish(y_f + y_ref[pl.ds(row_f, Q), :], xs_f, zf_ref, hi_ref)
        finish(y_b + y_ref[pl.ds(row_b, Q), :], xs_b, zb_ref, lo_ref)


def _ssd(xs, bc, pk, sk, ac, z, dexp, nw, negmask, rexp, w_up):
    L, DS = xs.shape
    Q = SSD_CHUNK
    nc = L // Q
    hc = nc // 2
    slab = pl.BlockSpec((w_up.shape[0] // nc, w_up.shape[1]), lambda i: (i, 0))
    fwd = lambda i: i
    bwd = lambda i: nc - 1 - i
    fwd_late = lambda i: jnp.maximum(i, hc)
    bwd_late = lambda i: jnp.minimum(nc - 1 - i, hc - 1)
    in_specs = [
        pl.BlockSpec((Q, DS), lambda i: (fwd(i), 0)),
        pl.BlockSpec((Q, bc.shape[1]), lambda i: (fwd(i), 0)),
        pl.BlockSpec((None, None) + pk.shape[2:], lambda i: (fwd(i), 0, 0, 0)),
        pl.BlockSpec((None,) + sk.shape[1:], lambda i: (fwd(i), 0, 0)),
        pl.BlockSpec((Q, V7X_LANES), lambda i: (fwd(i), 0)),
        pl.BlockSpec((Q, DS), lambda i: (fwd_late(i), 0)),
        pl.BlockSpec((Q, DS), lambda i: (bwd(i), 0)),
        pl.BlockSpec((Q, bc.shape[1]), lambda i: (bwd(i), 0)),
        pl.BlockSpec((None, None) + pk.shape[2:], lambda i: (bwd(i), 1, 0, 0)),
        pl.BlockSpec((None,) + sk.shape[1:], lambda i: (bwd(i), 0, 0)),
        pl.BlockSpec((Q, V7X_LANES), lambda i: (bwd(i), 0)),
        pl.BlockSpec((Q, DS), lambda i: (bwd_late(i), 0)),
    ] + [_const_spec(a.shape) for a in (dexp, nw, negmask, rexp)] + [slab]
    est = (L * DS * 4 + 2 * SSD_GROUPS * SSD_STATE * DS * 4 + 8 * Q * (DS + bc.shape[1]) * 4
           + rexp.size * 2 + 4 * Q * DS * 4 * 2 + 40 * Q * DS * 4 + 12 * w_up.size // nc)
    return pl.pallas_call(
        _ssd_kernel,
        grid=(nc,),
        in_specs=in_specs,
        out_specs=[pl.BlockSpec((Q, DS), lambda i: (bwd_late(i), 0)),
                   pl.BlockSpec((Q, DS), lambda i: (fwd_late(i) - hc, 0)), slab],
        out_shape=[jax.ShapeDtypeStruct((L // 2, DS), BF16), jax.ShapeDtypeStruct((L // 2, DS), BF16),
                   jax.ShapeDtypeStruct(w_up.shape, BF16)],
        scratch_shapes=[
            pltpu.VMEM((L, DS), F32),
            pltpu.VMEM((2, SSD_GROUPS, SSD_STATE, DS // SSD_GROUPS), F32),
        ],
        compiler_params=pltpu.CompilerParams(dimension_semantics=("arbitrary",), vmem_limit_bytes=_vmem_limit(est)),
        name="ssd",
    )(xs, bc, pk, sk, ac, z, xs, bc, pk, sk, ac, z, dexp, nw, negmask, rexp, w_up)


def _mixout_kernel(up_ref, um_ref, un_ref, ylo_ref, yhi_ref, x_ref, pw_ref, ps_ref, wo_ref, g_ref, gn_ref,
                   o_ref, on_ref, ext_ref, mix_ref, *, seq):
    i = pl.program_id(0)
    n = pl.num_programs(0)
    tm = um_ref.shape[0]
    cg = um_ref.shape[1] // len(POOL_WINDOWS)
    H = V7X_SUBLANES
    ext_ref[0:H, :] = jnp.where(i > 0, up_ref[...], 0.0)
    ext_ref[H:H + tm, :] = um_ref[...]
    ext_ref[H + tm:2 * H + tm, :] = jnp.where(i < n - 1, un_ref[...], 0.0)
    ys = jnp.where(i < n // 2, ylo_ref[...], yhi_ref[...])
    ds = ys.shape[1]
    dc = o_ref.shape[1] // len(POOL_WINDOWS)
    t = i * tm + lax.broadcasted_iota(jnp.int32, (tm, cg), 0)
    rows = tm + 2 * H

    def ahead(v, k):
        return pltpu.roll(v, (rows - k) % rows, axis=0)

    pooled = []
    for gi, k in enumerate(POOL_WINDOWS):
        cols = slice(gi * cg, (gi + 1) * cg)
        mix_ref[:, gi * dc:(gi + 1) * dc] = jnp.dot(ys, wo_ref[0:ds, gi * dc:(gi + 1) * dc],
                                                    preferred_element_type=F32)
        e = ext_ref[:, cols]
        half = k // 2
        run, length = e, 1
        while length < half:
            run = run + ahead(run, length)
            length *= 2
        before = run[0:tm] if half == H else ahead(run, rows - half)[H:H + tm]
        acc = before + run[H:H + tm]
        cnt = (jnp.minimum(t + (k - k // 2), seq) - jnp.maximum(t - k // 2, 0)).astype(F32)
        mixed = acc / cnt - um_ref[:, cols]
        yp = jnp.dot(mixed.astype(BF16), pw_ref[gi], preferred_element_type=F32) * ps_ref[:, cols]
        pooled.append(yp.astype(BF16))
    ypool = jnp.concatenate(pooled, axis=1)
    mix = mix_ref[...] + jnp.dot(ypool, wo_ref[ds:, :], preferred_element_type=F32)
    h = x_ref[...] + _rms(mix) * g_ref[...]
    o_ref[...] = h
    on_ref[...] = (_rms(h) * gn_ref[...]).astype(on_ref.dtype)


def _mixout(u, y_lo, y_hi, x, pool_w, pool_scale, w_out, g, g_next, *, tm):
    assert all(k % 2 == 0 and k // 2 <= V7X_SUBLANES and (k // 2) & (k // 2 - 1) == 0 for k in POOL_WINDOWS)
    L, DP = u.shape
    D = x.shape[1]
    DS = y_lo.shape[1]
    n = L // tm
    row = lambda n_: pl.BlockSpec((tm, n_), lambda i: (i, 0))
    in_specs = (_halo_specs(tm, V7X_SUBLANES, L, DP)
                + [pl.BlockSpec((tm, DS), lambda i: (jnp.minimum(i, n // 2 - 1), 0)),
                   pl.BlockSpec((tm, DS), lambda i: (jnp.maximum(i - n // 2, 0), 0)), row(D)]
                + [_const_spec(a.shape) for a in (pool_w, pool_scale, w_out, g, g_next)])
    est = (2 * tm * (DP * 4 + 2 * DS * 2 + D * 4 + D * 4 + D * 2) + (w_out.size + pool_w.size) * 2
           + (tm + 2 * V7X_SUBLANES) * DP * 4 + 8 * tm * D * 4)
    return pl.pallas_call(
        functools.partial(_mixout_kernel, seq=L),
        grid=(n,),
        in_specs=in_specs,
        out_specs=[row(D), row(D)],
        out_shape=[jax.ShapeDtypeStruct((L, D), F32), jax.ShapeDtypeStruct((L, D), BF16)],
        scratch_shapes=[pltpu.VMEM((tm + 2 * V7X_SUBLANES, DP), F32), pltpu.VMEM((tm, D), F32)],
        compiler_params=pltpu.CompilerParams(dimension_semantics=("arbitrary",), vmem_limit_bytes=_vmem_limit(est)),
        name="mixout",
    )(u, u, u, y_lo, y_hi, x, pool_w, pool_scale, w_out, g, g_next)


def _ffn_kernel(hp_ref, hm_ref, hx_ref, wg_ref, wv_ref, cw_ref, cb_ref, wd_ref, gpost_ref,
                o_ref, hn_ref, gate_ref):
    i = pl.program_id(0)
    f = pl.program_id(1)
    n = pl.num_programs(0)
    nf = pl.num_programs(1)
    tm = hm_ref.shape[0]
    H = BF16_ROWS

    @pl.when(f == 0)
    def _():
        hn_ref[0:H, :] = jnp.where(i > 0, hp_ref[...], jnp.zeros_like(hp_ref))
        hn_ref[H:H + tm, :] = hm_ref[...]
        hn_ref[H + tm:2 * H + tm, :] = jnp.where(i < n - 1, hx_ref[...], jnp.zeros_like(hx_ref))
        o_ref[...] = jnp.zeros_like(o_ref)

    gate_ref[...] = jnp.dot(hn_ref[...], wg_ref[...], preferred_element_type=F32)
    val = jnp.dot(hm_ref[...], wv_ref[...], preferred_element_type=F32)
    half = FFN_CONV // 2
    gc = cw_ref[0:1, :] * gate_ref[H - half:H - half + tm, :]
    for j in range(1, FFN_CONV):
        gc = gc + cw_ref[j:j + 1, :] * gate_ref[H - half + j:H - half + j + tm, :]
    gc = gc + cb_ref[...]
    act = (jax.nn.gelu(gc, approximate=True) * val).astype(BF16)
    o_ref[...] += jnp.dot(act, wd_ref[...], preferred_element_type=F32)

    @pl.when(f == nf - 1)
    def _():
        o_ref[...] = _rms(o_ref[...]) * gpost_ref[...]


def _ffn(hn, w_up, cw, cb, w_down, gpost, *, tm, tf):
    L, D = hn.shape
    DF = w_down.shape[0]
    nf = DF // tf
    H = BF16_ROWS
    in_specs = _halo_specs(tm, H, L, D) + [
        pl.BlockSpec((D, tf), lambda i, f: (0, f)),
        pl.BlockSpec((D, tf), lambda i, f: (0, nf + f)),
        pl.BlockSpec((FFN_CONV, tf), lambda i, f: (0, f)),
        pl.BlockSpec((1, tf), lambda i, f: (0, f)),
        pl.BlockSpec((tf, D), lambda i, f: (f, 0)),
        pl.BlockSpec(gpost.shape, lambda i, f: (0, 0)),
    ]
    est = (2 * (tm + 2 * H) * D * 2 + 2 * tm * D * 4 + (tm + 2 * H) * D * 2 + (tm + 2 * H) * tf * 4
           + 2 * 3 * D * tf * 2 + 8 * tm * tf * 4 + tm * D * 4)
    return pl.pallas_call(
        _ffn_kernel,
        grid=(L // tm, nf),
        in_specs=in_specs,
        out_specs=pl.BlockSpec((tm, D), lambda i, f: (i, 0)),
        out_shape=jax.ShapeDtypeStruct((L, D), F32),
        scratch_shapes=[
            pltpu.VMEM((tm + 2 * H, D), BF16),
            pltpu.VMEM((tm + 2 * H, tf), F32),
        ],
        compiler_params=pltpu.CompilerParams(dimension_semantics=("arbitrary", "arbitrary"),
                                             vmem_limit_bytes=_vmem_limit(est)),
        name="ffn",
    )(hn, hn, hn, w_up, w_up, cw, cb, w_down, gpost)


def _ple_kernel(h_ref, r_ref, p_ref, gpre_ref, wg_ref, wp_ref, gpost_ref, o_ref):
    h = h_ref[...] + r_ref[...]
    hn = (_rms(h) * gpre_ref[...]).astype(BF16)
    gate = jax.nn.sigmoid(jnp.dot(hn, wg_ref[...], preferred_element_type=F32))
    pe = jnp.dot(p_ref[...].astype(BF16), wp_ref[...], preferred_element_type=F32)
    o_ref[...] = h + _rms(gate * pe) * gpost_ref[...]


def _ple(h, r, p, gpre, w_gate, w_ple, gpost, *, tm):
    L, D = h.shape
    row = lambda n: pl.BlockSpec((tm, n), lambda i: (i, 0))
    est = 2 * tm * (3 * D + p.shape[1]) * 4 + (w_gate.size + w_ple.size) * 2 + 8 * tm * D * 4
    return pl.pallas_call(
        _ple_kernel,
        grid=(L // tm,),
        in_specs=[row(D), row(D), row(p.shape[1])] + [_const_spec(a.shape) for a in (gpre, w_gate, w_ple, gpost)],
        out_specs=row(D),
        out_shape=jax.ShapeDtypeStruct((L, D), F32),
        compiler_params=pltpu.CompilerParams(dimension_semantics=("arbitrary",), vmem_limit_bytes=_vmem_limit(est)),
        name="ple",
    )(h, r, p, gpre, w_gate, w_ple, gpost)


def _ssd_constants():
    Q = SSD_CHUNK
    r = jnp.arange(Q)
    lower = (r[:, None] >= r[None, :])
    tri = jnp.stack([lower, lower.T]).astype(BF16)
    tcat = jnp.concatenate([tri, tri, tri], axis=2)
    negmask = jnp.where(jnp.stack([lower, lower.T]), 0.0, NEG_BIG).astype(F32)
    k = jnp.arange(V7X_LANES)
    rexp = []
    for d in range(2):
        packed = jnp.logical_and(k >= d * SSD_HEADS, k < d * SSD_HEADS + 3 * SSD_PACK)
        head_of_lane = jnp.where(packed, (k - d * SSD_HEADS) % SSD_PACK, -1)
        rexp.append(head_of_lane[:, None] == (jnp.arange(SSD_HEADS * SSD_HEAD_DIM) // SSD_HEAD_DIM)[None, :])
    return tcat, negmask, jnp.stack(rexp).astype(BF16)


def _dir_lanes(v, fill):
    out = jnp.full((1, V7X_LANES), fill, F32)
    return out.at[0, :v.size].set(v.astype(F32).reshape(-1))


def kernel(x, p, mix_norm_pre, mix_norm_post, w_in, ssd_conv_w, ssd_conv_b, ssd_dt_bias, ssd_a_log, ssd_d,
           ssd_norm, pool_w, pool_scale, w_out, ffn_norm_pre, ffn_norm_post, w_ffn_up, ffn_conv_w, ffn_conv_b,
           w_ffn_down, ple_norm_pre, w_ple_gate, w_ple, ple_norm_post):
    B, L, D = x.shape
    depth = w_in.shape[0]
    d_ssd = SSD_HEADS * SSD_HEAD_DIM
    n_bc = 2 * SSD_GROUPS * SSD_STATE
    o_dt = 2 * d_ssd + n_bc
    o_u = o_dt + 2 * SSD_HEADS
    d_pool = w_in.shape[2] - o_u
    tcat, negmask, rexp = _ssd_constants()
    row = lambda v: v.reshape(1, -1).astype(F32)

    outs = []
    for b in range(B):
        h = x[b]
        for i in range(depth):
            xs, bc, z, u, pk, sk, ac, w_down16, w_out16, w_gate16 = _inproj(
                h, row(mix_norm_pre[i]), _cast_transposed(w_in[i].T, cols=CONV_PIECE), ssd_conv_w[i].astype(F32),
                row(ssd_conv_b[i]), _dir_lanes(ssd_dt_bias[i], 0.0), _dir_lanes(ssd_a_log[i], NEG_BIG), tcat,
                (d_ssd, n_bc, d_ssd, d_pool), w_ffn_down[i], w_out[i], w_ple_gate[i], tm=INPROJ_ROWS)
            y_lo, y_hi, w_up16 = _ssd(
                xs, bc, pk, sk, ac, z, jnp.repeat(ssd_d[i].astype(F32), SSD_HEAD_DIM).reshape(1, -1),
                row(ssd_norm[i]), negmask, rexp, w_ffn_up[i])
            h, hn = _mixout(u, y_lo, y_hi, h, pool_w[i].astype(BF16), row(pool_scale[i]), w_out16,
                            row(mix_norm_post[i]), row(ffn_norm_pre[i]), tm=MIXOUT_ROWS)
            r = _ffn(hn, w_up16, ffn_conv_w[i].astype(F32), row(ffn_conv_b[i]), w_down16, row(ffn_norm_post[i]),
                     tm=FFN_ROWS, tf=FFN_COLS)
            h = _ple(h, r, p[i, b], row(ple_norm_pre[i]), w_gate16, w_ple[i].astype(BF16),
                     row(ple_norm_post[i]), tm=PLE_ROWS)
        outs.append(h)
    return jnp.stack(outs)
```

```python
import functools

import jax
import jax.numpy as jnp
from jax import lax
from jax.experimental import pallas as pl
from jax.experimental.pallas import tpu as pltpu

F32 = jnp.float32
BF16 = jnp.bfloat16
EPS = 1e-6

V7X_VMEM_BYTES = 64 * 1024 * 1024
V7X_LANES = 128
V7X_SUBLANES = 8
BF16_ROWS = 16

SSD_HEAD_DIM = 64
SSD_HEADS = 16
SSD_GROUPS = 2
SSD_HEADS_PER_GROUP = SSD_HEADS // SSD_GROUPS
SSD_STATE = 128
SSD_CONV = 5
SSD_CHUNK = 128
POOL_WINDOWS = (2, 4, 8, 16)
FFN_CONV = 3
NEG_BIG = -1e30

INPROJ_ROWS = 512
MIXOUT_ROWS = 512
FFN_ROWS = 1024
FFN_COLS = 512
PLE_ROWS = 512
CONV_PIECE = 256


def _vmem_limit(nbytes):
    return int(min(nbytes, V7X_VMEM_BYTES - 6 * 1024 * 1024))


def _rms(x):
    return x * lax.rsqrt(jnp.mean(x * x, axis=-1, keepdims=True) + EPS)


def _split3(x):
    hi = x.astype(BF16).astype(F32)
    r1 = x - hi
    mid = r1.astype(BF16).astype(F32)
    lo = (r1 - mid).astype(BF16).astype(F32)
    return hi, mid, lo


def _const_spec(shape):
    nd = len(shape)
    return pl.BlockSpec(shape, lambda *_: (0,) * nd, pipeline_mode=pl.Buffered(1))


def _halo_specs(rows, halo, total_rows, ncols):
    hb = rows // halo
    nhb = total_rows // halo
    return [
        pl.BlockSpec((halo, ncols), lambda i, *_: (jnp.maximum(i * hb - 1, 0), 0)),
        pl.BlockSpec((rows, ncols), lambda i, *_: (i, 0)),
        pl.BlockSpec((halo, ncols), lambda i, *_: (jnp.minimum(i * hb + hb, nhb - 1), 0)),
    ]


def _cast_t_kernel(w_ref, o_ref, *, valid_rows):
    row = pl.program_id(0) * w_ref.shape[0] + lax.broadcasted_iota(jnp.int32, w_ref.shape, 0)
    w = jnp.where(row < valid_rows, w_ref[...], 0.0)
    o_ref[...] = w.T.astype(o_ref.dtype)


def _cast_transposed(w_t, *, cols):
    C, R = w_t.shape
    n = pl.cdiv(C, cols)
    return pl.pallas_call(
        functools.partial(_cast_t_kernel, valid_rows=C),
        grid=(n,),
        in_specs=[pl.BlockSpec((cols, R), lambda j: (j, 0))],
        out_specs=pl.BlockSpec((R, cols), lambda j: (0, j)),
        out_shape=jax.ShapeDtypeStruct((R, n * cols), BF16),
        compiler_params=pltpu.CompilerParams(dimension_semantics=("arbitrary",),
                                             vmem_limit_bytes=_vmem_limit(V7X_VMEM_BYTES)),
        name="cast_w_in",
    )(w_t)


def _scan_prep(raw, dtb_ref, alog_ref, t_ref, pk_ref, sk_ref, ac_ref):
    Q = SSD_CHUNK
    lane = lax.broadcasted_iota(jnp.int32, (Q, V7X_LANES), 1)
    head_lanes = [jnp.logical_and(lane >= d * SSD_HEADS, lane < (d + 1) * SSD_HEADS) for d in (0, 1)]
    a = -jnp.exp(alog_ref[...])
    for c in range(raw.shape[0] // Q):
        dt = jax.nn.softplus(raw[c * Q:(c + 1) * Q, :] + dtb_ref[...])
        da3 = jnp.concatenate([v.astype(BF16) for v in _split3(dt * a)], axis=0)
        acum = jnp.where(head_lanes[0], jnp.dot(t_ref[0], da3, preferred_element_type=F32),
                         jnp.dot(t_ref[1], da3, preferred_element_type=F32)) * LOG2E
        tot = jnp.where(head_lanes[0][0:1], acum[Q - 1:Q, :], acum[0:1, :])
        ac_ref[c * Q:(c + 1) * Q, :] = acum
        sk_ref[c] = (acum - jnp.log2(dt)).T[0:2 * SSD_HEADS, :]
        e_a = jnp.exp2(acum)
        dtw = dt * jnp.exp2(tot - acum)
        for d in (0, 1):
            pk_ref[c, d, 0:Q, :] = _pack3(dtw, head_lanes[d])
            pk_ref[c, d, Q:2 * Q, :] = _pack3(e_a, head_lanes[d])


def _inproj_kernel(xp_ref, xm_ref, xn_ref, g_ref, w_ref, cw_ref, cb_ref, dtb_ref, alog_ref, t_ref,
                   wo_ref, wg_ref,
                   xs_ref, bc_ref, z_ref, u_ref, pk_ref, sk_ref, ac_ref, wo16_ref, wg16_ref,
                   hn_ref, *ext_refs):
    i = pl.program_id(0)
    n = pl.num_programs(0)
    tm = xm_ref.shape[0]
    H = BF16_ROWS
    g = g_ref[...]
    hn_ref[0:H, :] = jnp.where(i > 0, _rms(xp_ref[...]) * g, 0.0).astype(BF16)
    hn_ref[H:H + tm, :] = (_rms(xm_ref[...]) * g).astype(BF16)
    hn_ref[H + tm:2 * H + tm, :] = jnp.where(i < n - 1, _rms(xn_ref[...]) * g, 0.0).astype(BF16)
    for src, dst in ((wo_ref, wo16_ref), (wg_ref, wg16_ref)):
        dst[...] = src[...].astype(BF16)

    cs = CONV_PIECE
    nz, nxs, nbc, nu = z_ref.shape[1], xs_ref.shape[1], bc_ref.shape[1], u_ref.shape[1]
    o_dt = nz + nxs + nbc
    n_dt = 2 * SSD_HEADS
    w_end = o_dt + n_dt + nu
    conv_outs = [(xs_ref, c, nz + c, c) for c in range(0, nxs, cs)]
    conv_outs += [(bc_ref, c, nz + nxs + c, nxs + c) for c in range(0, nbc, cs)]

    def project(c0, c1):
        return jnp.dot(hn_ref[H:H + tm, :], w_ref[:, c0:c1], preferred_element_type=F32)

    def z_chunk(c):
        z_ref[:, c:c + cs] = project(c, c + cs)

    split = o_dt + 2 * cs
    tail = [project(o_dt, split)]
    _scan_prep(tail[0][:, 0:V7X_LANES], dtb_ref, alog_ref, t_ref, pk_ref, sk_ref, ac_ref)
    plain = [functools.partial(z_chunk, c) for c in range(0, nz, cs)]
    plain += [lambda: tail.append(project(split, w_end))]

    half = SSD_CONV // 2
    for k, (o_ref, c0, wc, cc) in enumerate(conv_outs):
        ext_ref = ext_refs[k % len(ext_refs)]
        ext_ref[...] = jnp.dot(hn_ref[...], w_ref[:, wc:wc + cs], preferred_element_type=F32)
        if k < len(plain):
            plain[k]()
        acc = cw_ref[0:1, cc:cc + cs] * ext_ref[H - half:H - half + tm, :]
        for j in range(1, SSD_CONV):
            acc = acc + cw_ref[j:j + 1, cc:cc + cs] * ext_ref[H - half + j:H - half + j + tm, :]
        acc = acc + cb_ref[:, cc:cc + cs]
        o_ref[:, c0:c0 + cs] = (acc * jax.nn.sigmoid(acc)).astype(o_ref.dtype)
    for fn in plain[len(conv_outs):]:
        fn()
    u_ref[...] = jnp.concatenate([tail[0][:, n_dt:], tail[1]], axis=1)


def _inproj(x, g, w, cw, cb, dtb, alog, tcat, out_cols, w_out, w_gate, *, tm):
    L, D = x.shape
    H = BF16_ROWS
    Q = SSD_CHUNK
    n = L // tm
    dts = (BF16, BF16, F32, F32)
    row = lambda c: pl.BlockSpec((tm, c), lambda i: (i, 0))
    slab = lambda a: pl.BlockSpec((a.shape[0] // n, a.shape[1]), lambda i: (i, 0))
    casts = (w_out, w_gate)
    consts = (g, w, cw, cb, dtb, alog, tcat)
    prep_shapes = [((L // Q, 2, 2 * Q, V7X_LANES), BF16), ((L // Q, 2 * SSD_HEADS, Q), F32), ((L, V7X_LANES), F32)]
    prep_specs = [pl.BlockSpec((tm // Q,) + s[1:], lambda i, nd=len(s): (i,) + (0,) * (nd - 1))
                  for s, _ in prep_shapes[:2]] + [row(V7X_LANES)]
    est = (2 * (tm + 2 * H) * D * 4 + w.size * 2 + 2 * tm * sum(out_cols) * 4
           + (tm + 2 * H) * D * 2 + 4 * (tm + 2 * H) * CONV_PIECE * 4 + 2 * sum(a.size for a in casts) // n * 6
           + 24 * tm * CONV_PIECE * 4)
    return pl.pallas_call(
        _inproj_kernel,
        grid=(n,),
        in_specs=_halo_specs(tm, H, L, D) + [_const_spec(a.shape) for a in consts] + [slab(a) for a in casts],
        out_specs=[row(c) for c in out_cols] + prep_specs + [slab(a) for a in casts],
        out_shape=[jax.ShapeDtypeStruct((L, c), dt) for c, dt in zip(out_cols, dts)]
                  + [jax.ShapeDtypeStruct(s, dt) for s, dt in prep_shapes]
                  + [jax.ShapeDtypeStruct(a.shape, BF16) for a in casts],
        scratch_shapes=[pltpu.VMEM((tm + 2 * H, D), BF16)]
                       + [pltpu.VMEM((tm + 2 * H, CONV_PIECE), F32) for _ in range(2)],
        compiler_params=pltpu.CompilerParams(dimension_semantics=("arbitrary",), vmem_limit_bytes=_vmem_limit(est)),
        name="inproj",
    )(x, x, x, *consts, *casts)


SSD_PACK = 16


LOG2E = 1.4426950408889634


def _pack3(v, head_lanes):
    hi, mid, lo = _split3(jnp.where(head_lanes, v, jnp.zeros_like(v)))
    packed = hi + pltpu.roll(mid, SSD_PACK, axis=1) + pltpu.roll(lo, 2 * SSD_PACK, axis=1)
    return packed.astype(BF16)


def _scan_both(xs_refs, bc_refs, pk_refs, sk_refs, ac_refs, nm_ref, rexp_ref, h_ref):
    Q = SSD_CHUNK
    P = SSD_HEAD_DIM
    E = SSD_HEADS_PER_GROUP
    GW = E * P
    DIRS = (0, 1)
    last = (Q - 1, 0)
    xs16 = [xs_refs[d][...] for d in DIRS]
    bc = [bc_refs[d][...] for d in DIRS]
    src_t = [sk_refs[d][...] for d in DIRS]

    exp2x = [jnp.dot(pk_refs[d][0:2 * Q, :], rexp_ref[d], preferred_element_type=F32) for d in DIRS]
    acum = [ac_refs[d][...] for d in DIRS]
    xw = [xs16[d] * exp2x[d][0:Q].astype(BF16) for d in DIRS]
    ea_x = [exp2x[d][Q:2 * Q] for d in DIRS]
    negmask = [nm_ref[d] for d in DIRS]
    first_head = lax.broadcasted_iota(jnp.int32, (Q, 2 * P), 1) < P

    y_groups = [[], []]
    for g in range(SSD_GROUPS):
        c0 = g * GW
        bm = [bc[d][:, g * SSD_STATE:(g + 1) * SSD_STATE] for d in DIRS]
        cm = [bc[d][:, (SSD_GROUPS + g) * SSD_STATE:(SSD_GROUPS + g + 1) * SSD_STATE] for d in DIRS]
        cb = [lax.dot_general(cm[d], bm[d], (((1,), (1,)), ((), ())), preferred_element_type=F32)
              for d in DIRS]
        st = [lax.dot_general(bm[d], xw[d][:, c0:c0 + GW], (((0,), (0,)), ((), ())), preferred_element_type=F32)
              for d in DIRS]
        h_in = [h_ref[d, g] for d in DIRS]
        y_off = [jnp.dot(cm[d], h_in[d].astype(BF16), preferred_element_type=F32) for d in DIRS]
        y_g = [y_off[d] * ea_x[d][:, c0:c0 + GW] for d in DIRS]
        for d in DIRS:
            h_ref[d, g] = h_in[d] * ea_x[d][last[d]:last[d] + 1, c0:c0 + GW] + st[d]
        pairs = [[], []]
        for hp in range(E // 2):
            for d in DIRS:
                h0 = g * E + hp * 2
                ms = []
                for k in range(2):
                    r = d * SSD_HEADS + h0 + k
                    seg = jnp.broadcast_to(acum[d][:, r:r + 1], (Q, Q)) - src_t[d][r:r + 1, :] + negmask[d]
                    ms.append((cb[d] * jnp.exp2(seg)).astype(BF16))
                m2 = jnp.concatenate(ms, axis=1)
                xp = xs16[d][:, (g * E + hp * 2) * P:(g * E + hp * 2 + 2) * P]
                zero = jnp.zeros_like(xp)
                rhs = jnp.concatenate([jnp.where(first_head, xp, zero), jnp.where(first_head, zero, xp)], axis=0)
                pairs[d].append(jnp.dot(m2, rhs, preferred_element_type=F32))
        for d in DIRS:
            y_groups[d].append(y_g[d] + jnp.concatenate(pairs[d], axis=1))
    return [(jnp.concatenate(y_groups[d], axis=1), xs16[d]) for d in DIRS]


def _ssd_kernel(xsf_ref, bcf_ref, pkf_ref, skf_ref, acf_ref, zf_ref, xsb_ref, bcb_ref, pkb_ref, skb_ref, acb_ref, zb_ref,
                dexp_ref, nw_ref, nm_ref, rexp_ref, wup_ref, wdn_ref,
                lo_ref, hi_ref, wup16_ref, wdn16_ref,
                y_ref, h_ref):
    Q = SSD_CHUNK
    GW = SSD_HEADS_PER_GROUP * SSD_HEAD_DIM
    i = pl.program_id(0)
    nc = pl.num_programs(0)

    wup16_ref[...] = wup_ref[...].astype(BF16)
    wdn16_ref[...] = wdn_ref[...].astype(BF16)

    @pl.when(i == 0)
    def _():
        h_ref[...] = jnp.zeros_like(h_ref)

    (y_f, xs_f), (y_b, xs_b) = _scan_both((xsf_ref, xsb_ref), (bcf_ref, bcb_ref), (pkf_ref, pkb_ref),
                                          (skf_ref, skb_ref), (acf_ref, acb_ref), nm_ref, rexp_ref, h_ref)
    row_f = pl.multiple_of(i * Q, Q)
    row_b = pl.multiple_of((nc - 1 - i) * Q, Q)

    @pl.when(i < nc // 2)
    def _():
        y_ref[pl.ds(row_f, Q), :] = y_f
        y_ref[pl.ds(row_b, Q), :] = y_b

    def finish(y, xs, z_ref, o_ref):
        yy = y + xs.astype(F32) * dexp_ref[...]
        z = z_ref[...]
        yy = yy * (z * jax.nn.sigmoid(z))
        outs = [_rms(yy[:, g * GW:(g + 1) * GW]) for g in range(SSD_GROUPS)]
        o_ref[...] = (jnp.concatenate(outs, axis=1) * nw_ref[...]).astype(o_ref.dtype)

    @pl.when(i >= nc // 2)
    def _():
        finish(y_f + y_ref[pl.ds(row_f, Q), :], xs_f, zf_ref, hi_ref)
        finish(y_b + y_ref[pl.ds(row_b, Q), :], xs_b, zb_ref, lo_ref)


def _ssd(xs, bc, pk, sk, ac, z, dexp, nw, negmask, rexp, w_up, w_down):
    L, DS = xs.shape
    Q = SSD_CHUNK
    nc = L // Q
    hc = nc // 2
    slabs = [pl.BlockSpec((w.shape[0] // nc, w.shape[1]), lambda i: (i, 0)) for w in (w_up, w_down)]
    fwd = lambda i: i
    bwd = lambda i: nc - 1 - i
    fwd_late = lambda i: jnp.maximum(i, hc)
    bwd_late = lambda i: jnp.minimum(nc - 1 - i, hc - 1)
    in_specs = [
        pl.BlockSpec((Q, DS), lambda i: (fwd(i), 0)),
        pl.BlockSpec((Q, bc.shape[1]), lambda i: (fwd(i), 0)),
        pl.BlockSpec((None, None) + pk.shape[2:], lambda i: (fwd(i), 0, 0, 0)),
        pl.BlockSpec((None,) + sk.shape[1:], lambda i: (fwd(i), 0, 0)),
        pl.BlockSpec((Q, V7X_LANES), lambda i: (fwd(i), 0)),
        pl.BlockSpec((Q, DS), lambda i: (fwd_late(i), 0)),
        pl.BlockSpec((Q, DS), lambda i: (bwd(i), 0)),
        pl.BlockSpec((Q, bc.shape[1]), lambda i: (bwd(i), 0)),
        pl.BlockSpec((None, None) + pk.shape[2:], lambda i: (bwd(i), 1, 0, 0)),
        pl.BlockSpec((None,) + sk.shape[1:], lambda i: (bwd(i), 0, 0)),
        pl.BlockSpec((Q, V7X_LANES), lambda i: (bwd(i), 0)),
        pl.BlockSpec((Q, DS), lambda i: (bwd_late(i), 0)),
    ] + [_const_spec(a.shape) for a in (dexp, nw, negmask, rexp)] + slabs
    est = (L * DS * 4 + 2 * SSD_GROUPS * SSD_STATE * DS * 4 + 8 * Q * (DS + bc.shape[1]) * 4
           + rexp.size * 2 + 4 * Q * DS * 4 * 2 + 40 * Q * DS * 4 + 12 * (w_up.size + w_down.size) // nc)
    return pl.pallas_call(
        _ssd_kernel,
        grid=(nc,),
        in_specs=in_specs,
        out_specs=[pl.BlockSpec((Q, DS), lambda i: (bwd_late(i), 0)),
                   pl.BlockSpec((Q, DS), lambda i: (fwd_late(i) - hc, 0))] + slabs,
        out_shape=[jax.ShapeDtypeStruct((L // 2, DS), BF16), jax.ShapeDtypeStruct((L // 2, DS), BF16),
                   jax.ShapeDtypeStruct(w_up.shape, BF16), jax.ShapeDtypeStruct(w_down.shape, BF16)],
        scratch_shapes=[
            pltpu.VMEM((L, DS), F32),
            pltpu.VMEM((2, SSD_GROUPS, SSD_STATE, DS // SSD_GROUPS), F32),
        ],
        compiler_params=pltpu.CompilerParams(dimension_semantics=("arbitrary",), vmem_limit_bytes=_vmem_limit(est)),
        name="ssd",
    )(xs, bc, pk, sk, ac, z, xs, bc, pk, sk, ac, z, dexp, nw, negmask, rexp, w_up, w_down)


def _mixout_kernel(up_ref, um_ref, un_ref, ylo_ref, yhi_ref, x_ref, pw_ref, ps_ref, wo_ref, g_ref, gn_ref,
                   o_ref, on_ref, ext_ref, mix_ref, *, seq):
    i = pl.program_id(0)
    n = pl.num_programs(0)
    tm = um_ref.shape[0]
    cg = um_ref.shape[1] // len(POOL_WINDOWS)
    H = V7X_SUBLANES
    ext_ref[0:H, :] = jnp.where(i > 0, up_ref[...], 0.0)
    ext_ref[H:H + tm, :] = um_ref[...]
    ext_ref[H + tm:2 * H + tm, :] = jnp.where(i < n - 1, un_ref[...], 0.0)
    ys = jnp.where(i < n // 2, ylo_ref[...], yhi_ref[...])
    ds = ys.shape[1]
    dc = o_ref.shape[1] // len(POOL_WINDOWS)
    t = i * tm + lax.broadcasted_iota(jnp.int32, (tm, cg), 0)
    rows = tm + 2 * H

    def ahead(v, k):
        return pltpu.roll(v, (rows - k) % rows, axis=0)

    pooled = []
    for gi, k in enumerate(POOL_WINDOWS):
        cols = slice(gi * cg, (gi + 1) * cg)
        mix_ref[:, gi * dc:(gi + 1) * dc] = jnp.dot(ys, wo_ref[0:ds, gi * dc:(gi + 1) * dc],
                                                    preferred_element_type=F32)
        e = ext_ref[:, cols]
        half = k // 2
        run, length = e, 1
        while length < half:
            run = run + ahead(run, length)
            length *= 2
        before = run[0:tm] if half == H else ahead(run, rows - half)[H:H + tm]
        acc = before + run[H:H + tm]
        cnt = (jnp.minimum(t + (k - k // 2), seq) - jnp.maximum(t - k // 2, 0)).astype(F32)
        mixed = acc / cnt - um_ref[:, cols]
        yp = jnp.dot(mixed.astype(BF16), pw_ref[gi], preferred_element_type=F32) * ps_ref[:, cols]
        pooled.append(yp.astype(BF16))
    ypool = jnp.concatenate(pooled, axis=1)
    mix = mix_ref[...] + jnp.dot(ypool, wo_ref[ds:, :], preferred_element_type=F32)
    h = x_ref[...] + _rms(mix) * g_ref[...]
    o_ref[...] = h
    on_ref[...] = (_rms(h) * gn_ref[...]).astype(on_ref.dtype)


def _mixout(u, y_lo, y_hi, x, pool_w, pool_scale, w_out, g, g_next, *, tm):
    assert all(k % 2 == 0 and k // 2 <= V7X_SUBLANES and (k // 2) & (k // 2 - 1) == 0 for k in POOL_WINDOWS)
    L, DP = u.shape
    D = x.shape[1]
    DS = y_lo.shape[1]
    n = L // tm
    row = lambda n_: pl.BlockSpec((tm, n_), lambda i: (i, 0))
    in_specs = (_halo_specs(tm, V7X_SUBLANES, L, DP)
                + [pl.BlockSpec((tm, DS), lambda i: (jnp.minimum(i, n // 2 - 1), 0)),
                   pl.BlockSpec((tm, DS), lambda i: (jnp.maximum(i - n // 2, 0), 0)), row(D)]
                + [_const_spec(a.shape) for a in (pool_w, pool_scale, w_out, g, g_next)])
    est = (2 * tm * (DP * 4 + 2 * DS * 2 + D * 4 + D * 4 + D * 2) + (w_out.size + pool_w.size) * 2
           + (tm + 2 * V7X_SUBLANES) * DP * 4 + 8 * tm * D * 4)
    return pl.pallas_call(
        functools.partial(_mixout_kernel, seq=L),
        grid=(n,),
        in_specs=in_specs,
        out_specs=[row(D), row(D)],
        out_shape=[jax.ShapeDtypeStruct((L, D), F32), jax.ShapeDtypeStruct((L, D), BF16)],
        scratch_shapes=[pltpu.VMEM((tm + 2 * V7X_SUBLANES, DP), F32), pltpu.VMEM((tm, D), F32)],
        compiler_params=pltpu.CompilerParams(dimension_semantics=("arbitrary",), vmem_limit_bytes=_vmem_limit(est)),
        name="mixout",
    )(u, u, u, y_lo, y_hi, x, pool_w, pool_scale, w_out, g, g_next)


def _ffn_kernel(hp_ref, hm_ref, hx_ref, wg_ref, wv_ref, cw_ref, cb_ref, wd_ref, gpost_ref,
                o_ref, hn_ref, gate_ref):
    i = pl.program_id(0)
    f = pl.program_id(1)
    n = pl.num_programs(0)
    nf = pl.num_programs(1)
    tm = hm_ref.shape[0]
    H = BF16_ROWS

    @pl.when(f == 0)
    def _():
        hn_ref[0:H, :] = jnp.where(i > 0, hp_ref[...], jnp.zeros_like(hp_ref))
        hn_ref[H:H + tm, :] = hm_ref[...]
        hn_ref[H + tm:2 * H + tm, :] = jnp.where(i < n - 1, hx_ref[...], jnp.zeros_like(hx_ref))
        o_ref[...] = jnp.zeros_like(o_ref)

    gate_ref[...] = jnp.dot(hn_ref[...], wg_ref[...], preferred_element_type=F32)
    val = jnp.dot(hm_ref[...], wv_ref[...], preferred_element_type=F32)
    half = FFN_CONV // 2
    gc = cw_ref[0:1, :] * gate_ref[H - half:H - half + tm, :]
    for j in range(1, FFN_CONV):
        gc = gc + cw_ref[j:j + 1, :] * gate_ref[H - half + j:H - half + j + tm, :]
    gc = gc + cb_ref[...]
    act = (jax.nn.gelu(gc, approximate=True) * val).astype(BF16)
    o_ref[...] += jnp.dot(act, wd_ref[...], preferred_element_type=F32)

    @pl.when(f == nf - 1)
    def _():
        o_ref[...] = _rms(o_ref[...]) * gpost_ref[...]


def _ffn(hn, w_up, cw, cb, w_down, gpost, *, tm, tf):
    L, D = hn.shape
    DF = w_down.shape[0]
    nf = DF // tf
    H = BF16_ROWS
    in_specs = _halo_specs(tm, H, L, D) + [
        pl.BlockSpec((D, tf), lambda i, f: (0, f)),
        pl.BlockSpec((D, tf), lambda i, f: (0, nf + f)),
        pl.BlockSpec((FFN_CONV, tf), lambda i, f: (0, f)),
        pl.BlockSpec((1, tf), lambda i, f: (0, f)),
        pl.BlockSpec((tf, D), lambda i, f: (f, 0)),
        pl.BlockSpec(gpost.shape, lambda i, f: (0, 0)),
    ]
    est = (2 * (tm + 2 * H) * D * 2 + 2 * tm * D * 4 + (tm + 2 * H) * D * 2 + (tm + 2 * H) * tf * 4
           + 2 * 3 * D * tf * 2 + 8 * tm * tf * 4 + tm * D * 4)
    return pl.pallas_call(
        _ffn_kernel,
        grid=(L // tm, nf),
        in_specs=in_specs,
        out_specs=pl.BlockSpec((tm, D), lambda i, f: (i, 0)),
        out_shape=jax.ShapeDtypeStruct((L, D), F32),
        scratch_shapes=[
            pltpu.VMEM((tm + 2 * H, D), BF16),
            pltpu.VMEM((tm + 2 * H, tf), F32),
        ],
        compiler_params=pltpu.CompilerParams(dimension_semantics=("arbitrary", "arbitrary"),
                                             vmem_limit_bytes=_vmem_limit(est)),
        name="ffn",
    )(hn, hn, hn, w_up, w_up, cw, cb, w_down, gpost)


def _ple_kernel(h_ref, r_ref, p_ref, gpre_ref, wg_ref, wp_ref, gpost_ref, o_ref):
    h = h_ref[...] + r_ref[...]
    hn = (_rms(h) * gpre_ref[...]).astype(BF16)
    gate = jax.nn.sigmoid(jnp.dot(hn, wg_ref[...], preferred_element_type=F32))
    pe = jnp.dot(p_ref[...].astype(BF16), wp_ref[...], preferred_element_type=F32)
    o_ref[...] = h + _rms(gate * pe) * gpost_ref[...]


def _ple(h, r, p, gpre, w_gate, w_ple, gpost, *, tm):
    L, D = h.shape
    row = lambda n: pl.BlockSpec((tm, n), lambda i: (i, 0))
    est = 2 * tm * (3 * D + p.shape[1]) * 4 + (w_gate.size + w_ple.size) * 2 + 8 * tm * D * 4
    return pl.pallas_call(
        _ple_kernel,
        grid=(L // tm,),
        in_specs=[row(D), row(D), row(p.shape[1])] + [_const_spec(a.shape) for a in (gpre, w_gate, w_ple, gpost)],
        out_specs=row(D),
        out_shape=jax.ShapeDtypeStruct((L, D), F32),
        compiler_params=pltpu.CompilerParams(dimension_semantics=("arbitrary",), vmem_limit_bytes=_vmem_limit(est)),
        name="ple",
    )(h, r, p, gpre, w_gate, w_ple, gpost)


def _ssd_constants():
    Q = SSD_CHUNK
    r = jnp.arange(Q)
    lower = (r[:, None] >= r[None, :])
    tri = jnp.stack([lower, lower.T]).astype(BF16)
    tcat = jnp.concatenate([tri, tri, tri], axis=2)
    negmask = jnp.where(jnp.stack([lower, lower.T]), 0.0, NEG_BIG).astype(F32)
    k = jnp.arange(V7X_LANES)
    rexp = []
    for d in range(2):
        packed = jnp.logical_and(k >= d * SSD_HEADS, k < d * SSD_HEADS + 3 * SSD_PACK)
        head_of_lane = jnp.where(packed, (k - d * SSD_HEADS) % SSD_PACK, -1)
        rexp.append(head_of_lane[:, None] == (jnp.arange(SSD_HEADS * SSD_HEAD_DIM) // SSD_HEAD_DIM)[None, :])
    return tcat, negmask, jnp.stack(rexp).astype(BF16)


def _dir_lanes(v, fill):
    out = jnp.full((1, V7X_LANES), fill, F32)
    return out.at[0, :v.size].set(v.astype(F32).reshape(-1))


def kernel(x, p, mix_norm_pre, mix_norm_post, w_in, ssd_conv_w, ssd_conv_b, ssd_dt_bias, ssd_a_log, ssd_d,
           ssd_norm, pool_w, pool_scale, w_out, ffn_norm_pre, ffn_norm_post, w_ffn_up, ffn_conv_w, ffn_conv_b,
           w_ffn_down, ple_norm_pre, w_ple_gate, w_ple, ple_norm_post):
    B, L, D = x.shape
    depth = w_in.shape[0]
    d_ssd = SSD_HEADS * SSD_HEAD_DIM
    n_bc = 2 * SSD_GROUPS * SSD_STATE
    o_dt = 2 * d_ssd + n_bc
    o_u = o_dt + 2 * SSD_HEADS
    d_pool = w_in.shape[2] - o_u
    tcat, negmask, rexp = _ssd_constants()
    row = lambda v: v.reshape(1, -1).astype(F32)

    outs = []
    for b in range(B):
        h = x[b]
        for i in range(depth):
            xs, bc, z, u, pk, sk, ac, w_out16, w_gate16 = _inproj(
                h, row(mix_norm_pre[i]), _cast_transposed(w_in[i].T, cols=CONV_PIECE), ssd_conv_w[i].astype(F32),
                row(ssd_conv_b[i]), _dir_lanes(ssd_dt_bias[i], 0.0), _dir_lanes(ssd_a_log[i], NEG_BIG), tcat,
                (d_ssd, n_bc, d_ssd, d_pool), w_out[i], w_ple_gate[i], tm=INPROJ_ROWS)
            y_lo, y_hi, w_up16, w_down16 = _ssd(
                xs, bc, pk, sk, ac, z, jnp.repeat(ssd_d[i].astype(F32), SSD_HEAD_DIM).reshape(1, -1),
                row(ssd_norm[i]), negmask, rexp, w_ffn_up[i], w_ffn_down[i])
            h, hn = _mixout(u, y_lo, y_hi, h, pool_w[i].astype(BF16), row(pool_scale[i]), w_out16,
                            row(mix_norm_post[i]), row(ffn_norm_pre[i]), tm=MIXOUT_ROWS)
            r = _ffn(hn, w_up16, ffn_conv_w[i].astype(F32), row(ffn_conv_b[i]), w_down16, row(ffn_norm_post[i]),
                     tm=FFN_ROWS, tf=FFN_COLS)
            h = _ple(h, r, p[i, b], row(ple_norm_pre[i]), w_gate16, w_ple[i].astype(BF16),
                     row(ple_norm_post[i]), tm=PLE_ROWS)
        outs.append(h)
    return jnp.stack(outs)
```

```python
import functools

import jax
import jax.numpy as jnp
from jax import lax
from jax.experimental import pallas as pl
from jax.experimental.pallas import tpu as pltpu

F32 = jnp.float32
BF16 = jnp.bfloat16
EPS = 1e-6

V7X_VMEM_BYTES = 64 * 1024 * 1024
V7X_LANES = 128
V7X_SUBLANES = 8
BF16_ROWS = 16

SSD_HEAD_DIM = 64
SSD_HEADS = 16
SSD_GROUPS = 2
SSD_HEADS_PER_GROUP = SSD_HEADS // SSD_GROUPS
SSD_STATE = 128
SSD_CONV = 5
SSD_CHUNK = 128
POOL_WINDOWS = (2, 4, 8, 16)
FFN_CONV = 3
NEG_BIG = -1e30

INPROJ_ROWS = 512
MIXOUT_ROWS = 512
FFN_ROWS = 1024
FFN_COLS = 512
PLE_ROWS = 512
CONV_PIECE = 256


def _vmem_limit(nbytes):
    return int(min(nbytes, V7X_VMEM_BYTES - 6 * 1024 * 1024))


def _rms(x):
    return x * lax.rsqrt(jnp.mean(x * x, axis=-1, keepdims=True) + EPS)


def _split3(x):
    hi = x.astype(BF16).astype(F32)
    r1 = x - hi
    mid = r1.astype(BF16).astype(F32)
    lo = (r1 - mid).astype(BF16).astype(F32)
    return hi, mid, lo


def _const_spec(shape):
    nd = len(shape)
    return pl.BlockSpec(shape, lambda *_: (0,) * nd, pipeline_mode=pl.Buffered(1))


def _halo_specs(rows, halo, total_rows, ncols):
    hb = rows // halo
    nhb = total_rows // halo
    return [
        pl.BlockSpec((halo, ncols), lambda i, *_: (jnp.maximum(i * hb - 1, 0), 0)),
        pl.BlockSpec((rows, ncols), lambda i, *_: (i, 0)),
        pl.BlockSpec((halo, ncols), lambda i, *_: (jnp.minimum(i * hb + hb, nhb - 1), 0)),
    ]


def _cast_t_kernel(w_ref, o_ref, *, valid_rows):
    row = pl.program_id(0) * w_ref.shape[0] + lax.broadcasted_iota(jnp.int32, w_ref.shape, 0)
    w = jnp.where(row < valid_rows, w_ref[...], 0.0)
    o_ref[...] = w.T.astype(o_ref.dtype)


def _cast_transposed(w_t, *, cols):
    C, R = w_t.shape
    n = pl.cdiv(C, cols)
    return pl.pallas_call(
        functools.partial(_cast_t_kernel, valid_rows=C),
        grid=(n,),
        in_specs=[pl.BlockSpec((cols, R), lambda j: (j, 0))],
        out_specs=pl.BlockSpec((R, cols), lambda j: (0, j)),
        out_shape=jax.ShapeDtypeStruct((R, n * cols), BF16),
        compiler_params=pltpu.CompilerParams(dimension_semantics=("arbitrary",),
                                             vmem_limit_bytes=_vmem_limit(V7X_VMEM_BYTES)),
        name="cast_w_in",
    )(w_t)


def _scan_prep(raw, dtb_ref, alog_ref, t_ref, pk_ref, sk_ref, ac_ref):
    Q = SSD_CHUNK
    lane = lax.broadcasted_iota(jnp.int32, (Q, V7X_LANES), 1)
    head_lanes = [jnp.logical_and(lane >= d * SSD_HEADS, lane < (d + 1) * SSD_HEADS) for d in (0, 1)]
    a = -jnp.exp(alog_ref[...])
    for c in range(raw.shape[0] // Q):
        dt = jax.nn.softplus(raw[c * Q:(c + 1) * Q, :] + dtb_ref[...])
        da3 = jnp.concatenate([v.astype(BF16) for v in _split3(dt * a)], axis=0)
        acum = jnp.where(head_lanes[0], jnp.dot(t_ref[0], da3, preferred_element_type=F32),
                         jnp.dot(t_ref[1], da3, preferred_element_type=F32)) * LOG2E
        tot = jnp.where(head_lanes[0][0:1], acum[Q - 1:Q, :], acum[0:1, :])
        ac_ref[c * Q:(c + 1) * Q, :] = acum
        sk_ref[c] = (acum - jnp.log2(dt)).T[0:2 * SSD_HEADS, :]
        e_a = jnp.exp2(acum)
        dtw = dt * jnp.exp2(tot - acum)
        for d in (0, 1):
            pk_ref[c, d, 0:Q, :] = _pack3(dtw, head_lanes[d])
            pk_ref[c, d, Q:2 * Q, :] = _pack3(e_a, head_lanes[d])


def _inproj_kernel(xp_ref, xm_ref, xn_ref, g_ref, w_ref, cw_ref, cb_ref, dtb_ref, alog_ref, t_ref,
                   wo_ref, wg_ref,
                   xs_ref, bc_ref, z_ref, u_ref, pk_ref, sk_ref, ac_ref, wo16_ref, wg16_ref,
                   hn_ref, *ext_refs):
    i = pl.program_id(0)
    n = pl.num_programs(0)
    tm = xm_ref.shape[0]
    H = BF16_ROWS
    g = g_ref[...]
    hn_ref[0:H, :] = jnp.where(i > 0, _rms(xp_ref[...]) * g, 0.0).astype(BF16)
    hn_ref[H:H + tm, :] = (_rms(xm_ref[...]) * g).astype(BF16)
    hn_ref[H + tm:2 * H + tm, :] = jnp.where(i < n - 1, _rms(xn_ref[...]) * g, 0.0).astype(BF16)
    for src, dst in ((wo_ref, wo16_ref), (wg_ref, wg16_ref)):
        dst[...] = src[...].astype(BF16)

    cs = CONV_PIECE
    nz, nxs, nbc, nu = z_ref.shape[1], xs_ref.shape[1], bc_ref.shape[1], u_ref.shape[1]
    o_dt = nz + nxs + nbc
    n_dt = 2 * SSD_HEADS
    w_end = o_dt + n_dt + nu
    conv_outs = [(xs_ref, c, nz + c, c) for c in range(0, nxs, cs)]
    conv_outs += [(bc_ref, c, nz + nxs + c, nxs + c) for c in range(0, nbc, cs)]

    def project(c0, c1):
        return jnp.dot(hn_ref[H:H + tm, :], w_ref[:, c0:c1], preferred_element_type=F32)

    def z_chunk(c):
        z_ref[:, c:c + cs] = project(c, c + cs)

    split = o_dt + 2 * cs
    tail = [project(o_dt, split)]
    _scan_prep(tail[0][:, 0:V7X_LANES], dtb_ref, alog_ref, t_ref, pk_ref, sk_ref, ac_ref)
    plain = [functools.partial(z_chunk, c) for c in range(0, nz, cs)]
    plain += [lambda: tail.append(project(split, w_end))]

    half = SSD_CONV // 2
    for k, (o_ref, c0, wc, cc) in enumerate(conv_outs):
        ext_ref = ext_refs[k % len(ext_refs)]
        ext_ref[...] = jnp.dot(hn_ref[...], w_ref[:, wc:wc + cs], preferred_element_type=F32)
        if k < len(plain):
            plain[k]()
        acc = cw_ref[0:1, cc:cc + cs] * ext_ref[H - half:H - half + tm, :]
        for j in range(1, SSD_CONV):
            acc = acc + cw_ref[j:j + 1, cc:cc + cs] * ext_ref[H - half + j:H - half + j + tm, :]
        acc = acc + cb_ref[:, cc:cc + cs]
        o_ref[:, c0:c0 + cs] = (acc * jax.nn.sigmoid(acc)).astype(o_ref.dtype)
    for fn in plain[len(conv_outs):]:
        fn()
    u_ref[...] = jnp.concatenate([tail[0][:, n_dt:], tail[1]], axis=1)


def _inproj(x, g, w, cw, cb, dtb, alog, tcat, out_cols, w_out, w_gate, *, tm):
    L, D = x.shape
    H = BF16_ROWS
    Q = SSD_CHUNK
    n = L // tm
    dts = (BF16, BF16, F32, F32)
    row = lambda c: pl.BlockSpec((tm, c), lambda i: (i, 0))
    slab = lambda a: pl.BlockSpec((a.shape[0] // n, a.shape[1]), lambda i: (i, 0))
    casts = (w_out, w_gate)
    consts = (g, w, cw, cb, dtb, alog, tcat)
    prep_shapes = [((L // Q, 2, 2 * Q, V7X_LANES), BF16), ((L // Q, 2 * SSD_HEADS, Q), F32), ((L, V7X_LANES), F32)]
    prep_specs = [pl.BlockSpec((tm // Q,) + s[1:], lambda i, nd=len(s): (i,) + (0,) * (nd - 1))
                  for s, _ in prep_shapes[:2]] + [row(V7X_LANES)]
    est = (2 * (tm + 2 * H) * D * 4 + w.size * 2 + 2 * tm * sum(out_cols) * 4
           + (tm + 2 * H) * D * 2 + 4 * (tm + 2 * H) * CONV_PIECE * 4 + 2 * sum(a.size for a in casts) // n * 6
           + 24 * tm * CONV_PIECE * 4)
    return pl.pallas_call(
        _inproj_kernel,
        grid=(n,),
        in_specs=_halo_specs(tm, H, L, D) + [_const_spec(a.shape) for a in consts] + [slab(a) for a in casts],
        out_specs=[row(c) for c in out_cols] + prep_specs + [slab(a) for a in casts],
        out_shape=[jax.ShapeDtypeStruct((L, c), dt) for c, dt in zip(out_cols, dts)]
                  + [jax.ShapeDtypeStruct(s, dt) for s, dt in prep_shapes]
                  + [jax.ShapeDtypeStruct(a.shape, BF16) for a in casts],
        scratch_shapes=[pltpu.VMEM((tm + 2 * H, D), BF16)]
                       + [pltpu.VMEM((tm + 2 * H, CONV_PIECE), F32) for _ in range(2)],
        compiler_params=pltpu.CompilerParams(dimension_semantics=("arbitrary",), vmem_limit_bytes=_vmem_limit(est)),
        name="inproj",
    )(x, x, x, *consts, *casts)


SSD_PACK = 16


LOG2E = 1.4426950408889634


def _pack3(v, head_lanes):
    hi, mid, lo = _split3(jnp.where(head_lanes, v, jnp.zeros_like(v)))
    packed = hi + pltpu.roll(mid, SSD_PACK, axis=1) + pltpu.roll(lo, 2 * SSD_PACK, axis=1)
    return packed.astype(BF16)


def _scan_both(xs_refs, bc_refs, pk_refs, sk_refs, ac_refs, nm_ref, rexp_ref, h_ref):
    Q = SSD_CHUNK
    P = SSD_HEAD_DIM
    E = SSD_HEADS_PER_GROUP
    GW = E * P
    DIRS = (0, 1)
    last = (Q - 1, 0)
    xs16 = [xs_refs[d][...] for d in DIRS]
    bc = [bc_refs[d][...] for d in DIRS]
    src_t = [sk_refs[d][...] for d in DIRS]

    exp2x = [jnp.dot(pk_refs[d][0:2 * Q, :], rexp_ref[d], preferred_element_type=F32) for d in DIRS]
    acum = [ac_refs[d][...] for d in DIRS]
    xw = [xs16[d] * exp2x[d][0:Q].astype(BF16) for d in DIRS]
    ea_x = [exp2x[d][Q:2 * Q] for d in DIRS]
    negmask = [nm_ref[d] for d in DIRS]
    first_head = lax.broadcasted_iota(jnp.int32, (Q, 2 * P), 1) < P

    y_groups = [[], []]
    for g in range(SSD_GROUPS):
        c0 = g * GW
        bm = [bc[d][:, g * SSD_STATE:(g + 1) * SSD_STATE] for d in DIRS]
        cm = [bc[d][:, (SSD_GROUPS + g) * SSD_STATE:(SSD_GROUPS + g + 1) * SSD_STATE] for d in DIRS]
        cb = [lax.dot_general(cm[d], bm[d], (((1,), (1,)), ((), ())), preferred_element_type=F32)
              for d in DIRS]
        st = [lax.dot_general(bm[d], xw[d][:, c0:c0 + GW], (((0,), (0,)), ((), ())), preferred_element_type=F32)
              for d in DIRS]
        h_in = [h_ref[d, g] for d in DIRS]
        y_off = [jnp.dot(cm[d], h_in[d].astype(BF16), preferred_element_type=F32) for d in DIRS]
        y_g = [y_off[d] * ea_x[d][:, c0:c0 + GW] for d in DIRS]
        for d in DIRS:
            h_ref[d, g] = h_in[d] * ea_x[d][last[d]:last[d] + 1, c0:c0 + GW] + st[d]
        pairs = [[], []]
        for hp in range(E // 2):
            for d in DIRS:
                h0 = g * E + hp * 2
                ms = []
                for k in range(2):
                    r = d * SSD_HEADS + h0 + k
                    seg = jnp.broadcast_to(acum[d][:, r:r + 1], (Q, Q)) - src_t[d][r:r + 1, :] + negmask[d]
                    ms.append((cb[d] * jnp.exp2(seg)).astype(BF16))
                m2 = jnp.concatenate(ms, axis=1)
                xp = xs16[d][:, (g * E + hp * 2) * P:(g * E + hp * 2 + 2) * P]
                zero = jnp.zeros_like(xp)
                rhs = jnp.concatenate([jnp.where(first_head, xp, zero), jnp.where(first_head, zero, xp)], axis=0)
                pairs[d].append(jnp.dot(m2, rhs, preferred_element_type=F32))
        for d in DIRS:
            y_groups[d].append(y_g[d] + jnp.concatenate(pairs[d], axis=1))
    return [(jnp.concatenate(y_groups[d], axis=1), xs16[d]) for d in DIRS]


def _ssd_kernel(xsf_ref, bcf_ref, pkf_ref, skf_ref, acf_ref, zf_ref, xsb_ref, bcb_ref, pkb_ref, skb_ref, acb_ref, zb_ref,
                dexp_ref, nw_ref, nm_ref, rexp_ref, wup_ref,
                lo_ref, hi_ref, wup16_ref,
                y_ref, h_ref):
    Q = SSD_CHUNK
    GW = SSD_HEADS_PER_GROUP * SSD_HEAD_DIM
    i = pl.program_id(0)
    nc = pl.num_programs(0)

    wup16_ref[...] = wup_ref[...].astype(BF16)

    @pl.when(i == 0)
    def _():
        h_ref[...] = jnp.zeros_like(h_ref)

    (y_f, xs_f), (y_b, xs_b) = _scan_both((xsf_ref, xsb_ref), (bcf_ref, bcb_ref), (pkf_ref, pkb_ref),
                                          (skf_ref, skb_ref), (acf_ref, acb_ref), nm_ref, rexp_ref, h_ref)
    row_f = pl.multiple_of(i * Q, Q)
    row_b = pl.multiple_of((nc - 1 - i) * Q, Q)

    @pl.when(i < nc // 2)
    def _():
        y_ref[pl.ds(row_f, Q), :] = y_f
        y_ref[pl.ds(row_b, Q), :] = y_b

    def finish(y, xs, z_ref, o_ref):
        yy = y + xs.astype(F32) * dexp_ref[...]
        z = z_ref[...]
        yy = yy * (z * jax.nn.sigmoid(z))
        outs = [_rms(yy[:, g * GW:(g + 1) * GW]) for g in range(SSD_GROUPS)]
        o_ref[...] = (jnp.concatenate(outs, axis=1) * nw_ref[...]).astype(o_ref.dtype)

    @pl.when(i >= nc // 2)
    def _():
        finish(y_f + y_ref[pl.ds(row_f, Q), :], xs_f, zf_ref, hi_ref)
        finish(y_b + y_ref[pl.ds(row_b, Q), :], xs_b, zb_ref, lo_ref)


def _ssd(xs, bc, pk, sk, ac, z, dexp, nw, negmask, rexp, w_up):
    L, DS = xs.shape
    Q = SSD_CHUNK
    nc = L // Q
    hc = nc // 2
    slab = pl.BlockSpec((w_up.shape[0] // nc, w_up.shape[1]), lambda i: (i, 0))
    fwd = lambda i: i
    bwd = lambda i: nc - 1 - i
    fwd_late = lambda i: jnp.maximum(i, hc)
    bwd_late = lambda i: jnp.minimum(nc - 1 - i, hc - 1)
    in_specs = [
        pl.BlockSpec((Q, DS), lambda i: (fwd(i), 0)),
        pl.BlockSpec((Q, bc.shape[1]), lambda i: (fwd(i), 0)),
        pl.BlockSpec((None, None) + pk.shape[2:], lambda i: (fwd(i), 0, 0, 0)),
        pl.BlockSpec((None,) + sk.shape[1:], lambda i: (fwd(i), 0, 0)),
        pl.BlockSpec((Q, V7X_LANES), lambda i: (fwd(i), 0)),
        pl.BlockSpec((Q, DS), lambda i: (fwd_late(i), 0)),
        pl.BlockSpec((Q, DS), lambda i: (bwd(i), 0)),
        pl.BlockSpec((Q, bc.shape[1]), lambda i: (bwd(i), 0)),
        pl.BlockSpec((None, None) + pk.shape[2:], lambda i: (bwd(i), 1, 0, 0)),
        pl.BlockSpec((None,) + sk.shape[1:], lambda i: (bwd(i), 0, 0)),
        pl.BlockSpec((Q, V7X_LANES), lambda i: (bwd(i), 0)),
        pl.BlockSpec((Q, DS), lambda i: (bwd_late(i), 0)),
    ] + [_const_spec(a.shape) for a in (dexp, nw, negmask, rexp)] + [slab]
    est = (L * DS * 4 + 2 * SSD_GROUPS * SSD_STATE * DS * 4 + 8 * Q * (DS + bc.shape[1]) * 4
           + rexp.size * 2 + 4 * Q * DS * 4 * 2 + 40 * Q * DS * 4 + 12 * w_up.size // nc)
    return pl.pallas_call(
        _ssd_kernel,
        grid=(nc,),
        in_specs=in_specs,
        out_specs=[pl.BlockSpec((Q, DS), lambda i: (bwd_late(i), 0)),
                   pl.BlockSpec((Q, DS), lambda i: (fwd_late(i) - hc, 0)), slab],
        out_shape=[jax.ShapeDtypeStruct((L // 2, DS), BF16), jax.ShapeDtypeStruct((L // 2, DS), BF16),
                   jax.ShapeDtypeStruct(w_up.shape, BF16)],
        scratch_shapes=[
            pltpu.VMEM((L, DS), F32),
            pltpu.VMEM((2, SSD_GROUPS, SSD_STATE, DS // SSD_GROUPS), F32),
        ],
        compiler_params=pltpu.CompilerParams(dimension_semantics=("arbitrary",), vmem_limit_bytes=_vmem_limit(est)),
        name="ssd",
    )(xs, bc, pk, sk, ac, z, xs, bc, pk, sk, ac, z, dexp, nw, negmask, rexp, w_up)


def _mixout_kernel(up_ref, um_ref, un_ref, ylo_ref, yhi_ref, x_ref, pw_ref, ps_ref, wo_ref, g_ref, gn_ref,
                   o_ref, on_ref, ext_ref, mix_ref, *, seq):
    i = pl.program_id(0)
    n = pl.num_programs(0)
    tm = um_ref.shape[0]
    cg = um_ref.shape[1] // len(POOL_WINDOWS)
    H = V7X_SUBLANES
    ext_ref[0:H, :] = jnp.where(i > 0, up_ref[...], 0.0)
    ext_ref[H:H + tm, :] = um_ref[...]
    ext_ref[H + tm:2 * H + tm, :] = jnp.where(i < n - 1, un_ref[...], 0.0)
    ys = jnp.where(i < n // 2, ylo_ref[...], yhi_ref[...])
    ds = ys.shape[1]
    dc = o_ref.shape[1] // len(POOL_WINDOWS)
    t = i * tm + lax.broadcasted_iota(jnp.int32, (tm, cg), 0)
    rows = tm + 2 * H

    def ahead(v, k):
        return pltpu.roll(v, (rows - k) % rows, axis=0)

    pooled = []
    for gi, k in enumerate(POOL_WINDOWS):
        cols = slice(gi * cg, (gi + 1) * cg)
        mix_ref[:, gi * dc:(gi + 1) * dc] = jnp.dot(ys, wo_ref[0:ds, gi * dc:(gi + 1) * dc],
                                                    preferred_element_type=F32)
        e = ext_ref[:, cols]
        half = k // 2
        run, length = e, 1
        while length < half:
            run = run + ahead(run, length)
            length *= 2
        before = run[0:tm] if half == H else ahead(run, rows - half)[H:H + tm]
        acc = before + run[H:H + tm]
        cnt = (jnp.minimum(t + (k - k // 2), seq) - jnp.maximum(t - k // 2, 0)).astype(F32)
        mixed = acc / cnt - um_ref[:, cols]
        yp = jnp.dot(mixed.astype(BF16), pw_ref[gi], preferred_element_type=F32) * ps_ref[:, cols]
        pooled.append(yp.astype(BF16))
    ypool = jnp.concatenate(pooled, axis=1)
    mix = mix_ref[...] + jnp.dot(ypool, wo_ref[ds:, :], preferred_element_type=F32)
    h = x_ref[...] + _rms(mix) * g_ref[...]
    o_ref[...] = h
    on_ref[...] = (_rms(h) * gn_ref[...]).astype(on_ref.dtype)


def _mixout(u, y_lo, y_hi, x, pool_w, pool_scale, w_out, g, g_next, *, tm):
    assert all(k % 2 == 0 and k // 2 <= V7X_SUBLANES and (k // 2) & (k // 2 - 1) == 0 for k in POOL_WINDOWS)
    L, DP = u.shape
    D = x.shape[1]
    DS = y_lo.shape[1]
    n = L // tm
    row = lambda n_: pl.BlockSpec((tm, n_), lambda i: (i, 0))
    in_specs = (_halo_specs(tm, V7X_SUBLANES, L, DP)
                + [pl.BlockSpec((tm, DS), lambda i: (jnp.minimum(i, n // 2 - 1), 0)),
                   pl.BlockSpec((tm, DS), lambda i: (jnp.maximum(i - n // 2, 0), 0)), row(D)]
                + [_const_spec(a.shape) for a in (pool_w, pool_scale, w_out, g, g_next)])
    est = (2 * tm * (DP * 4 + 2 * DS * 2 + D * 4 + D * 4 + D * 2) + (w_out.size + pool_w.size) * 2
           + (tm + 2 * V7X_SUBLANES) * DP * 4 + 8 * tm * D * 4)
    return pl.pallas_call(
        functools.partial(_mixout_kernel, seq=L),
        grid=(n,),
        in_specs=in_specs,
        out_specs=[row(D), row(D)],
        out_shape=[jax.ShapeDtypeStruct((L, D), F32), jax.ShapeDtypeStruct((L, D), BF16)],
        scratch_shapes=[pltpu.VMEM((tm + 2 * V7X_SUBLANES, DP), F32), pltpu.VMEM((tm, D), F32)],
        compiler_params=pltpu.CompilerParams(dimension_semantics=("arbitrary",), vmem_limit_bytes=_vmem_limit(est)),
        name="mixout",
    )(u, u, u, y_lo, y_hi, x, pool_w, pool_scale, w_out, g, g_next)


def _ffn_kernel(hp_ref, hm_ref, hx_ref, wg_ref, wv_ref, cw_ref, cb_ref, wd_ref, gpost_ref,
                o_ref, hn_ref, gate_ref):
    i = pl.program_id(0)
    f = pl.program_id(1)
    n = pl.num_programs(0)
    nf = pl.num_programs(1)
    tm = hm_ref.shape[0]
    H = BF16_ROWS

    @pl.when(f == 0)
    def _():
        hn_ref[0:H, :] = jnp.where(i > 0, hp_ref[...], jnp.zeros_like(hp_ref))
        hn_ref[H:H + tm, :] = hm_ref[...]
        hn_ref[H + tm:2 * H + tm, :] = jnp.where(i < n - 1, hx_ref[...], jnp.zeros_like(hx_ref))
        o_ref[...] = jnp.zeros_like(o_ref)

    gate_ref[...] = jnp.dot(hn_ref[...], wg_ref[...], preferred_element_type=F32)
    wd = wd_ref[...].astype(BF16)
    val = jnp.dot(hm_ref[...], wv_ref[...], preferred_element_type=F32)
    half = FFN_CONV // 2
    gc = cw_ref[0:1, :] * gate_ref[H - half:H - half + tm, :]
    for j in range(1, FFN_CONV):
        gc = gc + cw_ref[j:j + 1, :] * gate_ref[H - half + j:H - half + j + tm, :]
    gc = gc + cb_ref[...]
    act = (jax.nn.gelu(gc, approximate=True) * val).astype(BF16)
    o_ref[...] += jnp.dot(act, wd, preferred_element_type=F32)

    @pl.when(f == nf - 1)
    def _():
        o_ref[...] = _rms(o_ref[...]) * gpost_ref[...]


def _ffn(hn, w_up, cw, cb, w_down, gpost, *, tm, tf):
    L, D = hn.shape
    DF = w_down.shape[0]
    nf = DF // tf
    H = BF16_ROWS
    in_specs = _halo_specs(tm, H, L, D) + [
        pl.BlockSpec((D, tf), lambda i, f: (0, f)),
        pl.BlockSpec((D, tf), lambda i, f: (0, nf + f)),
        pl.BlockSpec((FFN_CONV, tf), lambda i, f: (0, f)),
        pl.BlockSpec((1, tf), lambda i, f: (0, f)),
        pl.BlockSpec((tf, D), lambda i, f: (f, 0)),
        pl.BlockSpec(gpost.shape, lambda i, f: (0, 0)),
    ]
    est = (2 * (tm + 2 * H) * D * 2 + 2 * tm * D * 4 + (tm + 2 * H) * D * 2 + (tm + 2 * H) * tf * 4
           + 2 * 2 * D * tf * 2 + 2 * D * tf * w_down.dtype.itemsize + 8 * tm * tf * 4 + tm * D * 4)
    return pl.pallas_call(
        _ffn_kernel,
        grid=(L // tm, nf),
        in_specs=in_specs,
        out_specs=pl.BlockSpec((tm, D), lambda i, f: (i, 0)),
        out_shape=jax.ShapeDtypeStruct((L, D), F32),
        scratch_shapes=[
            pltpu.VMEM((tm + 2 * H, D), BF16),
            pltpu.VMEM((tm + 2 * H, tf), F32),
        ],
        compiler_params=pltpu.CompilerParams(dimension_semantics=("arbitrary", "arbitrary"),
                                             vmem_limit_bytes=_vmem_limit(est)),
        name="ffn",
    )(hn, hn, hn, w_up, w_up, cw, cb, w_down, gpost)


def _ple_kernel(h_ref, r_ref, p_ref, gpre_ref, wg_ref, wp_ref, gpost_ref, o_ref):
    h = h_ref[...] + r_ref[...]
    hn = (_rms(h) * gpre_ref[...]).astype(BF16)
    gate = jax.nn.sigmoid(jnp.dot(hn, wg_ref[...], preferred_element_type=F32))
    pe = jnp.dot(p_ref[...].astype(BF16), wp_ref[...], preferred_element_type=F32)
    o_ref[...] = h + _rms(gate * pe) * gpost_ref[...]


def _ple(h, r, p, gpre, w_gate, w_ple, gpost, *, tm):
    L, D = h.shape
    row = lambda n: pl.BlockSpec((tm, n), lambda i: (i, 0))
    est = 2 * tm * (3 * D + p.shape[1]) * 4 + (w_gate.size + w_ple.size) * 2 + 8 * tm * D * 4
    return pl.pallas_call(
        _ple_kernel,
        grid=(L // tm,),
        in_specs=[row(D), row(D), row(p.shape[1])] + [_const_spec(a.shape) for a in (gpre, w_gate, w_ple, gpost)],
        out_specs=row(D),
        out_shape=jax.ShapeDtypeStruct((L, D), F32),
        compiler_params=pltpu.CompilerParams(dimension_semantics=("arbitrary",), vmem_limit_bytes=_vmem_limit(est)),
        name="ple",
    )(h, r, p, gpre, w_gate, w_ple, gpost)


def _ssd_constants():
    Q = SSD_CHUNK
    r = jnp.arange(Q)
    lower = (r[:, None] >= r[None, :])
    tri = jnp.stack([lower, lower.T]).astype(BF16)
    tcat = jnp.concatenate([tri, tri, tri], axis=2)
    negmask = jnp.where(jnp.stack([lower, lower.T]), 0.0, NEG_BIG).astype(F32)
    k = jnp.arange(V7X_LANES)
    rexp = []
    for d in range(2):
        packed = jnp.logical_and(k >= d * SSD_HEADS, k < d * SSD_HEADS + 3 * SSD_PACK)
        head_of_lane = jnp.where(packed, (k - d * SSD_HEADS) % SSD_PACK, -1)
        rexp.append(head_of_lane[:, None] == (jnp.arange(SSD_HEADS * SSD_HEAD_DIM) // SSD_HEAD_DIM)[None, :])
    return tcat, negmask, jnp.stack(rexp).astype(BF16)


def _dir_lanes(v, fill):
    out = jnp.full((1, V7X_LANES), fill, F32)
    return out.at[0, :v.size].set(v.astype(F32).reshape(-1))


def kernel(x, p, mix_norm_pre, mix_norm_post, w_in, ssd_conv_w, ssd_conv_b, ssd_dt_bias, ssd_a_log, ssd_d,
           ssd_norm, pool_w, pool_scale, w_out, ffn_norm_pre, ffn_norm_post, w_ffn_up, ffn_conv_w, ffn_conv_b,
           w_ffn_down, ple_norm_pre, w_ple_gate, w_ple, ple_norm_post):
    B, L, D = x.shape
    depth = w_in.shape[0]
    d_ssd = SSD_HEADS * SSD_HEAD_DIM
    n_bc = 2 * SSD_GROUPS * SSD_STATE
    o_dt = 2 * d_ssd + n_bc
    o_u = o_dt + 2 * SSD_HEADS
    d_pool = w_in.shape[2] - o_u
    tcat, negmask, rexp = _ssd_constants()
    row = lambda v: v.reshape(1, -1).astype(F32)

    outs = []
    for b in range(B):
        h = x[b]
        for i in range(depth):
            xs, bc, z, u, pk, sk, ac, w_out16, w_gate16 = _inproj(
                h, row(mix_norm_pre[i]), _cast_transposed(w_in[i].T, cols=CONV_PIECE), ssd_conv_w[i].astype(F32),
                row(ssd_conv_b[i]), _dir_lanes(ssd_dt_bias[i], 0.0), _dir_lanes(ssd_a_log[i], NEG_BIG), tcat,
                (d_ssd, n_bc, d_ssd, d_pool), w_out[i], w_ple_gate[i], tm=INPROJ_ROWS)
            y_lo, y_hi, w_up16 = _ssd(
                xs, bc, pk, sk, ac, z, jnp.repeat(ssd_d[i].astype(F32), SSD_HEAD_DIM).reshape(1, -1),
                row(ssd_norm[i]), negmask, rexp, w_ffn_up[i])
            h, hn = _mixout(u, y_lo, y_hi, h, pool_w[i].astype(BF16), row(pool_scale[i]), w_out16,
                            row(mix_norm_post[i]), row(ffn_norm_pre[i]), tm=MIXOUT_ROWS)
            r = _ffn(hn, w_up16, ffn_conv_w[i].astype(F32), row(ffn_conv_b[i]), w_ffn_down[i], row(ffn_norm_post[i]),
                     tm=FFN_ROWS, tf=FFN_COLS)
            h = _ple(h, r, p[i, b], row(ple_norm_pre[i]), w_gate16, w_ple[i].astype(BF16),
                     row(ple_norm_post[i]), tm=PLE_ROWS)
        outs.append(h)
    return jnp.stack(outs)
```

```python
import functools

import jax
import jax.numpy as jnp
from jax import lax
from jax.experimental import pallas as pl
from jax.experimental.pallas import tpu as pltpu

F32 = jnp.float32
BF16 = jnp.bfloat16
EPS = 1e-6

V7X_VMEM_BYTES = 64 * 1024 * 1024
V7X_LANES = 128
V7X_SUBLANES = 8
BF16_ROWS = 16

SSD_HEAD_DIM = 64
SSD_HEADS = 16
SSD_GROUPS = 2
SSD_HEADS_PER_GROUP = SSD_HEADS // SSD_GROUPS
SSD_STATE = 128
SSD_CONV = 5
SSD_CHUNK = 128
POOL_WINDOWS = (2, 4, 8, 16)
FFN_CONV = 3
NEG_BIG = -1e30

INPROJ_ROWS = 512
MIXOUT_ROWS = 512
FFN_ROWS = 1024
FFN_COLS = 512
PLE_ROWS = 512
CONV_PIECE = 256


def _vmem_limit(nbytes):
    return int(min(nbytes, V7X_VMEM_BYTES - 6 * 1024 * 1024))


def _rms(x):
    return x * lax.rsqrt(jnp.mean(x * x, axis=-1, keepdims=True) + EPS)


def _split3(x):
    hi = x.astype(BF16).astype(F32)
    r1 = x - hi
    mid = r1.astype(BF16).astype(F32)
    lo = (r1 - mid).astype(BF16).astype(F32)
    return hi, mid, lo


def _const_spec(shape):
    nd = len(shape)
    return pl.BlockSpec(shape, lambda *_: (0,) * nd, pipeline_mode=pl.Buffered(1))


def _halo_specs(rows, halo, total_rows, ncols):
    hb = rows // halo
    nhb = total_rows // halo
    return [
        pl.BlockSpec((halo, ncols), lambda i, *_: (jnp.maximum(i * hb - 1, 0), 0)),
        pl.BlockSpec((rows, ncols), lambda i, *_: (i, 0)),
        pl.BlockSpec((halo, ncols), lambda i, *_: (jnp.minimum(i * hb + hb, nhb - 1), 0)),
    ]


def _cast_t_kernel(w_ref, o_ref, *, valid_rows):
    row = pl.program_id(0) * w_ref.shape[0] + lax.broadcasted_iota(jnp.int32, w_ref.shape, 0)
    w = jnp.where(row < valid_rows, w_ref[...], 0.0)
    o_ref[...] = w.T.astype(o_ref.dtype)


def _cast_transposed(w_t, *, cols):
    C, R = w_t.shape
    n = pl.cdiv(C, cols)
    return pl.pallas_call(
        functools.partial(_cast_t_kernel, valid_rows=C),
        grid=(n,),
        in_specs=[pl.BlockSpec((cols, R), lambda j: (j, 0))],
        out_specs=pl.BlockSpec((R, cols), lambda j: (0, j)),
        out_shape=jax.ShapeDtypeStruct((R, n * cols), BF16),
        compiler_params=pltpu.CompilerParams(dimension_semantics=("arbitrary",),
                                             vmem_limit_bytes=_vmem_limit(V7X_VMEM_BYTES)),
        name="cast_w_in",
    )(w_t)


def _scan_prep(raw, dtb_ref, alog_ref, t_ref, pk_ref, sk_ref, ac_ref):
    Q = SSD_CHUNK
    lane = lax.broadcasted_iota(jnp.int32, (Q, V7X_LANES), 1)
    head_lanes = [jnp.logical_and(lane >= d * SSD_HEADS, lane < (d + 1) * SSD_HEADS) for d in (0, 1)]
    a = -jnp.exp(alog_ref[...])
    for c in range(raw.shape[0] // Q):
        dt = jax.nn.softplus(raw[c * Q:(c + 1) * Q, :] + dtb_ref[...])
        da3 = jnp.concatenate([v.astype(BF16) for v in _split3(dt * a)], axis=0)
        acum = jnp.where(head_lanes[0], jnp.dot(t_ref[0], da3, preferred_element_type=F32),
                         jnp.dot(t_ref[1], da3, preferred_element_type=F32)) * LOG2E
        tot = jnp.where(head_lanes[0][0:1], acum[Q - 1:Q, :], acum[0:1, :])
        ac_ref[c * Q:(c + 1) * Q, :] = acum
        sk_ref[c] = (acum - jnp.log2(dt)).T[0:2 * SSD_HEADS, :]
        e_a = jnp.exp2(acum)
        dtw = dt * jnp.exp2(tot - acum)
        for d in (0, 1):
            pk_ref[c, d, 0:Q, :] = _pack3(dtw, head_lanes[d])
            pk_ref[c, d, Q:2 * Q, :] = _pack3(e_a, head_lanes[d])


def _inproj_kernel(xp_ref, xm_ref, xn_ref, g_ref, w_ref, cw_ref, cb_ref, dtb_ref, alog_ref, t_ref,
                   wo_ref, wg_ref,
                   xs_ref, bc_ref, z_ref, u_ref, pk_ref, sk_ref, ac_ref, wo16_ref, wg16_ref,
                   hn_ref, *ext_refs):
    i = pl.program_id(0)
    n = pl.num_programs(0)
    tm = xm_ref.shape[0]
    H = BF16_ROWS
    g = g_ref[...]
    hn_ref[0:H, :] = jnp.where(i > 0, _rms(xp_ref[...]) * g, 0.0).astype(BF16)
    hn_ref[H:H + tm, :] = (_rms(xm_ref[...]) * g).astype(BF16)
    hn_ref[H + tm:2 * H + tm, :] = jnp.where(i < n - 1, _rms(xn_ref[...]) * g, 0.0).astype(BF16)
    for src, dst in ((wo_ref, wo16_ref), (wg_ref, wg16_ref)):
        dst[...] = src[...].astype(BF16)

    cs = CONV_PIECE
    nz, nxs, nbc, nu = z_ref.shape[1], xs_ref.shape[1], bc_ref.shape[1], u_ref.shape[1]
    o_dt = nz + nxs + nbc
    n_dt = 2 * SSD_HEADS
    w_end = o_dt + n_dt + nu
    conv_outs = [(xs_ref, c, nz + c, c) for c in range(0, nxs, cs)]
    conv_outs += [(bc_ref, c, nz + nxs + c, nxs + c) for c in range(0, nbc, cs)]

    def project(c0, c1):
        return jnp.dot(hn_ref[H:H + tm, :], w_ref[:, c0:c1], preferred_element_type=F32)

    def z_chunk(c):
        z_ref[:, c:c + cs] = project(c, c + cs)

    split = o_dt + 2 * cs
    tail = [project(o_dt, split)]
    _scan_prep(tail[0][:, 0:V7X_LANES], dtb_ref, alog_ref, t_ref, pk_ref, sk_ref, ac_ref)
    plain = [functools.partial(z_chunk, c) for c in range(0, nz, cs)]
    plain += [lambda: tail.append(project(split, w_end))]

    half = SSD_CONV // 2
    for k, (o_ref, c0, wc, cc) in enumerate(conv_outs):
        ext_ref = ext_refs[k % len(ext_refs)]
        ext_ref[...] = jnp.dot(hn_ref[...], w_ref[:, wc:wc + cs], preferred_element_type=F32)
        if k < len(plain):
            plain[k]()
        acc = cw_ref[0:1, cc:cc + cs] * ext_ref[H - half:H - half + tm, :]
        for j in range(1, SSD_CONV):
            acc = acc + cw_ref[j:j + 1, cc:cc + cs] * ext_ref[H - half + j:H - half + j + tm, :]
        acc = acc + cb_ref[:, cc:cc + cs]
        o_ref[:, c0:c0 + cs] = (acc * jax.nn.sigmoid(acc)).astype(o_ref.dtype)
    for fn in plain[len(conv_outs):]:
        fn()
    u_ref[...] = jnp.concatenate([tail[0][:, n_dt:], tail[1]], axis=1)


def _inproj(x, g, w, cw, cb, dtb, alog, tcat, out_cols, w_out, w_gate, *, tm):
    L, D = x.shape
    H = BF16_ROWS
    Q = SSD_CHUNK
    n = L // tm
    dts = (BF16, BF16, F32, F32)
    row = lambda c: pl.BlockSpec((tm, c), lambda i: (i, 0))
    slab = lambda a: pl.BlockSpec((a.shape[0] // n, a.shape[1]), lambda i: (i, 0))
    casts = (w_out, w_gate)
    consts = (g, w, cw, cb, dtb, alog, tcat)
    prep_shapes = [((L // Q, 2, 2 * Q, V7X_LANES), BF16), ((L // Q, 2 * SSD_HEADS, Q), F32), ((L, V7X_LANES), F32)]
    prep_specs = [pl.BlockSpec((tm // Q,) + s[1:], lambda i, nd=len(s): (i,) + (0,) * (nd - 1))
                  for s, _ in prep_shapes[:2]] + [row(V7X_LANES)]
    est = (2 * (tm + 2 * H) * D * 4 + w.size * 2 + 2 * tm * sum(out_cols) * 4
           + (tm + 2 * H) * D * 2 + 4 * (tm + 2 * H) * CONV_PIECE * 4 + 2 * sum(a.size for a in casts) // n * 6
           + 24 * tm * CONV_PIECE * 4)
    return pl.pallas_call(
        _inproj_kernel,
        grid=(n,),
        in_specs=_halo_specs(tm, H, L, D) + [_const_spec(a.shape) for a in consts] + [slab(a) for a in casts],
        out_specs=[row(c) for c in out_cols] + prep_specs + [slab(a) for a in casts],
        out_shape=[jax.ShapeDtypeStruct((L, c), dt) for c, dt in zip(out_cols, dts)]
                  + [jax.ShapeDtypeStruct(s, dt) for s, dt in prep_shapes]
                  + [jax.ShapeDtypeStruct(a.shape, BF16) for a in casts],
        scratch_shapes=[pltpu.VMEM((tm + 2 * H, D), BF16)]
                       + [pltpu.VMEM((tm + 2 * H, CONV_PIECE), F32) for _ in range(2)],
        compiler_params=pltpu.CompilerParams(dimension_semantics=("arbitrary",), vmem_limit_bytes=_vmem_limit(est)),
        name="inproj",
    )(x, x, x, *consts, *casts)


SSD_PACK = 16


LOG2E = 1.4426950408889634


def _pack3(v, head_lanes):
    hi, mid, lo = _split3(jnp.where(head_lanes, v, jnp.zeros_like(v)))
    packed = hi + pltpu.roll(mid, SSD_PACK, axis=1) + pltpu.roll(lo, 2 * SSD_PACK, axis=1)
    return packed.astype(BF16)


def _scan_both(xs_refs, bc_refs, pk_refs, sk_refs, ac_refs, nm_ref, rexp_ref, h_ref):
    Q = SSD_CHUNK
    P = SSD_HEAD_DIM
    E = SSD_HEADS_PER_GROUP
    GW = E * P
    DIRS = (0, 1)
    last = (Q - 1, 0)
    xs16 = [xs_refs[d][...] for d in DIRS]
    bc = [bc_refs[d][...] for d in DIRS]
    src_t = [sk_refs[d][...] for d in DIRS]

    exp2x = [jnp.dot(pk_refs[d][0:2 * Q, :], rexp_ref[d], preferred_element_type=F32) for d in DIRS]
    acum = [ac_refs[d][...] for d in DIRS]
    xw = [xs16[d] * exp2x[d][0:Q].astype(BF16) for d in DIRS]
    ea_x = [exp2x[d][Q:2 * Q] for d in DIRS]
    negmask = [nm_ref[d] for d in DIRS]
    first_head = lax.broadcasted_iota(jnp.int32, (Q, 2 * P), 1) < P

    y_groups = [[], []]
    for g in range(SSD_GROUPS):
        c0 = g * GW
        bm = [bc[d][:, g * SSD_STATE:(g + 1) * SSD_STATE] for d in DIRS]
        cm = [bc[d][:, (SSD_GROUPS + g) * SSD_STATE:(SSD_GROUPS + g + 1) * SSD_STATE] for d in DIRS]
        cb = [lax.dot_general(cm[d], bm[d], (((1,), (1,)), ((), ())), preferred_element_type=F32)
              for d in DIRS]
        st = [lax.dot_general(bm[d], xw[d][:, c0:c0 + GW], (((0,), (0,)), ((), ())), preferred_element_type=F32)
              for d in DIRS]
        h_in = [h_ref[d, g] for d in DIRS]
        y_off = [jnp.dot(cm[d], h_in[d].astype(BF16), preferred_element_type=F32) for d in DIRS]
        y_g = [y_off[d] * ea_x[d][:, c0:c0 + GW] for d in DIRS]
        for d in DIRS:
            h_ref[d, g] = h_in[d] * ea_x[d][last[d]:last[d] + 1, c0:c0 + GW] + st[d]
        pairs = [[], []]
        for hp in range(E // 2):
            for d in DIRS:
                h0 = g * E + hp * 2
                ms = []
                for k in range(2):
                    r = d * SSD_HEADS + h0 + k
                    seg = jnp.broadcast_to(acum[d][:, r:r + 1], (Q, Q)) - src_t[d][r:r + 1, :] + negmask[d]
                    ms.append((cb[d] * jnp.exp2(seg)).astype(BF16))
                m2 = jnp.concatenate(ms, axis=1)
                xp = xs16[d][:, (g * E + hp * 2) * P:(g * E + hp * 2 + 2) * P]
                zero = jnp.zeros_like(xp)
                rhs = jnp.concatenate([jnp.where(first_head, xp, zero), jnp.where(first_head, zero, xp)], axis=0)
                pairs[d].append(jnp.dot(m2, rhs, preferred_element_type=F32))
        for d in DIRS:
            y_groups[d].append(y_g[d] + jnp.concatenate(pairs[d], axis=1))
    return [(jnp.concatenate(y_groups[d], axis=1), xs16[d]) for d in DIRS]


def _ssd_kernel(xsf_ref, bcf_ref, pkf_ref, skf_ref, acf_ref, zf_ref, xsb_ref, bcb_ref, pkb_ref, skb_ref, acb_ref, zb_ref,
                dexp_ref, nw_ref, nm_ref, rexp_ref, wup_ref,
                lo_ref, hi_ref, wup16_ref,
                y_ref, h_ref):
    Q = SSD_CHUNK
    GW = SSD_HEADS_PER_GROUP * SSD_HEAD_DIM
    i = pl.program_id(0)
    nc = pl.num_programs(0)

    wup16_ref[...] = wup_ref[...].astype(BF16)

    @pl.when(i == 0)
    def _():
        h_ref[...] = jnp.zeros_like(h_ref)

    (y_f, xs_f), (y_b, xs_b) = _scan_both((xsf_ref, xsb_ref), (bcf_ref, bcb_ref), (pkf_ref, pkb_ref),
                                          (skf_ref, skb_ref), (acf_ref, acb_ref), nm_ref, rexp_ref, h_ref)
    row_f = pl.multiple_of(i * Q, Q)
    row_b = pl.multiple_of((nc - 1 - i) * Q, Q)

    @pl.when(i < nc // 2)
    def _():
        y_ref[pl.ds(row_f, Q), :] = y_f
        y_ref[pl.ds(row_b, Q), :] = y_b

    def finish(y, xs, z_ref, o_ref):
        yy = y + xs.astype(F32) * dexp_ref[...]
        z = z_ref[...]
        yy = yy * (z * jax.nn.sigmoid(z))
        outs = [_rms(yy[:, g * GW:(g + 1) * GW]) for g in range(SSD_GROUPS)]
        o_ref[...] = (jnp.concatenate(outs, axis=1) * nw_ref[...]).astype(o_ref.dtype)

    @pl.when(i >= nc // 2)
    def _():
        finish(y_f + y_ref[pl.ds(row_f, Q), :], xs_f, zf_ref, hi_ref)
        finish(y_b + y_ref[pl.ds(row_b, Q), :], xs_b, zb_ref, lo_ref)


def _ssd(xs, bc, pk, sk, ac, z, dexp, nw, negmask, rexp, w_up):
    L, DS = xs.shape
    Q = SSD_CHUNK
    nc = L // Q
    hc = nc // 2
    slab = pl.BlockSpec((w_up.shape[0] // nc, w_up.shape[1]), lambda i: (i, 0))
    fwd = lambda i: i
    bwd = lambda i: nc - 1 - i
    fwd_late = lambda i: jnp.maximum(i, hc)
    bwd_late = lambda i: jnp.minimum(nc - 1 - i, hc - 1)
    in_specs = [
        pl.BlockSpec((Q, DS), lambda i: (fwd(i), 0)),
        pl.BlockSpec((Q, bc.shape[1]), lambda i: (fwd(i), 0)),
        pl.BlockSpec((None, None) + pk.shape[2:], lambda i: (fwd(i), 0, 0, 0)),
        pl.BlockSpec((None,) + sk.shape[1:], lambda i: (fwd(i), 0, 0)),
        pl.BlockSpec((Q, V7X_LANES), lambda i: (fwd(i), 0)),
        pl.BlockSpec((Q, DS), lambda i: (fwd_late(i), 0)),
        pl.BlockSpec((Q, DS), lambda i: (bwd(i), 0)),
        pl.BlockSpec((Q, bc.shape[1]), lambda i: (bwd(i), 0)),
        pl.BlockSpec((None, None) + pk.shape[2:], lambda i: (bwd(i), 1, 0, 0)),
        pl.BlockSpec((None,) + sk.shape[1:], lambda i: (bwd(i), 0, 0)),
        pl.BlockSpec((Q, V7X_LANES), lambda i: (bwd(i), 0)),
        pl.BlockSpec((Q, DS), lambda i: (bwd_late(i), 0)),
    ] + [_const_spec(a.shape) for a in (dexp, nw, negmask, rexp)] + [slab]
    est = (L * DS * 4 + 2 * SSD_GROUPS * SSD_STATE * DS * 4 + 8 * Q * (DS + bc.shape[1]) * 4
           + rexp.size * 2 + 4 * Q * DS * 4 * 2 + 40 * Q * DS * 4 + 12 * w_up.size // nc)
    return pl.pallas_call(
        _ssd_kernel,
        grid=(nc,),
        in_specs=in_specs,
        out_specs=[pl.BlockSpec((Q, DS), lambda i: (bwd_late(i), 0)),
                   pl.BlockSpec((Q, DS), lambda i: (fwd_late(i) - hc, 0)), slab],
        out_shape=[jax.ShapeDtypeStruct((L // 2, DS), BF16), jax.ShapeDtypeStruct((L // 2, DS), BF16),
                   jax.ShapeDtypeStruct(w_up.shape, BF16)],
        scratch_shapes=[
            pltpu.VMEM((L, DS), F32),
            pltpu.VMEM((2, SSD_GROUPS, SSD_STATE, DS // SSD_GROUPS), F32),
        ],
        compiler_params=pltpu.CompilerParams(dimension_semantics=("arbitrary",), vmem_limit_bytes=_vmem_limit(est)),
        name="ssd",
    )(xs, bc, pk, sk, ac, z, xs, bc, pk, sk, ac, z, dexp, nw, negmask, rexp, w_up)


def _mixout_kernel(up_ref, um_ref, un_ref, ylo_ref, yhi_ref, x_ref, pw_ref, ps_ref, wo_ref, g_ref, gn_ref,
                   o_ref, on_ref, ext_ref, mix_ref, *, seq):
    i = pl.program_id(0)
    n = pl.num_programs(0)
    tm = um_ref.shape[0]
    cg = um_ref.shape[1] // len(POOL_WINDOWS)
    H = V7X_SUBLANES
    ext_ref[0:H, :] = jnp.where(i > 0, up_ref[...], 0.0)
    ext_ref[H:H + tm, :] = um_ref[...]
    ext_ref[H + tm:2 * H + tm, :] = jnp.where(i < n - 1, un_ref[...], 0.0)
    ys = jnp.where(i < n // 2, ylo_ref[...], yhi_ref[...])
    ds = ys.shape[1]
    dc = o_ref.shape[1] // len(POOL_WINDOWS)
    t = i * tm + lax.broadcasted_iota(jnp.int32, (tm, cg), 0)
    rows = tm + 2 * H

    def ahead(v, k):
        return pltpu.roll(v, (rows - k) % rows, axis=0)

    pooled = []
    for gi, k in enumerate(POOL_WINDOWS):
        cols = slice(gi * cg, (gi + 1) * cg)
        mix_ref[:, gi * dc:(gi + 1) * dc] = jnp.dot(ys, wo_ref[0:ds, gi * dc:(gi + 1) * dc],
                                                    preferred_element_type=F32)
        e = ext_ref[:, cols]
        half = k // 2
        run, length = e, 1
        while length < half:
            run = run + ahead(run, length)
            length *= 2
        before = run[0:tm] if half == H else ahead(run, rows - half)[H:H + tm]
        acc = before + run[H:H + tm]
        cnt = (jnp.minimum(t + (k - k // 2), seq) - jnp.maximum(t - k // 2, 0)).astype(F32)
        mixed = acc / cnt - um_ref[:, cols]
        yp = jnp.dot(mixed.astype(BF16), pw_ref[gi], preferred_element_type=F32) * ps_ref[:, cols]
        pooled.append(yp.astype(BF16))
    ypool = jnp.concatenate(pooled, axis=1)
    mix = mix_ref[...] + jnp.dot(ypool, wo_ref[ds:, :], preferred_element_type=F32)
    h = x_ref[...] + _rms(mix) * g_ref[...]
    o_ref[...] = h
    on_ref[...] = (_rms(h) * gn_ref[...]).astype(on_ref.dtype)


def _mixout(u, y_lo, y_hi, x, pool_w, pool_scale, w_out, g, g_next, *, tm):
    assert all(k % 2 == 0 and k // 2 <= V7X_SUBLANES and (k // 2) & (k // 2 - 1) == 0 for k in POOL_WINDOWS)
    L, DP = u.shape
    D = x.shape[1]
    DS = y_lo.shape[1]
    n = L // tm
    row = lambda n_: pl.BlockSpec((tm, n_), lambda i: (i, 0))
    in_specs = (_halo_specs(tm, V7X_SUBLANES, L, DP)
                + [pl.BlockSpec((tm, DS), lambda i: (jnp.minimum(i, n // 2 - 1), 0)),
                   pl.BlockSpec((tm, DS), lambda i: (jnp.maximum(i - n // 2, 0), 0)), row(D)]
                + [_const_spec(a.shape) for a in (pool_w, pool_scale, w_out, g, g_next)])
    est = (2 * tm * (DP * 4 + 2 * DS * 2 + D * 4 + D * 4 + D * 2) + (w_out.size + pool_w.size) * 2
           + (tm + 2 * V7X_SUBLANES) * DP * 4 + 8 * tm * D * 4)
    return pl.pallas_call(
        functools.partial(_mixout_kernel, seq=L),
        grid=(n,),
        in_specs=in_specs,
        out_specs=[row(D), row(D)],
        out_shape=[jax.ShapeDtypeStruct((L, D), F32), jax.ShapeDtypeStruct((L, D), BF16)],
        scratch_shapes=[pltpu.VMEM((tm + 2 * V7X_SUBLANES, DP), F32), pltpu.VMEM((tm, D), F32)],
        compiler_params=pltpu.CompilerParams(dimension_semantics=("arbitrary",), vmem_limit_bytes=_vmem_limit(est)),
        name="mixout",
    )(u, u, u, y_lo, y_hi, x, pool_w, pool_scale, w_out, g, g_next)


def _ffn_kernel(hp_ref, hm_ref, hx_ref, wg_ref, wv_ref, cw_ref, cb_ref, wd_ref, gpost_ref,
                o_ref, hn_ref, gate_ref, val_ref, act_ref):
    i = pl.program_id(0)
    f = pl.program_id(1)
    n = pl.num_programs(0)
    nf = pl.num_programs(1)
    tm = hm_ref.shape[0]
    H = BF16_ROWS

    @pl.when(f == 0)
    def _():
        hn_ref[0:H, :] = jnp.where(i > 0, hp_ref[...], jnp.zeros_like(hp_ref))
        hn_ref[H:H + tm, :] = hm_ref[...]
        hn_ref[H + tm:2 * H + tm, :] = jnp.where(i < n - 1, hx_ref[...], jnp.zeros_like(hx_ref))
        o_ref[...] = jnp.zeros_like(o_ref)

    half = FFN_CONV // 2
    th = tm // 2
    ra = th + 2 * H

    def activate(lo):
        gc = cw_ref[0:1, :] * gate_ref[H - half + lo:H - half + lo + th, :]
        for j in range(1, FFN_CONV):
            gc = gc + cw_ref[j:j + 1, :] * gate_ref[H - half + j + lo:H - half + j + lo + th, :]
        gc = gc + cb_ref[...]
        act_ref[lo:lo + th, :] = (jax.nn.gelu(gc, approximate=True) * val_ref[lo:lo + th, :]).astype(BF16)

    gate_ref[0:ra, :] = jnp.dot(hn_ref[0:ra, :], wg_ref[...], preferred_element_type=F32)
    wd = wd_ref[...].astype(BF16)
    val_ref[0:th, :] = jnp.dot(hm_ref[0:th, :], wv_ref[...], preferred_element_type=F32)
    gate_ref[ra:, :] = jnp.dot(hn_ref[ra:, :], wg_ref[...], preferred_element_type=F32)
    activate(0)
    val_ref[th:, :] = jnp.dot(hm_ref[th:, :], wv_ref[...], preferred_element_type=F32)
    o_ref[0:th, :] += jnp.dot(act_ref[0:th, :], wd, preferred_element_type=F32)
    activate(th)
    o_ref[th:, :] += jnp.dot(act_ref[th:, :], wd, preferred_element_type=F32)

    @pl.when(f == nf - 1)
    def _():
        o_ref[...] = _rms(o_ref[...]) * gpost_ref[...]


def _ffn(hn, w_up, cw, cb, w_down, gpost, *, tm, tf):
    L, D = hn.shape
    DF = w_down.shape[0]
    nf = DF // tf
    H = BF16_ROWS
    in_specs = _halo_specs(tm, H, L, D) + [
        pl.BlockSpec((D, tf), lambda i, f: (0, f)),
        pl.BlockSpec((D, tf), lambda i, f: (0, nf + f)),
        pl.BlockSpec((FFN_CONV, tf), lambda i, f: (0, f)),
        pl.BlockSpec((1, tf), lambda i, f: (0, f)),
        pl.BlockSpec((tf, D), lambda i, f: (f, 0)),
        pl.BlockSpec(gpost.shape, lambda i, f: (0, 0)),
    ]
    est = (2 * (tm + 2 * H) * D * 2 + 2 * tm * D * 4 + (tm + 2 * H) * D * 2 + (tm + 2 * H) * tf * 4
           + 2 * 2 * D * tf * 2 + 2 * D * tf * w_down.dtype.itemsize + 8 * tm * tf * 4 + tm * D * 4)
    return pl.pallas_call(
        _ffn_kernel,
        grid=(L // tm, nf),
        in_specs=in_specs,
        out_specs=pl.BlockSpec((tm, D), lambda i, f: (i, 0)),
        out_shape=jax.ShapeDtypeStruct((L, D), F32),
        scratch_shapes=[
            pltpu.VMEM((tm + 2 * H, D), BF16),
            pltpu.VMEM((tm + 2 * H, tf), F32),
            pltpu.VMEM((tm, tf), F32),
            pltpu.VMEM((tm, tf), BF16),
        ],
        compiler_params=pltpu.CompilerParams(dimension_semantics=("arbitrary", "arbitrary"),
                                             vmem_limit_bytes=_vmem_limit(est)),
        name="ffn",
    )(hn, hn, hn, w_up, w_up, cw, cb, w_down, gpost)


def _ple_kernel(h_ref, r_ref, p_ref, gpre_ref, wg_ref, wp_ref, gpost_ref, o_ref):
    h = h_ref[...] + r_ref[...]
    hn = (_rms(h) * gpre_ref[...]).astype(BF16)
    gate = jax.nn.sigmoid(jnp.dot(hn, wg_ref[...], preferred_element_type=F32))
    pe = jnp.dot(p_ref[...].astype(BF16), wp_ref[...], preferred_element_type=F32)
    o_ref[...] = h + _rms(gate * pe) * gpost_ref[...]


def _ple(h, r, p, gpre, w_gate, w_ple, gpost, *, tm):
    L, D = h.shape
    row = lambda n: pl.BlockSpec((tm, n), lambda i: (i, 0))
    est = 2 * tm * (3 * D + p.shape[1]) * 4 + (w_gate.size + w_ple.size) * 2 + 8 * tm * D * 4
    return pl.pallas_call(
        _ple_kernel,
        grid=(L // tm,),
        in_specs=[row(D), row(D), row(p.shape[1])] + [_const_spec(a.shape) for a in (gpre, w_gate, w_ple, gpost)],
        out_specs=row(D),
        out_shape=jax.ShapeDtypeStruct((L, D), F32),
        compiler_params=pltpu.CompilerParams(dimension_semantics=("arbitrary",), vmem_limit_bytes=_vmem_limit(est)),
        name="ple",
    )(h, r, p, gpre, w_gate, w_ple, gpost)


def _ssd_constants():
    Q = SSD_CHUNK
    r = jnp.arange(Q)
    lower = (r[:, None] >= r[None, :])
    tri = jnp.stack([lower, lower.T]).astype(BF16)
    tcat = jnp.concatenate([tri, tri, tri], axis=2)
    negmask = jnp.where(jnp.stack([lower, lower.T]), 0.0, NEG_BIG).astype(F32)
    k = jnp.arange(V7X_LANES)
    rexp = []
    for d in range(2):
        packed = jnp.logical_and(k >= d * SSD_HEADS, k < d * SSD_HEADS + 3 * SSD_PACK)
        head_of_lane = jnp.where(packed, (k - d * SSD_HEADS) % SSD_PACK, -1)
        rexp.append(head_of_lane[:, None] == (jnp.arange(SSD_HEADS * SSD_HEAD_DIM) // SSD_HEAD_DIM)[None, :])
    return tcat, negmask, jnp.stack(rexp).astype(BF16)


def _dir_lanes(v, fill):
    out = jnp.full((1, V7X_LANES), fill, F32)
    return out.at[0, :v.size].set(v.astype(F32).reshape(-1))


def kernel(x, p, mix_norm_pre, mix_norm_post, w_in, ssd_conv_w, ssd_conv_b, ssd_dt_bias, ssd_a_log, ssd_d,
           ssd_norm, pool_w, pool_scale, w_out, ffn_norm_pre, ffn_norm_post, w_ffn_up, ffn_conv_w, ffn_conv_b,
           w_ffn_down, ple_norm_pre, w_ple_gate, w_ple, ple_norm_post):
    B, L, D = x.shape
    depth = w_in.shape[0]
    d_ssd = SSD_HEADS * SSD_HEAD_DIM
    n_bc = 2 * SSD_GROUPS * SSD_STATE
    o_dt = 2 * d_ssd + n_bc
    o_u = o_dt + 2 * SSD_HEADS
    d_pool = w_in.shape[2] - o_u
    tcat, negmask, rexp = _ssd_constants()
    row = lambda v: v.reshape(1, -1).astype(F32)

    outs = []
    for b in range(B):
        h = x[b]
        for i in range(depth):
            xs, bc, z, u, pk, sk, ac, w_out16, w_gate16 = _inproj(
                h, row(mix_norm_pre[i]), _cast_transposed(w_in[i].T, cols=CONV_PIECE), ssd_conv_w[i].astype(F32),
                row(ssd_conv_b[i]), _dir_lanes(ssd_dt_bias[i], 0.0), _dir_lanes(ssd_a_log[i], NEG_BIG), tcat,
                (d_ssd, n_bc, d_ssd, d_pool), w_out[i], w_ple_gate[i], tm=INPROJ_ROWS)
            y_lo, y_hi, w_up16 = _ssd(
                xs, bc, pk, sk, ac, z, jnp.repeat(ssd_d[i].astype(F32), SSD_HEAD_DIM).reshape(1, -1),
                row(ssd_norm[i]), negmask, rexp, w_ffn_up[i])
            h, hn = _mixout(u, y_lo, y_hi, h, pool_w[i].astype(BF16), row(pool_scale[i]), w_out16,
                            row(mix_norm_post[i]), row(ffn_norm_pre[i]), tm=MIXOUT_ROWS)
            r = _ffn(hn, w_up16, ffn_conv_w[i].astype(F32), row(ffn_conv_b[i]), w_ffn_down[i], row(ffn_norm_post[i]),
                     tm=FFN_ROWS, tf=FFN_COLS)
            h = _ple(h, r, p[i, b], row(ple_norm_pre[i]), w_gate16, w_ple[i].astype(BF16),
                     row(ple_norm_post[i]), tm=PLE_ROWS)
        outs.append(h)
    return jnp.stack(outs)
```

```python
import functools

import jax
import jax.numpy as jnp
from jax import lax
from jax.experimental import pallas as pl
from jax.experimental.pallas import tpu as pltpu

F32 = jnp.float32
BF16 = jnp.bfloat16
EPS = 1e-6

V7X_VMEM_BYTES = 64 * 1024 * 1024
V7X_LANES = 128
V7X_SUBLANES = 8
BF16_ROWS = 16

SSD_HEAD_DIM = 64
SSD_HEADS = 16
SSD_GROUPS = 2
SSD_HEADS_PER_GROUP = SSD_HEADS // SSD_GROUPS
SSD_STATE = 128
SSD_CONV = 5
SSD_CHUNK = 128
POOL_WINDOWS = (2, 4, 8, 16)
FFN_CONV = 3
NEG_BIG = -1e30

INPROJ_ROWS = 512
MIXOUT_ROWS = 512
FFN_ROWS = 1024
FFN_COLS = 512
FFN_NORM_ROWS = 64
PLE_ROWS = 512
CONV_PIECE = 256


def _vmem_limit(nbytes):
    return int(min(nbytes, V7X_VMEM_BYTES - 6 * 1024 * 1024))


def _rms(x):
    return x * lax.rsqrt(jnp.mean(x * x, axis=-1, keepdims=True) + EPS)


def _split3(x):
    hi = x.astype(BF16).astype(F32)
    r1 = x - hi
    mid = r1.astype(BF16).astype(F32)
    lo = (r1 - mid).astype(BF16).astype(F32)
    return hi, mid, lo


def _const_spec(shape):
    nd = len(shape)
    return pl.BlockSpec(shape, lambda *_: (0,) * nd, pipeline_mode=pl.Buffered(1))


def _halo_specs(rows, halo, total_rows, ncols):
    hb = rows // halo
    nhb = total_rows // halo
    return [
        pl.BlockSpec((halo, ncols), lambda i, *_: (jnp.maximum(i * hb - 1, 0), 0)),
        pl.BlockSpec((rows, ncols), lambda i, *_: (i, 0)),
        pl.BlockSpec((halo, ncols), lambda i, *_: (jnp.minimum(i * hb + hb, nhb - 1), 0)),
    ]


def _cast_t_kernel(w_ref, o_ref, *, valid_rows):
    row = pl.program_id(0) * w_ref.shape[0] + lax.broadcasted_iota(jnp.int32, w_ref.shape, 0)
    w = jnp.where(row < valid_rows, w_ref[...], 0.0)
    o_ref[...] = w.T.astype(o_ref.dtype)


def _cast_transposed(w_t, *, cols):
    C, R = w_t.shape
    n = pl.cdiv(C, cols)
    return pl.pallas_call(
        functools.partial(_cast_t_kernel, valid_rows=C),
        grid=(n,),
        in_specs=[pl.BlockSpec((cols, R), lambda j: (j, 0))],
        out_specs=pl.BlockSpec((R, cols), lambda j: (0, j)),
        out_shape=jax.ShapeDtypeStruct((R, n * cols), BF16),
        compiler_params=pltpu.CompilerParams(dimension_semantics=("arbitrary",),
                                             vmem_limit_bytes=_vmem_limit(V7X_VMEM_BYTES)),
        name="cast_w_in",
    )(w_t)


def _scan_prep(raw, dtb_ref, alog_ref, t_ref, pk_ref, sk_ref, ac_ref):
    Q = SSD_CHUNK
    lane = lax.broadcasted_iota(jnp.int32, (Q, V7X_LANES), 1)
    head_lanes = [jnp.logical_and(lane >= d * SSD_HEADS, lane < (d + 1) * SSD_HEADS) for d in (0, 1)]
    a = -jnp.exp(alog_ref[...])
    for c in range(raw.shape[0] // Q):
        dt = jax.nn.softplus(raw[c * Q:(c + 1) * Q, :] + dtb_ref[...])
        da3 = jnp.concatenate([v.astype(BF16) for v in _split3(dt * a)], axis=0)
        acum = jnp.where(head_lanes[0], jnp.dot(t_ref[0], da3, preferred_element_type=F32),
                         jnp.dot(t_ref[1], da3, preferred_element_type=F32)) * LOG2E
        tot = jnp.where(head_lanes[0][0:1], acum[Q - 1:Q, :], acum[0:1, :])
        ac_ref[c * Q:(c + 1) * Q, :] = acum
        sk_ref[c] = (acum - jnp.log2(dt)).T[0:2 * SSD_HEADS, :]
        e_a = jnp.exp2(acum)
        dtw = dt * jnp.exp2(tot - acum)
        for d in (0, 1):
            pk_ref[c, d, 0:Q, :] = _pack3(dtw, head_lanes[d])
            pk_ref[c, d, Q:2 * Q, :] = _pack3(e_a, head_lanes[d])


def _inproj_kernel(xp_ref, xm_ref, xn_ref, g_ref, w_ref, cw_ref, cb_ref, dtb_ref, alog_ref, t_ref,
                   wo_ref, wg_ref,
                   xs_ref, bc_ref, z_ref, u_ref, pk_ref, sk_ref, ac_ref, wo16_ref, wg16_ref,
                   hn_ref, *ext_refs):
    i = pl.program_id(0)
    n = pl.num_programs(0)
    tm = xm_ref.shape[0]
    H = BF16_ROWS
    g = g_ref[...]
    hn_ref[0:H, :] = jnp.where(i > 0, _rms(xp_ref[...]) * g, 0.0).astype(BF16)
    hn_ref[H:H + tm, :] = (_rms(xm_ref[...]) * g).astype(BF16)
    hn_ref[H + tm:2 * H + tm, :] = jnp.where(i < n - 1, _rms(xn_ref[...]) * g, 0.0).astype(BF16)
    for src, dst in ((wo_ref, wo16_ref), (wg_ref, wg16_ref)):
        dst[...] = src[...].astype(BF16)

    cs = CONV_PIECE
    nz, nxs, nbc, nu = z_ref.shape[1], xs_ref.shape[1], bc_ref.shape[1], u_ref.shape[1]
    o_dt = nz + nxs + nbc
    n_dt = 2 * SSD_HEADS
    w_end = o_dt + n_dt + nu
    conv_outs = [(xs_ref, c, nz + c, c) for c in range(0, nxs, cs)]
    conv_outs += [(bc_ref, c, nz + nxs + c, nxs + c) for c in range(0, nbc, cs)]

    def project(c0, c1):
        return jnp.dot(hn_ref[H:H + tm, :], w_ref[:, c0:c1], preferred_element_type=F32)

    def z_chunk(c):
        z_ref[:, c:c + cs] = project(c, c + cs)

    split = o_dt + 2 * cs
    tail = [project(o_dt, split)]
    _scan_prep(tail[0][:, 0:V7X_LANES], dtb_ref, alog_ref, t_ref, pk_ref, sk_ref, ac_ref)
    plain = [functools.partial(z_chunk, c) for c in range(0, nz, cs)]
    plain += [lambda: tail.append(project(split, w_end))]

    half = SSD_CONV // 2
    for k, (o_ref, c0, wc, cc) in enumerate(conv_outs):
        ext_ref = ext_refs[k % len(ext_refs)]
        ext_ref[...] = jnp.dot(hn_ref[...], w_ref[:, wc:wc + cs], preferred_element_type=F32)
        if k < len(plain):
            plain[k]()
        acc = cw_ref[0:1, cc:cc + cs] * ext_ref[H - half:H - half + tm, :]
        for j in range(1, SSD_CONV):
            acc = acc + cw_ref[j:j + 1, cc:cc + cs] * ext_ref[H - half + j:H - half + j + tm, :]
        acc = acc + cb_ref[:, cc:cc + cs]
        o_ref[:, c0:c0 + cs] = (acc * jax.nn.sigmoid(acc)).astype(o_ref.dtype)
    for fn in plain[len(conv_outs):]:
        fn()
    u_ref[...] = jnp.concatenate([tail[0][:, n_dt:], tail[1]], axis=1)


def _inproj(x, g, w, cw, cb, dtb, alog, tcat, out_cols, w_out, w_gate, *, tm):
    L, D = x.shape
    H = BF16_ROWS
    Q = SSD_CHUNK
    n = L // tm
    dts = (BF16, BF16, F32, F32)
    row = lambda c: pl.BlockSpec((tm, c), lambda i: (i, 0))
    slab = lambda a: pl.BlockSpec((a.shape[0] // n, a.shape[1]), lambda i: (i, 0))
    casts = (w_out, w_gate)
    consts = (g, w, cw, cb, dtb, alog, tcat)
    prep_shapes = [((L // Q, 2, 2 * Q, V7X_LANES), BF16), ((L // Q, 2 * SSD_HEADS, Q), F32), ((L, V7X_LANES), F32)]
    prep_specs = [pl.BlockSpec((tm // Q,) + s[1:], lambda i, nd=len(s): (i,) + (0,) * (nd - 1))
                  for s, _ in prep_shapes[:2]] + [row(V7X_LANES)]
    est = (2 * (tm + 2 * H) * D * 4 + w.size * 2 + 2 * tm * sum(out_cols) * 4
           + (tm + 2 * H) * D * 2 + 4 * (tm + 2 * H) * CONV_PIECE * 4 + 2 * sum(a.size for a in casts) // n * 6
           + 24 * tm * CONV_PIECE * 4)
    return pl.pallas_call(
        _inproj_kernel,
        grid=(n,),
        in_specs=_halo_specs(tm, H, L, D) + [_const_spec(a.shape) for a in consts] + [slab(a) for a in casts],
        out_specs=[row(c) for c in out_cols] + prep_specs + [slab(a) for a in casts],
        out_shape=[jax.ShapeDtypeStruct((L, c), dt) for c, dt in zip(out_cols, dts)]
                  + [jax.ShapeDtypeStruct(s, dt) for s, dt in prep_shapes]
                  + [jax.ShapeDtypeStruct(a.shape, BF16) for a in casts],
        scratch_shapes=[pltpu.VMEM((tm + 2 * H, D), BF16)]
                       + [pltpu.VMEM((tm + 2 * H, CONV_PIECE), F32) for _ in range(2)],
        compiler_params=pltpu.CompilerParams(dimension_semantics=("arbitrary",), vmem_limit_bytes=_vmem_limit(est)),
        name="inproj",
    )(x, x, x, *consts, *casts)


SSD_PACK = 16


LOG2E = 1.4426950408889634


def _pack3(v, head_lanes):
    hi, mid, lo = _split3(jnp.where(head_lanes, v, jnp.zeros_like(v)))
    packed = hi + pltpu.roll(mid, SSD_PACK, axis=1) + pltpu.roll(lo, 2 * SSD_PACK, axis=1)
    return packed.astype(BF16)


def _scan_both(xs_refs, bc_refs, pk_refs, sk_refs, ac_refs, nm_ref, rexp_ref, h_ref):
    Q = SSD_CHUNK
    P = SSD_HEAD_DIM
    E = SSD_HEADS_PER_GROUP
    GW = E * P
    DIRS = (0, 1)
    last = (Q - 1, 0)
    xs16 = [xs_refs[d][...] for d in DIRS]
    bc = [bc_refs[d][...] for d in DIRS]
    src_t = [sk_refs[d][...] for d in DIRS]

    exp2x = [jnp.dot(pk_refs[d][0:2 * Q, :], rexp_ref[d], preferred_element_type=F32) for d in DIRS]
    acum = [ac_refs[d][...] for d in DIRS]
    xw = [xs16[d] * exp2x[d][0:Q].astype(BF16) for d in DIRS]
    ea_x = [exp2x[d][Q:2 * Q] for d in DIRS]
    negmask = [nm_ref[d] for d in DIRS]
    first_head = lax.broadcasted_iota(jnp.int32, (Q, 2 * P), 1) < P

    y_groups = [[], []]
    for g in range(SSD_GROUPS):
        c0 = g * GW
        bm = [bc[d][:, g * SSD_STATE:(g + 1) * SSD_STATE] for d in DIRS]
        cm = [bc[d][:, (SSD_GROUPS + g) * SSD_STATE:(SSD_GROUPS + g + 1) * SSD_STATE] for d in DIRS]
        cb = [lax.dot_general(cm[d], bm[d], (((1,), (1,)), ((), ())), preferred_element_type=F32)
              for d in DIRS]
        st = [lax.dot_general(bm[d], xw[d][:, c0:c0 + GW], (((0,), (0,)), ((), ())), preferred_element_type=F32)
              for d in DIRS]
        h_in = [h_ref[d, g] for d in DIRS]
        y_off = [jnp.dot(cm[d], h_in[d].astype(BF16), preferred_element_type=F32) for d in DIRS]
        y_g = [y_off[d] * ea_x[d][:, c0:c0 + GW] for d in DIRS]
        for d in DIRS:
            h_ref[d, g] = h_in[d] * ea_x[d][last[d]:last[d] + 1, c0:c0 + GW] + st[d]
        pairs = [[], []]
        for hp in range(E // 2):
            for d in DIRS:
                h0 = g * E + hp * 2
                ms = []
                for k in range(2):
                    r = d * SSD_HEADS + h0 + k
                    seg = jnp.broadcast_to(acum[d][:, r:r + 1], (Q, Q)) - src_t[d][r:r + 1, :] + negmask[d]
                    ms.append((cb[d] * jnp.exp2(seg)).astype(BF16))
                m2 = jnp.concatenate(ms, axis=1)
                xp = xs16[d][:, (g * E + hp * 2) * P:(g * E + hp * 2 + 2) * P]
                zero = jnp.zeros_like(xp)
                rhs = jnp.concatenate([jnp.where(first_head, xp, zero), jnp.where(first_head, zero, xp)], axis=0)
                pairs[d].append(jnp.dot(m2, rhs, preferred_element_type=F32))
        for d in DIRS:
            y_groups[d].append(y_g[d] + jnp.concatenate(pairs[d], axis=1))
    return [(jnp.concatenate(y_groups[d], axis=1), xs16[d]) for d in DIRS]


def _ssd_kernel(xsf_ref, bcf_ref, pkf_ref, skf_ref, acf_ref, zf_ref, xsb_ref, bcb_ref, pkb_ref, skb_ref, acb_ref, zb_ref,
                dexp_ref, nw_ref, nm_ref, rexp_ref, wup_ref,
                lo_ref, hi_ref, wup16_ref,
                y_ref, h_ref):
    Q = SSD_CHUNK
    GW = SSD_HEADS_PER_GROUP * SSD_HEAD_DIM
    i = pl.program_id(0)
    nc = pl.num_programs(0)

    wup16_ref[...] = wup_ref[...].astype(BF16)

    @pl.when(i == 0)
    def _():
        h_ref[...] = jnp.zeros_like(h_ref)

    (y_f, xs_f), (y_b, xs_b) = _scan_both((xsf_ref, xsb_ref), (bcf_ref, bcb_ref), (pkf_ref, pkb_ref),
                                          (skf_ref, skb_ref), (acf_ref, acb_ref), nm_ref, rexp_ref, h_ref)
    row_f = pl.multiple_of(i * Q, Q)
    row_b = pl.multiple_of((nc - 1 - i) * Q, Q)

    @pl.when(i < nc // 2)
    def _():
        y_ref[pl.ds(row_f, Q), :] = y_f
        y_ref[pl.ds(row_b, Q), :] = y_b

    def finish(y, xs, z_ref, o_ref):
        yy = y + xs.astype(F32) * dexp_ref[...]
        z = z_ref[...]
        yy = yy * (z * jax.nn.sigmoid(z))
        outs = [_rms(yy[:, g * GW:(g + 1) * GW]) for g in range(SSD_GROUPS)]
        o_ref[...] = (jnp.concatenate(outs, axis=1) * nw_ref[...]).astype(o_ref.dtype)

    @pl.when(i >= nc // 2)
    def _():
        finish(y_f + y_ref[pl.ds(row_f, Q), :], xs_f, zf_ref, hi_ref)
        finish(y_b + y_ref[pl.ds(row_b, Q), :], xs_b, zb_ref, lo_ref)


def _ssd(xs, bc, pk, sk, ac, z, dexp, nw, negmask, rexp, w_up):
    L, DS = xs.shape
    Q = SSD_CHUNK
    nc = L // Q
    hc = nc // 2
    slab = pl.BlockSpec((w_up.shape[0] // nc, w_up.shape[1]), lambda i: (i, 0))
    fwd = lambda i: i
    bwd = lambda i: nc - 1 - i
    fwd_late = lambda i: jnp.maximum(i, hc)
    bwd_late = lambda i: jnp.minimum(nc - 1 - i, hc - 1)
    in_specs = [
        pl.BlockSpec((Q, DS), lambda i: (fwd(i), 0)),
        pl.BlockSpec((Q, bc.shape[1]), lambda i: (fwd(i), 0)),
        pl.BlockSpec((None, None) + pk.shape[2:], lambda i: (fwd(i), 0, 0, 0)),
        pl.BlockSpec((None,) + sk.shape[1:], lambda i: (fwd(i), 0, 0)),
        pl.BlockSpec((Q, V7X_LANES), lambda i: (fwd(i), 0)),
        pl.BlockSpec((Q, DS), lambda i: (fwd_late(i), 0)),
        pl.BlockSpec((Q, DS), lambda i: (bwd(i), 0)),
        pl.BlockSpec((Q, bc.shape[1]), lambda i: (bwd(i), 0)),
        pl.BlockSpec((None, None) + pk.shape[2:], lambda i: (bwd(i), 1, 0, 0)),
        pl.BlockSpec((None,) + sk.shape[1:], lambda i: (bwd(i), 0, 0)),
        pl.BlockSpec((Q, V7X_LANES), lambda i: (bwd(i), 0)),
        pl.BlockSpec((Q, DS), lambda i: (bwd_late(i), 0)),
    ] + [_const_spec(a.shape) for a in (dexp, nw, negmask, rexp)] + [slab]
    est = (L * DS * 4 + 2 * SSD_GROUPS * SSD_STATE * DS * 4 + 8 * Q * (DS + bc.shape[1]) * 4
           + rexp.size * 2 + 4 * Q * DS * 4 * 2 + 40 * Q * DS * 4 + 12 * w_up.size // nc)
    return pl.pallas_call(
        _ssd_kernel,
        grid=(nc,),
        in_specs=in_specs,
        out_specs=[pl.BlockSpec((Q, DS), lambda i: (bwd_late(i), 0)),
                   pl.BlockSpec((Q, DS), lambda i: (fwd_late(i) - hc, 0)), slab],
        out_shape=[jax.ShapeDtypeStruct((L // 2, DS), BF16), jax.ShapeDtypeStruct((L // 2, DS), BF16),
                   jax.ShapeDtypeStruct(w_up.shape, BF16)],
        scratch_shapes=[
            pltpu.VMEM((L, DS), F32),
            pltpu.VMEM((2, SSD_GROUPS, SSD_STATE, DS // SSD_GROUPS), F32),
        ],
        compiler_params=pltpu.CompilerParams(dimension_semantics=("arbitrary",), vmem_limit_bytes=_vmem_limit(est)),
        name="ssd",
    )(xs, bc, pk, sk, ac, z, xs, bc, pk, sk, ac, z, dexp, nw, negmask, rexp, w_up)


def _mixout_kernel(up_ref, um_ref, un_ref, ylo_ref, yhi_ref, x_ref, pw_ref, ps_ref, wo_ref, g_ref,
                   o_ref, ext_ref, mix_ref, *, seq):
    i = pl.program_id(0)
    n = pl.num_programs(0)
    tm = um_ref.shape[0]
    cg = um_ref.shape[1] // len(POOL_WINDOWS)
    H = V7X_SUBLANES
    ext_ref[0:H, :] = jnp.where(i > 0, up_ref[...], 0.0)
    ext_ref[H:H + tm, :] = um_ref[...]
    ext_ref[H + tm:2 * H + tm, :] = jnp.where(i < n - 1, un_ref[...], 0.0)
    ys = jnp.where(i < n // 2, ylo_ref[...], yhi_ref[...])
    ds = ys.shape[1]
    dc = o_ref.shape[1] // len(POOL_WINDOWS)
    t = i * tm + lax.broadcasted_iota(jnp.int32, (tm, cg), 0)
    rows = tm + 2 * H

    def ahead(v, k):
        return pltpu.roll(v, (rows - k) % rows, axis=0)

    pooled = []
    for gi, k in enumerate(POOL_WINDOWS):
        cols = slice(gi * cg, (gi + 1) * cg)
        mix_ref[:, gi * dc:(gi + 1) * dc] = jnp.dot(ys, wo_ref[0:ds, gi * dc:(gi + 1) * dc],
                                                    preferred_element_type=F32)
        e = ext_ref[:, cols]
        half = k // 2
        run, length = e, 1
        while length < half:
            run = run + ahead(run, length)
            length *= 2
        before = run[0:tm] if half == H else ahead(run, rows - half)[H:H + tm]
        acc = before + run[H:H + tm]
        cnt = (jnp.minimum(t + (k - k // 2), seq) - jnp.maximum(t - k // 2, 0)).astype(F32)
        mixed = acc / cnt - um_ref[:, cols]
        yp = jnp.dot(mixed.astype(BF16), pw_ref[gi], preferred_element_type=F32) * ps_ref[:, cols]
        pooled.append(yp.astype(BF16))
    ypool = jnp.concatenate(pooled, axis=1)
    mix = mix_ref[...] + jnp.dot(ypool, wo_ref[ds:, :], preferred_element_type=F32)
    o_ref[...] = x_ref[...] + _rms(mix) * g_ref[...]


def _mixout(u, y_lo, y_hi, x, pool_w, pool_scale, w_out, g, *, tm):
    assert all(k % 2 == 0 and k // 2 <= V7X_SUBLANES and (k // 2) & (k // 2 - 1) == 0 for k in POOL_WINDOWS)
    L, DP = u.shape
    D = x.shape[1]
    DS = y_lo.shape[1]
    n = L // tm
    row = lambda n_: pl.BlockSpec((tm, n_), lambda i: (i, 0))
    in_specs = (_halo_specs(tm, V7X_SUBLANES, L, DP)
                + [pl.BlockSpec((tm, DS), lambda i: (jnp.minimum(i, n // 2 - 1), 0)),
                   pl.BlockSpec((tm, DS), lambda i: (jnp.maximum(i - n // 2, 0), 0)), row(D)]
                + [_const_spec(a.shape) for a in (pool_w, pool_scale, w_out, g)])
    est = (2 * tm * (DP * 4 + 2 * DS * 2 + D * 4 + D * 4) + (w_out.size + pool_w.size) * 2
           + (tm + 2 * V7X_SUBLANES) * DP * 4 + 8 * tm * D * 4)
    return pl.pallas_call(
        functools.partial(_mixout_kernel, seq=L),
        grid=(n,),
        in_specs=in_specs,
        out_specs=row(D),
        out_shape=jax.ShapeDtypeStruct((L, D), F32),
        scratch_shapes=[pltpu.VMEM((tm + 2 * V7X_SUBLANES, DP), F32), pltpu.VMEM((tm, D), F32)],
        compiler_params=pltpu.CompilerParams(dimension_semantics=("arbitrary",), vmem_limit_bytes=_vmem_limit(est)),
        name="mixout",
    )(u, u, u, y_lo, y_hi, x, pool_w, pool_scale, w_out, g)


def _ffn_kernel(hp_ref, hm_ref, hx_ref, gpre_ref, wg_ref, wv_ref, cw_ref, cb_ref, wd_ref, gpost_ref,
                o_ref, hn_ref, gate_ref):
    i = pl.program_id(0)
    f = pl.program_id(1)
    n = pl.num_programs(0)
    nf = pl.num_programs(1)
    tm = hm_ref.shape[0]
    H = BF16_ROWS

    @pl.when(f == 0)
    def _():
        normed = lambda v: (_rms(v) * gpre_ref[...]).astype(BF16)
        hn_ref[0:H, :] = jnp.where(i > 0, normed(hp_ref[...]), jnp.zeros((H, hn_ref.shape[1]), BF16))
        for r in range(0, tm, FFN_NORM_ROWS):
            hn_ref[H + r:H + r + FFN_NORM_ROWS, :] = normed(hm_ref[r:r + FFN_NORM_ROWS, :])
        hn_ref[H + tm:2 * H + tm, :] = jnp.where(i < n - 1, normed(hx_ref[...]),
                                                 jnp.zeros((H, hn_ref.shape[1]), BF16))
        o_ref[...] = jnp.zeros_like(o_ref)

    gate_ref[...] = jnp.dot(hn_ref[...], wg_ref[...], preferred_element_type=F32)
    wd = wd_ref[...].astype(BF16)
    val = jnp.dot(hn_ref[H:H + tm, :], wv_ref[...], preferred_element_type=F32)
    half = FFN_CONV // 2
    gc = cw_ref[0:1, :] * gate_ref[H - half:H - half + tm, :]
    for j in range(1, FFN_CONV):
        gc = gc + cw_ref[j:j + 1, :] * gate_ref[H - half + j:H - half + j + tm, :]
    gc = gc + cb_ref[...]
    act = (jax.nn.gelu(gc, approximate=True) * val).astype(BF16)
    o_ref[...] += jnp.dot(act, wd, preferred_element_type=F32)

    @pl.when(f == nf - 1)
    def _():
        for r in range(0, tm, FFN_NORM_ROWS):
            rows = slice(r, r + FFN_NORM_ROWS)
            o_ref[rows, :] = hm_ref[rows, :] + _rms(o_ref[rows, :]) * gpost_ref[...]


def _ffn(h, gpre, w_up, cw, cb, w_down, gpost, *, tm, tf):
    L, D = h.shape
    DF = w_down.shape[0]
    nf = DF // tf
    H = BF16_ROWS
    in_specs = _halo_specs(tm, H, L, D) + [
        pl.BlockSpec(gpre.shape, lambda i, f: (0, 0)),
        pl.BlockSpec((D, tf), lambda i, f: (0, f)),
        pl.BlockSpec((D, tf), lambda i, f: (0, nf + f)),
        pl.BlockSpec((FFN_CONV, tf), lambda i, f: (0, f)),
        pl.BlockSpec((1, tf), lambda i, f: (0, f)),
        pl.BlockSpec((tf, D), lambda i, f: (f, 0)),
        pl.BlockSpec(gpost.shape, lambda i, f: (0, 0)),
    ]
    est = (2 * (tm + 2 * H) * D * 4 + 2 * tm * D * 4 + (tm + 2 * H) * D * 2 + (tm + 2 * H) * tf * 4
           + 2 * 2 * D * tf * 2 + 2 * D * tf * w_down.dtype.itemsize + 8 * tm * tf * 4 + tm * D * 4)
    return pl.pallas_call(
        _ffn_kernel,
        grid=(L // tm, nf),
        in_specs=in_specs,
        out_specs=pl.BlockSpec((tm, D), lambda i, f: (i, 0)),
        out_shape=jax.ShapeDtypeStruct((L, D), F32),
        scratch_shapes=[
            pltpu.VMEM((tm + 2 * H, D), BF16),
            pltpu.VMEM((tm + 2 * H, tf), F32),
        ],
        compiler_params=pltpu.CompilerParams(dimension_semantics=("arbitrary", "arbitrary"),
                                             vmem_limit_bytes=_vmem_limit(est)),
        name="ffn",
    )(h, h, h, gpre, w_up, w_up, cw, cb, w_down, gpost)


def _ple_kernel(h_ref, p_ref, gpre_ref, wg_ref, wp_ref, gpost_ref, o_ref):
    h = h_ref[...]
    hn = (_rms(h) * gpre_ref[...]).astype(BF16)
    gate = jax.nn.sigmoid(jnp.dot(hn, wg_ref[...], preferred_element_type=F32))
    pe = jnp.dot(p_ref[...].astype(BF16), wp_ref[...], preferred_element_type=F32)
    o_ref[...] = h + _rms(gate * pe) * gpost_ref[...]


def _ple(h, p, gpre, w_gate, w_ple, gpost, *, tm):
    L, D = h.shape
    row = lambda n: pl.BlockSpec((tm, n), lambda i: (i, 0))
    est = 2 * tm * (2 * D + p.shape[1]) * 4 + (w_gate.size + w_ple.size) * 2 + 8 * tm * D * 4
    return pl.pallas_call(
        _ple_kernel,
        grid=(L // tm,),
        in_specs=[row(D), row(p.shape[1])] + [_const_spec(a.shape) for a in (gpre, w_gate, w_ple, gpost)],
        out_specs=row(D),
        out_shape=jax.ShapeDtypeStruct((L, D), F32),
        compiler_params=pltpu.CompilerParams(dimension_semantics=("arbitrary",), vmem_limit_bytes=_vmem_limit(est)),
        name="ple",
    )(h, p, gpre, w_gate, w_ple, gpost)


def _ssd_constants():
    Q = SSD_CHUNK
    r = jnp.arange(Q)
    lower = (r[:, None] >= r[None, :])
    tri = jnp.stack([lower, lower.T]).astype(BF16)
    tcat = jnp.concatenate([tri, tri, tri], axis=2)
    negmask = jnp.where(jnp.stack([lower, lower.T]), 0.0, NEG_BIG).astype(F32)
    k = jnp.arange(V7X_LANES)
    rexp = []
    for d in range(2):
        packed = jnp.logical_and(k >= d * SSD_HEADS, k < d * SSD_HEADS + 3 * SSD_PACK)
        head_of_lane = jnp.where(packed, (k - d * SSD_HEADS) % SSD_PACK, -1)
        rexp.append(head_of_lane[:, None] == (jnp.arange(SSD_HEADS * SSD_HEAD_DIM) // SSD_HEAD_DIM)[None, :])
    return tcat, negmask, jnp.stack(rexp).astype(BF16)


def _dir_lanes(v, fill):
    out = jnp.full((1, V7X_LANES), fill, F32)
    return out.at[0, :v.size].set(v.astype(F32).reshape(-1))


def kernel(x, p, mix_norm_pre, mix_norm_post, w_in, ssd_conv_w, ssd_conv_b, ssd_dt_bias, ssd_a_log, ssd_d,
           ssd_norm, pool_w, pool_scale, w_out, ffn_norm_pre, ffn_norm_post, w_ffn_up, ffn_conv_w, ffn_conv_b,
           w_ffn_down, ple_norm_pre, w_ple_gate, w_ple, ple_norm_post):
    B, L, D = x.shape
    depth = w_in.shape[0]
    d_ssd = SSD_HEADS * SSD_HEAD_DIM
    n_bc = 2 * SSD_GROUPS * SSD_STATE
    o_dt = 2 * d_ssd + n_bc
    o_u = o_dt + 2 * SSD_HEADS
    d_pool = w_in.shape[2] - o_u
    tcat, negmask, rexp = _ssd_constants()
    row = lambda v: v.reshape(1, -1).astype(F32)

    outs = []
    for b in range(B):
        h = x[b]
        for i in range(depth):
            xs, bc, z, u, pk, sk, ac, w_out16, w_gate16 = _inproj(
                h, row(mix_norm_pre[i]), _cast_transposed(w_in[i].T, cols=CONV_PIECE), ssd_conv_w[i].astype(F32),
                row(ssd_conv_b[i]), _dir_lanes(ssd_dt_bias[i], 0.0), _dir_lanes(ssd_a_log[i], NEG_BIG), tcat,
                (d_ssd, n_bc, d_ssd, d_pool), w_out[i], w_ple_gate[i], tm=INPROJ_ROWS)
            y_lo, y_hi, w_up16 = _ssd(
                xs, bc, pk, sk, ac, z, jnp.repeat(ssd_d[i].astype(F32), SSD_HEAD_DIM).reshape(1, -1),
                row(ssd_norm[i]), negmask, rexp, w_ffn_up[i])
            h = _mixout(u, y_lo, y_hi, h, pool_w[i].astype(BF16), row(pool_scale[i]), w_out16,
                        row(mix_norm_post[i]), tm=MIXOUT_ROWS)
            h = _ffn(h, row(ffn_norm_pre[i]), w_up16, ffn_conv_w[i].astype(F32), row(ffn_conv_b[i]), w_ffn_down[i],
                     row(ffn_norm_post[i]), tm=FFN_ROWS, tf=FFN_COLS)
            h = _ple(h, p[i, b], row(ple_norm_pre[i]), w_gate16, w_ple[i].astype(BF16),
                     row(ple_norm_post[i]), tm=PLE_ROWS)
        outs.append(h)
    return jnp.stack(outs)
```

```python
import functools

import jax
import jax.numpy as jnp
from jax import lax
from jax.experimental import pallas as pl
from jax.experimental.pallas import tpu as pltpu

F32 = jnp.float32
BF16 = jnp.bfloat16
EPS = 1e-6

V7X_VMEM_BYTES = 64 * 1024 * 1024
V7X_LANES = 128
V7X_SUBLANES = 8
BF16_ROWS = 16

SSD_HEAD_DIM = 64
SSD_HEADS = 16
SSD_GROUPS = 2
SSD_HEADS_PER_GROUP = SSD_HEADS // SSD_GROUPS
SSD_STATE = 128
SSD_CONV = 5
SSD_CHUNK = 128
POOL_WINDOWS = (2, 4, 8, 16)
FFN_CONV = 3
NEG_BIG = -1e30

INPROJ_ROWS = 512
MIXOUT_ROWS = 512
MIXOUT_PIECES = 2
FFN_ROWS = 1024
FFN_COLS = 512
PLE_ROWS = 512
CONV_PIECE = 256


def _vmem_limit(nbytes):
    return int(min(nbytes, V7X_VMEM_BYTES - 6 * 1024 * 1024))


def _rms(x):
    return x * lax.rsqrt(jnp.mean(x * x, axis=-1, keepdims=True) + EPS)


def _split3(x):
    hi = x.astype(BF16).astype(F32)
    r1 = x - hi
    mid = r1.astype(BF16).astype(F32)
    lo = (r1 - mid).astype(BF16).astype(F32)
    return hi, mid, lo


def _const_spec(shape):
    nd = len(shape)
    return pl.BlockSpec(shape, lambda *_: (0,) * nd, pipeline_mode=pl.Buffered(1))


def _halo_specs(rows, halo, total_rows, ncols):
    hb = rows // halo
    nhb = total_rows // halo
    return [
        pl.BlockSpec((halo, ncols), lambda i, *_: (jnp.maximum(i * hb - 1, 0), 0)),
        pl.BlockSpec((rows, ncols), lambda i, *_: (i, 0)),
        pl.BlockSpec((halo, ncols), lambda i, *_: (jnp.minimum(i * hb + hb, nhb - 1), 0)),
    ]


def _cast_t_kernel(w_ref, o_ref, *, valid_rows):
    row = pl.program_id(0) * w_ref.shape[0] + lax.broadcasted_iota(jnp.int32, w_ref.shape, 0)
    w = jnp.where(row < valid_rows, w_ref[...], 0.0)
    o_ref[...] = w.T.astype(o_ref.dtype)


def _cast_transposed(w_t, *, cols):
    C, R = w_t.shape
    n = pl.cdiv(C, cols)
    return pl.pallas_call(
        functools.partial(_cast_t_kernel, valid_rows=C),
        grid=(n,),
        in_specs=[pl.BlockSpec((cols, R), lambda j: (j, 0))],
        out_specs=pl.BlockSpec((R, cols), lambda j: (0, j)),
        out_shape=jax.ShapeDtypeStruct((R, n * cols), BF16),
        compiler_params=pltpu.CompilerParams(dimension_semantics=("arbitrary",),
                                             vmem_limit_bytes=_vmem_limit(V7X_VMEM_BYTES)),
        name="cast_w_in",
    )(w_t)


def _scan_prep(raw, dtb_ref, alog_ref, t_ref, pk_ref, sk_ref, ac_ref):
    Q = SSD_CHUNK
    lane = lax.broadcasted_iota(jnp.int32, (Q, V7X_LANES), 1)
    head_lanes = [jnp.logical_and(lane >= d * SSD_HEADS, lane < (d + 1) * SSD_HEADS) for d in (0, 1)]
    a = -jnp.exp(alog_ref[...])
    for c in range(raw.shape[0] // Q):
        dt = jax.nn.softplus(raw[c * Q:(c + 1) * Q, :] + dtb_ref[...])
        da3 = jnp.concatenate([v.astype(BF16) for v in _split3(dt * a)], axis=0)
        acum = jnp.where(head_lanes[0], jnp.dot(t_ref[0], da3, preferred_element_type=F32),
                         jnp.dot(t_ref[1], da3, preferred_element_type=F32)) * LOG2E
        tot = jnp.where(head_lanes[0][0:1], acum[Q - 1:Q, :], acum[0:1, :])
        ac_ref[c * Q:(c + 1) * Q, :] = acum
        sk_ref[c] = (acum - jnp.log2(dt)).T[0:2 * SSD_HEADS, :]
        e_a = jnp.exp2(acum)
        dtw = dt * jnp.exp2(tot - acum)
        for d in (0, 1):
            pk_ref[c, d, 0:Q, :] = _pack3(dtw, head_lanes[d])
            pk_ref[c, d, Q:2 * Q, :] = _pack3(e_a, head_lanes[d])


def _inproj_kernel(xp_ref, xm_ref, xn_ref, g_ref, w_ref, cw_ref, cb_ref, dtb_ref, alog_ref, t_ref,
                   wo_ref, wg_ref,
                   xs_ref, bc_ref, z_ref, u_ref, pk_ref, sk_ref, ac_ref, wo16_ref, wg16_ref,
                   hn_ref, *ext_refs):
    i = pl.program_id(0)
    n = pl.num_programs(0)
    tm = xm_ref.shape[0]
    H = BF16_ROWS
    g = g_ref[...]
    hn_ref[0:H, :] = jnp.where(i > 0, _rms(xp_ref[...]) * g, 0.0).astype(BF16)
    hn_ref[H:H + tm, :] = (_rms(xm_ref[...]) * g).astype(BF16)
    hn_ref[H + tm:2 * H + tm, :] = jnp.where(i < n - 1, _rms(xn_ref[...]) * g, 0.0).astype(BF16)
    for src, dst in ((wo_ref, wo16_ref), (wg_ref, wg16_ref)):
        dst[...] = src[...].astype(BF16)

    cs = CONV_PIECE
    nz, nxs, nbc, nu = z_ref.shape[1], xs_ref.shape[1], bc_ref.shape[1], u_ref.shape[1]
    o_dt = nz + nxs + nbc
    n_dt = 2 * SSD_HEADS
    w_end = o_dt + n_dt + nu
    conv_outs = [(xs_ref, c, nz + c, c) for c in range(0, nxs, cs)]
    conv_outs += [(bc_ref, c, nz + nxs + c, nxs + c) for c in range(0, nbc, cs)]

    def project(c0, c1):
        return jnp.dot(hn_ref[H:H + tm, :], w_ref[:, c0:c1], preferred_element_type=F32)

    def z_chunk(c):
        z_ref[:, c:c + cs] = project(c, c + cs)

    split = o_dt + 2 * cs
    tail = [project(o_dt, split)]
    _scan_prep(tail[0][:, 0:V7X_LANES], dtb_ref, alog_ref, t_ref, pk_ref, sk_ref, ac_ref)
    plain = [functools.partial(z_chunk, c) for c in range(0, nz, cs)]
    plain += [lambda: tail.append(project(split, w_end))]

    half = SSD_CONV // 2
    for k, (o_ref, c0, wc, cc) in enumerate(conv_outs):
        ext_ref = ext_refs[k % len(ext_refs)]
        ext_ref[...] = jnp.dot(hn_ref[...], w_ref[:, wc:wc + cs], preferred_element_type=F32)
        if k < len(plain):
            plain[k]()
        acc = cw_ref[0:1, cc:cc + cs] * ext_ref[H - half:H - half + tm, :]
        for j in range(1, SSD_CONV):
            acc = acc + cw_ref[j:j + 1, cc:cc + cs] * ext_ref[H - half + j:H - half + j + tm, :]
        acc = acc + cb_ref[:, cc:cc + cs]
        o_ref[:, c0:c0 + cs] = (acc * jax.nn.sigmoid(acc)).astype(o_ref.dtype)
    for fn in plain[len(conv_outs):]:
        fn()
    u_ref[...] = jnp.concatenate([tail[0][:, n_dt:], tail[1]], axis=1)


def _inproj(x, g, w, cw, cb, dtb, alog, tcat, out_cols, w_out, w_gate, *, tm):
    L, D = x.shape
    H = BF16_ROWS
    Q = SSD_CHUNK
    n = L // tm
    dts = (BF16, BF16, F32, F32)
    row = lambda c: pl.BlockSpec((tm, c), lambda i: (i, 0))
    slab = lambda a: pl.BlockSpec((a.shape[0] // n, a.shape[1]), lambda i: (i, 0))
    casts = (w_out, w_gate)
    consts = (g, w, cw, cb, dtb, alog, tcat)
    prep_shapes = [((L // Q, 2, 2 * Q, V7X_LANES), BF16), ((L // Q, 2 * SSD_HEADS, Q), F32), ((L, V7X_LANES), F32)]
    prep_specs = [pl.BlockSpec((tm // Q,) + s[1:], lambda i, nd=len(s): (i,) + (0,) * (nd - 1))
                  for s, _ in prep_shapes[:2]] + [row(V7X_LANES)]
    est = (2 * (tm + 2 * H) * D * 4 + w.size * 2 + 2 * tm * sum(out_cols) * 4
           + (tm + 2 * H) * D * 2 + 4 * (tm + 2 * H) * CONV_PIECE * 4 + 2 * sum(a.size for a in casts) // n * 6
           + 24 * tm * CONV_PIECE * 4)
    return pl.pallas_call(
        _inproj_kernel,
        grid=(n,),
        in_specs=_halo_specs(tm, H, L, D) + [_const_spec(a.shape) for a in consts] + [slab(a) for a in casts],
        out_specs=[row(c) for c in out_cols] + prep_specs + [slab(a) for a in casts],
        out_shape=[jax.ShapeDtypeStruct((L, c), dt) for c, dt in zip(out_cols, dts)]
                  + [jax.ShapeDtypeStruct(s, dt) for s, dt in prep_shapes]
                  + [jax.ShapeDtypeStruct(a.shape, BF16) for a in casts],
        scratch_shapes=[pltpu.VMEM((tm + 2 * H, D), BF16)]
                       + [pltpu.VMEM((tm + 2 * H, CONV_PIECE), F32) for _ in range(2)],
        compiler_params=pltpu.CompilerParams(dimension_semantics=("arbitrary",), vmem_limit_bytes=_vmem_limit(est)),
        name="inproj",
    )(x, x, x, *consts, *casts)


SSD_PACK = 16


LOG2E = 1.4426950408889634


def _pack3(v, head_lanes):
    hi, mid, lo = _split3(jnp.where(head_lanes, v, jnp.zeros_like(v)))
    packed = hi + pltpu.roll(mid, SSD_PACK, axis=1) + pltpu.roll(lo, 2 * SSD_PACK, axis=1)
    return packed.astype(BF16)


def _scan_both(xs_refs, bc_refs, pk_refs, sk_refs, ac_refs, nm_ref, rexp_ref, h_ref):
    Q = SSD_CHUNK
    P = SSD_HEAD_DIM
    E = SSD_HEADS_PER_GROUP
    GW = E * P
    DIRS = (0, 1)
    last = (Q - 1, 0)
    xs16 = [xs_refs[d][...] for d in DIRS]
    bc = [bc_refs[d][...] for d in DIRS]
    src_t = [sk_refs[d][...] for d in DIRS]

    exp2x = [jnp.dot(pk_refs[d][0:2 * Q, :], rexp_ref[d], preferred_element_type=F32) for d in DIRS]
    acum = [ac_refs[d][...] for d in DIRS]
    xw = [xs16[d] * exp2x[d][0:Q].astype(BF16) for d in DIRS]
    ea_x = [exp2x[d][Q:2 * Q] for d in DIRS]
    negmask = [nm_ref[d] for d in DIRS]
    first_head = lax.broadcasted_iota(jnp.int32, (Q, 2 * P), 1) < P

    y_groups = [[], []]
    for g in range(SSD_GROUPS):
        c0 = g * GW
        bm = [bc[d][:, g * SSD_STATE:(g + 1) * SSD_STATE] for d in DIRS]
        cm = [bc[d][:, (SSD_GROUPS + g) * SSD_STATE:(SSD_GROUPS + g + 1) * SSD_STATE] for d in DIRS]
        cb = [lax.dot_general(cm[d], bm[d], (((1,), (1,)), ((), ())), preferred_element_type=F32)
              for d in DIRS]
        st = [lax.dot_general(bm[d], xw[d][:, c0:c0 + GW], (((0,), (0,)), ((), ())), preferred_element_type=F32)
              for d in DIRS]
        h_in = [h_ref[d, g] for d in DIRS]
        y_off = [jnp.dot(cm[d], h_in[d].astype(BF16), preferred_element_type=F32) for d in DIRS]
        y_g = [y_off[d] * ea_x[d][:, c0:c0 + GW] for d in DIRS]
        for d in DIRS:
            h_ref[d, g] = h_in[d] * ea_x[d][last[d]:last[d] + 1, c0:c0 + GW] + st[d]
        pairs = [[], []]
        for hp in range(E // 2):
            for d in DIRS:
                h0 = g * E + hp * 2
                ms = []
                for k in range(2):
                    r = d * SSD_HEADS + h0 + k
                    seg = jnp.broadcast_to(acum[d][:, r:r + 1], (Q, Q)) - src_t[d][r:r + 1, :] + negmask[d]
                    ms.append((cb[d] * jnp.exp2(seg)).astype(BF16))
                m2 = jnp.concatenate(ms, axis=1)
                xp = xs16[d][:, (g * E + hp * 2) * P:(g * E + hp * 2 + 2) * P]
                zero = jnp.zeros_like(xp)
                rhs = jnp.concatenate([jnp.where(first_head, xp, zero), jnp.where(first_head, zero, xp)], axis=0)
                pairs[d].append(jnp.dot(m2, rhs, preferred_element_type=F32))
        for d in DIRS:
            y_groups[d].append(y_g[d] + jnp.concatenate(pairs[d], axis=1))
    return [(jnp.concatenate(y_groups[d], axis=1), xs16[d]) for d in DIRS]


def _ssd_kernel(xsf_ref, bcf_ref, pkf_ref, skf_ref, acf_ref, zf_ref, xsb_ref, bcb_ref, pkb_ref, skb_ref, acb_ref, zb_ref,
                dexp_ref, nw_ref, nm_ref, rexp_ref, wup_ref,
                lo_ref, hi_ref, wup16_ref,
                y_ref, h_ref):
    Q = SSD_CHUNK
    GW = SSD_HEADS_PER_GROUP * SSD_HEAD_DIM
    i = pl.program_id(0)
    nc = pl.num_programs(0)

    wup16_ref[...] = wup_ref[...].astype(BF16)

    @pl.when(i == 0)
    def _():
        h_ref[...] = jnp.zeros_like(h_ref)

    (y_f, xs_f), (y_b, xs_b) = _scan_both((xsf_ref, xsb_ref), (bcf_ref, bcb_ref), (pkf_ref, pkb_ref),
                                          (skf_ref, skb_ref), (acf_ref, acb_ref), nm_ref, rexp_ref, h_ref)
    row_f = pl.multiple_of(i * Q, Q)
    row_b = pl.multiple_of((nc - 1 - i) * Q, Q)

    @pl.when(i < nc // 2)
    def _():
        y_ref[pl.ds(row_f, Q), :] = y_f
        y_ref[pl.ds(row_b, Q), :] = y_b

    def finish(y, xs, z_ref, o_ref):
        yy = y + xs.astype(F32) * dexp_ref[...]
        z = z_ref[...]
        yy = yy * (z * jax.nn.sigmoid(z))
        outs = [_rms(yy[:, g * GW:(g + 1) * GW]) for g in range(SSD_GROUPS)]
        o_ref[...] = (jnp.concatenate(outs, axis=1) * nw_ref[...]).astype(o_ref.dtype)

    @pl.when(i >= nc // 2)
    def _():
        finish(y_f + y_ref[pl.ds(row_f, Q), :], xs_f, zf_ref, hi_ref)
        finish(y_b + y_ref[pl.ds(row_b, Q), :], xs_b, zb_ref, lo_ref)


def _ssd(xs, bc, pk, sk, ac, z, dexp, nw, negmask, rexp, w_up):
    L, DS = xs.shape
    Q = SSD_CHUNK
    nc = L // Q
    hc = nc // 2
    slab = pl.BlockSpec((w_up.shape[0] // nc, w_up.shape[1]), lambda i: (i, 0))
    fwd = lambda i: i
    bwd = lambda i: nc - 1 - i
    fwd_late = lambda i: jnp.maximum(i, hc)
    bwd_late = lambda i: jnp.minimum(nc - 1 - i, hc - 1)
    in_specs = [
        pl.BlockSpec((Q, DS), lambda i: (fwd(i), 0)),
        pl.BlockSpec((Q, bc.shape[1]), lambda i: (fwd(i), 0)),
        pl.BlockSpec((None, None) + pk.shape[2:], lambda i: (fwd(i), 0, 0, 0)),
        pl.BlockSpec((None,) + sk.shape[1:], lambda i: (fwd(i), 0, 0)),
        pl.BlockSpec((Q, V7X_LANES), lambda i: (fwd(i), 0)),
        pl.BlockSpec((Q, DS), lambda i: (fwd_late(i), 0)),
        pl.BlockSpec((Q, DS), lambda i: (bwd(i), 0)),
        pl.BlockSpec((Q, bc.shape[1]), lambda i: (bwd(i), 0)),
        pl.BlockSpec((None, None) + pk.shape[2:], lambda i: (bwd(i), 1, 0, 0)),
        pl.BlockSpec((None,) + sk.shape[1:], lambda i: (bwd(i), 0, 0)),
        pl.BlockSpec((Q, V7X_LANES), lambda i: (bwd(i), 0)),
        pl.BlockSpec((Q, DS), lambda i: (bwd_late(i), 0)),
    ] + [_const_spec(a.shape) for a in (dexp, nw, negmask, rexp)] + [slab]
    est = (L * DS * 4 + 2 * SSD_GROUPS * SSD_STATE * DS * 4 + 8 * Q * (DS + bc.shape[1]) * 4
           + rexp.size * 2 + 4 * Q * DS * 4 * 2 + 40 * Q * DS * 4 + 12 * w_up.size // nc)
    return pl.pallas_call(
        _ssd_kernel,
        grid=(nc,),
        in_specs=in_specs,
        out_specs=[pl.BlockSpec((Q, DS), lambda i: (bwd_late(i), 0)),
                   pl.BlockSpec((Q, DS), lambda i: (fwd_late(i) - hc, 0)), slab],
        out_shape=[jax.ShapeDtypeStruct((L // 2, DS), BF16), jax.ShapeDtypeStruct((L // 2, DS), BF16),
                   jax.ShapeDtypeStruct(w_up.shape, BF16)],
        scratch_shapes=[
            pltpu.VMEM((L, DS), F32),
            pltpu.VMEM((2, SSD_GROUPS, SSD_STATE, DS // SSD_GROUPS), F32),
        ],
        compiler_params=pltpu.CompilerParams(dimension_semantics=("arbitrary",), vmem_limit_bytes=_vmem_limit(est)),
        name="ssd",
    )(xs, bc, pk, sk, ac, z, xs, bc, pk, sk, ac, z, dexp, nw, negmask, rexp, w_up)


def _mixout_kernel(up_ref, um_ref, un_ref, ylo_ref, yhi_ref, x_ref, pw_ref, ps_ref, wo_ref, g_ref, gn_ref,
                   o_ref, on_ref, ext_ref, mix_ref, *, seq):
    i = pl.program_id(0)
    n = pl.num_programs(0)
    tm = um_ref.shape[0]
    cg = um_ref.shape[1] // len(POOL_WINDOWS)
    H = V7X_SUBLANES
    ext_ref[0:H, :] = jnp.where(i > 0, up_ref[...], 0.0)
    ext_ref[H:H + tm, :] = um_ref[...]
    ext_ref[H + tm:2 * H + tm, :] = jnp.where(i < n - 1, un_ref[...], 0.0)
    ys = jnp.where(i < n // 2, ylo_ref[...], yhi_ref[...])
    ds = ys.shape[1]
    dc = o_ref.shape[1] // len(POOL_WINDOWS)
    t = i * tm + lax.broadcasted_iota(jnp.int32, (tm, cg), 0)
    rows = tm + 2 * H

    def ahead(v, k):
        return pltpu.roll(v, (rows - k) % rows, axis=0)

    pooled = []
    for gi, k in enumerate(POOL_WINDOWS):
        cols = slice(gi * cg, (gi + 1) * cg)
        mix_ref[:, gi * dc:(gi + 1) * dc] = jnp.dot(ys, wo_ref[0:ds, gi * dc:(gi + 1) * dc],
                                                    preferred_element_type=F32)
        e = ext_ref[:, cols]
        half = k // 2
        run, length = e, 1
        while length < half:
            run = run + ahead(run, length)
            length *= 2
        before = run[0:tm] if half == H else ahead(run, rows - half)[H:H + tm]
        acc = before + run[H:H + tm]
        cnt = (jnp.minimum(t + (k - k // 2), seq) - jnp.maximum(t - k // 2, 0)).astype(F32)
        mixed = acc / cnt - um_ref[:, cols]
        yp = jnp.dot(mixed.astype(BF16), pw_ref[gi], preferred_element_type=F32) * ps_ref[:, cols]
        pooled.append(yp.astype(BF16))
    ypool = jnp.concatenate(pooled, axis=1)
    tp = tm // MIXOUT_PIECES

    def project(k):
        rows = slice(k * tp, (k + 1) * tp)
        mix_ref[rows, :] += jnp.dot(ypool[rows], wo_ref[ds:, :], preferred_element_type=F32)

    def finish(k):
        rows = slice(k * tp, (k + 1) * tp)
        h = x_ref[rows, :] + _rms(mix_ref[rows, :]) * g_ref[...]
        o_ref[rows, :] = h
        on_ref[rows, :] = (_rms(h) * gn_ref[...]).astype(on_ref.dtype)

    project(0)
    for k in range(1, MIXOUT_PIECES):
        project(k)
        finish(k - 1)
    finish(MIXOUT_PIECES - 1)


def _mixout(u, y_lo, y_hi, x, pool_w, pool_scale, w_out, g, g_next, *, tm):
    assert all(k % 2 == 0 and k // 2 <= V7X_SUBLANES and (k // 2) & (k // 2 - 1) == 0 for k in POOL_WINDOWS)
    L, DP = u.shape
    D = x.shape[1]
    DS = y_lo.shape[1]
    n = L // tm
    row = lambda n_: pl.BlockSpec((tm, n_), lambda i: (i, 0))
    in_specs = (_halo_specs(tm, V7X_SUBLANES, L, DP)
                + [pl.BlockSpec((tm, DS), lambda i: (jnp.minimum(i, n // 2 - 1), 0)),
                   pl.BlockSpec((tm, DS), lambda i: (jnp.maximum(i - n // 2, 0), 0)), row(D)]
                + [_const_spec(a.shape) for a in (pool_w, pool_scale, w_out, g, g_next)])
    est = (2 * tm * (DP * 4 + 2 * DS * 2 + D * 4 + D * 4 + D * 2) + (w_out.size + pool_w.size) * 2
           + (tm + 2 * V7X_SUBLANES) * DP * 4 + 8 * tm * D * 4)
    return pl.pallas_call(
        functools.partial(_mixout_kernel, seq=L),
        grid=(n,),
        in_specs=in_specs,
        out_specs=[row(D), row(D)],
        out_shape=[jax.ShapeDtypeStruct((L, D), F32), jax.ShapeDtypeStruct((L, D), BF16)],
        scratch_shapes=[pltpu.VMEM((tm + 2 * V7X_SUBLANES, DP), F32), pltpu.VMEM((tm, D), F32)],
        compiler_params=pltpu.CompilerParams(dimension_semantics=("arbitrary",), vmem_limit_bytes=_vmem_limit(est)),
        name="mixout",
    )(u, u, u, y_lo, y_hi, x, pool_w, pool_scale, w_out, g, g_next)


def _ffn_kernel(hp_ref, hm_ref, hx_ref, wg_ref, wv_ref, cw_ref, cb_ref, wd_ref, gpost_ref,
                o_ref, hn_ref, gate_ref):
    i = pl.program_id(0)
    f = pl.program_id(1)
    n = pl.num_programs(0)
    nf = pl.num_programs(1)
    tm = hm_ref.shape[0]
    H = BF16_ROWS

    @pl.when(f == 0)
    def _():
        hn_ref[0:H, :] = jnp.where(i > 0, hp_ref[...], jnp.zeros_like(hp_ref))
        hn_ref[H:H + tm, :] = hm_ref[...]
        hn_ref[H + tm:2 * H + tm, :] = jnp.where(i < n - 1, hx_ref[...], jnp.zeros_like(hx_ref))
        o_ref[...] = jnp.zeros_like(o_ref)

    gate_ref[...] = jnp.dot(hn_ref[...], wg_ref[...], preferred_element_type=F32)
    wd = wd_ref[...].astype(BF16)
    val = jnp.dot(hm_ref[...], wv_ref[...], preferred_element_type=F32)
    half = FFN_CONV // 2
    gc = cw_ref[0:1, :] * gate_ref[H - half:H - half + tm, :]
    for j in range(1, FFN_CONV):
        gc = gc + cw_ref[j:j + 1, :] * gate_ref[H - half + j:H - half + j + tm, :]
    gc = gc + cb_ref[...]
    act = (jax.nn.gelu(gc, approximate=True) * val).astype(BF16)
    o_ref[...] += jnp.dot(act, wd, preferred_element_type=F32)

    @pl.when(f == nf - 1)
    def _():
        o_ref[...] = _rms(o_ref[...]) * gpost_ref[...]


def _ffn(hn, w_up, cw, cb, w_down, gpost, *, tm, tf):
    L, D = hn.shape
    DF = w_down.shape[0]
    nf = DF // tf
    H = BF16_ROWS
    in_specs = _halo_specs(tm, H, L, D) + [
        pl.BlockSpec((D, tf), lambda i, f: (0, f)),
        pl.BlockSpec((D, tf), lambda i, f: (0, nf + f)),
        pl.BlockSpec((FFN_CONV, tf), lambda i, f: (0, f)),
        pl.BlockSpec((1, tf), lambda i, f: (0, f)),
        pl.BlockSpec((tf, D), lambda i, f: (f, 0)),
        pl.BlockSpec(gpost.shape, lambda i, f: (0, 0)),
    ]
    est = (2 * (tm + 2 * H) * D * 2 + 2 * tm * D * 4 + (tm + 2 * H) * D * 2 + (tm + 2 * H) * tf * 4
           + 2 * 2 * D * tf * 2 + 2 * D * tf * w_down.dtype.itemsize + 8 * tm * tf * 4 + tm * D * 4)
    return pl.pallas_call(
        _ffn_kernel,
        grid=(L // tm, nf),
        in_specs=in_specs,
        out_specs=pl.BlockSpec((tm, D), lambda i, f: (i, 0)),
        out_shape=jax.ShapeDtypeStruct((L, D), F32),
        scratch_shapes=[
            pltpu.VMEM((tm + 2 * H, D), BF16),
            pltpu.VMEM((tm + 2 * H, tf), F32),
        ],
        compiler_params=pltpu.CompilerParams(dimension_semantics=("arbitrary", "arbitrary"),
                                             vmem_limit_bytes=_vmem_limit(est)),
        name="ffn",
    )(hn, hn, hn, w_up, w_up, cw, cb, w_down, gpost)


def _ple_kernel(h_ref, r_ref, p_ref, gpre_ref, wg_ref, wp_ref, gpost_ref, o_ref):
    h = h_ref[...] + r_ref[...]
    hn = (_rms(h) * gpre_ref[...]).astype(BF16)
    gate = jax.nn.sigmoid(jnp.dot(hn, wg_ref[...], preferred_element_type=F32))
    pe = jnp.dot(p_ref[...].astype(BF16), wp_ref[...], preferred_element_type=F32)
    o_ref[...] = h + _rms(gate * pe) * gpost_ref[...]


def _ple(h, r, p, gpre, w_gate, w_ple, gpost, *, tm):
    L, D = h.shape
    row = lambda n: pl.BlockSpec((tm, n), lambda i: (i, 0))
    est = 2 * tm * (3 * D + p.shape[1]) * 4 + (w_gate.size + w_ple.size) * 2 + 8 * tm * D * 4
    return pl.pallas_call(
        _ple_kernel,
        grid=(L // tm,),
        in_specs=[row(D), row(D), row(p.shape[1])] + [_const_spec(a.shape) for a in (gpre, w_gate, w_ple, gpost)],
        out_specs=row(D),
        out_shape=jax.ShapeDtypeStruct((L, D), F32),
        compiler_params=pltpu.CompilerParams(dimension_semantics=("arbitrary",), vmem_limit_bytes=_vmem_limit(est)),
        name="ple",
    )(h, r, p, gpre, w_gate, w_ple, gpost)


def _ssd_constants():
    Q = SSD_CHUNK
    r = jnp.arange(Q)
    lower = (r[:, None] >= r[None, :])
    tri = jnp.stack([lower, lower.T]).astype(BF16)
    tcat = jnp.concatenate([tri, tri, tri], axis=2)
    negmask = jnp.where(jnp.stack([lower, lower.T]), 0.0, NEG_BIG).astype(F32)
    k = jnp.arange(V7X_LANES)
    rexp = []
    for d in range(2):
        packed = jnp.logical_and(k >= d * SSD_HEADS, k < d * SSD_HEADS + 3 * SSD_PACK)
        head_of_lane = jnp.where(packed, (k - d * SSD_HEADS) % SSD_PACK, -1)
        rexp.append(head_of_lane[:, None] == (jnp.arange(SSD_HEADS * SSD_HEAD_DIM) // SSD_HEAD_DIM)[None, :])
    return tcat, negmask, jnp.stack(rexp).astype(BF16)


def _dir_lanes(v, fill):
    out = jnp.full((1, V7X_LANES), fill, F32)
    return out.at[0, :v.size].set(v.astype(F32).reshape(-1))


def kernel(x, p, mix_norm_pre, mix_norm_post, w_in, ssd_conv_w, ssd_conv_b, ssd_dt_bias, ssd_a_log, ssd_d,
           ssd_norm, pool_w, pool_scale, w_out, ffn_norm_pre, ffn_norm_post, w_ffn_up, ffn_conv_w, ffn_conv_b,
           w_ffn_down, ple_norm_pre, w_ple_gate, w_ple, ple_norm_post):
    B, L, D = x.shape
    depth = w_in.shape[0]
    d_ssd = SSD_HEADS * SSD_HEAD_DIM
    n_bc = 2 * SSD_GROUPS * SSD_STATE
    o_dt = 2 * d_ssd + n_bc
    o_u = o_dt + 2 * SSD_HEADS
    d_pool = w_in.shape[2] - o_u
    tcat, negmask, rexp = _ssd_constants()
    row = lambda v: v.reshape(1, -1).astype(F32)

    outs = []
    for b in range(B):
        h = x[b]
        for i in range(depth):
            xs, bc, z, u, pk, sk, ac, w_out16, w_gate16 = _inproj(
                h, row(mix_norm_pre[i]), _cast_transposed(w_in[i].T, cols=CONV_PIECE), ssd_conv_w[i].astype(F32),
                row(ssd_conv_b[i]), _dir_lanes(ssd_dt_bias[i], 0.0), _dir_lanes(ssd_a_log[i], NEG_BIG), tcat,
                (d_ssd, n_bc, d_ssd, d_pool), w_out[i], w_ple_gate[i], tm=INPROJ_ROWS)
            y_lo, y_hi, w_up16 = _ssd(
                xs, bc, pk, sk, ac, z, jnp.repeat(ssd_d[i].astype(F32), SSD_HEAD_DIM).reshape(1, -1),
                row(ssd_norm[i]), negmask, rexp, w_ffn_up[i])
            h, hn = _mixout(u, y_lo, y_hi, h, pool_w[i].astype(BF16), row(pool_scale[i]), w_out16,
                            row(mix_norm_post[i]), row(ffn_norm_pre[i]), tm=MIXOUT_ROWS)
            r = _ffn(hn, w_up16, ffn_conv_w[i].astype(F32), row(ffn_conv_b[i]), w_ffn_down[i], row(ffn_norm_post[i]),
                     tm=FFN_ROWS, tf=FFN_COLS)
            h = _ple(h, r, p[i, b], row(ple_norm_pre[i]), w_gate16, w_ple[i].astype(BF16),
                     row(ple_norm_post[i]), tm=PLE_ROWS)
        outs.append(h)
    return jnp.stack(outs)
```

```python
import functools

import jax
import jax.numpy as jnp
from jax import lax
from jax.experimental import pallas as pl
from jax.experimental.pallas import tpu as pltpu

F32 = jnp.float32
BF16 = jnp.bfloat16
EPS = 1e-6

V7X_VMEM_BYTES = 64 * 1024 * 1024
V7X_LANES = 128
V7X_SUBLANES = 8
BF16_ROWS = 16

SSD_HEAD_DIM = 64
SSD_HEADS = 16
SSD_GROUPS = 2
SSD_HEADS_PER_GROUP = SSD_HEADS // SSD_GROUPS
SSD_STATE = 128
SSD_CONV = 5
SSD_CHUNK = 128
POOL_WINDOWS = (2, 4, 8, 16)
FFN_CONV = 3
NEG_BIG = -1e30

INPROJ_ROWS = 512
MIXOUT_ROWS = 512
MIXOUT_PIECES = 2
FFN_ROWS = 1024
FFN_COLS = 512
PLE_ROWS = 512
PLE_PIECES = 2
CONV_PIECE = 256


def _vmem_limit(nbytes):
    return int(min(nbytes, V7X_VMEM_BYTES - 6 * 1024 * 1024))


def _rms(x):
    return x * lax.rsqrt(jnp.mean(x * x, axis=-1, keepdims=True) + EPS)


def _split3(x):
    hi = x.astype(BF16).astype(F32)
    r1 = x - hi
    mid = r1.astype(BF16).astype(F32)
    lo = (r1 - mid).astype(BF16).astype(F32)
    return hi, mid, lo


def _const_spec(shape):
    nd = len(shape)
    return pl.BlockSpec(shape, lambda *_: (0,) * nd, pipeline_mode=pl.Buffered(1))


def _halo_specs(rows, halo, total_rows, ncols):
    hb = rows // halo
    nhb = total_rows // halo
    return [
        pl.BlockSpec((halo, ncols), lambda i, *_: (jnp.maximum(i * hb - 1, 0), 0)),
        pl.BlockSpec((rows, ncols), lambda i, *_: (i, 0)),
        pl.BlockSpec((halo, ncols), lambda i, *_: (jnp.minimum(i * hb + hb, nhb - 1), 0)),
    ]


def _cast_t_kernel(w_ref, o_ref, *, valid_rows):
    row = pl.program_id(0) * w_ref.shape[0] + lax.broadcasted_iota(jnp.int32, w_ref.shape, 0)
    w = jnp.where(row < valid_rows, w_ref[...], 0.0)
    o_ref[...] = w.T.astype(o_ref.dtype)


def _cast_transposed(w_t, *, cols):
    C, R = w_t.shape
    n = pl.cdiv(C, cols)
    return pl.pallas_call(
        functools.partial(_cast_t_kernel, valid_rows=C),
        grid=(n,),
        in_specs=[pl.BlockSpec((cols, R), lambda j: (j, 0))],
        out_specs=pl.BlockSpec((R, cols), lambda j: (0, j)),
        out_shape=jax.ShapeDtypeStruct((R, n * cols), BF16),
        compiler_params=pltpu.CompilerParams(dimension_semantics=("arbitrary",),
                                             vmem_limit_bytes=_vmem_limit(V7X_VMEM_BYTES)),
        name="cast_w_in",
    )(w_t)


def _scan_prep(raw, dtb_ref, alog_ref, t_ref, pk_ref, sk_ref, ac_ref):
    Q = SSD_CHUNK
    lane = lax.broadcasted_iota(jnp.int32, (Q, V7X_LANES), 1)
    head_lanes = [jnp.logical_and(lane >= d * SSD_HEADS, lane < (d + 1) * SSD_HEADS) for d in (0, 1)]
    a = -jnp.exp(alog_ref[...])
    for c in range(raw.shape[0] // Q):
        dt = jax.nn.softplus(raw[c * Q:(c + 1) * Q, :] + dtb_ref[...])
        da3 = jnp.concatenate([v.astype(BF16) for v in _split3(dt * a)], axis=0)
        acum = jnp.where(head_lanes[0], jnp.dot(t_ref[0], da3, preferred_element_type=F32),
                         jnp.dot(t_ref[1], da3, preferred_element_type=F32)) * LOG2E
        tot = jnp.where(head_lanes[0][0:1], acum[Q - 1:Q, :], acum[0:1, :])
        ac_ref[c * Q:(c + 1) * Q, :] = acum
        sk_ref[c] = (acum - jnp.log2(dt)).T[0:2 * SSD_HEADS, :]
        e_a = jnp.exp2(acum)
        dtw = dt * jnp.exp2(tot - acum)
        for d in (0, 1):
            pk_ref[c, d, 0:Q, :] = _pack3(dtw, head_lanes[d])
            pk_ref[c, d, Q:2 * Q, :] = _pack3(e_a, head_lanes[d])


def _inproj_kernel(xp_ref, xm_ref, xn_ref, g_ref, w_ref, cw_ref, cb_ref, dtb_ref, alog_ref, t_ref,
                   wo_ref, wg_ref,
                   xs_ref, bc_ref, z_ref, u_ref, pk_ref, sk_ref, ac_ref, wo16_ref, wg16_ref,
                   hn_ref, *ext_refs):
    i = pl.program_id(0)
    n = pl.num_programs(0)
    tm = xm_ref.shape[0]
    H = BF16_ROWS
    g = g_ref[...]
    hn_ref[0:H, :] = jnp.where(i > 0, _rms(xp_ref[...]) * g, 0.0).astype(BF16)
    hn_ref[H:H + tm, :] = (_rms(xm_ref[...]) * g).astype(BF16)
    hn_ref[H + tm:2 * H + tm, :] = jnp.where(i < n - 1, _rms(xn_ref[...]) * g, 0.0).astype(BF16)
    for src, dst in ((wo_ref, wo16_ref), (wg_ref, wg16_ref)):
        dst[...] = src[...].astype(BF16)

    cs = CONV_PIECE
    nz, nxs, nbc, nu = z_ref.shape[1], xs_ref.shape[1], bc_ref.shape[1], u_ref.shape[1]
    o_dt = nz + nxs + nbc
    n_dt = 2 * SSD_HEADS
    w_end = o_dt + n_dt + nu
    conv_outs = [(xs_ref, c, nz + c, c) for c in range(0, nxs, cs)]
    conv_outs += [(bc_ref, c, nz + nxs + c, nxs + c) for c in range(0, nbc, cs)]

    def project(c0, c1):
        return jnp.dot(hn_ref[H:H + tm, :], w_ref[:, c0:c1], preferred_element_type=F32)

    def z_chunk(c):
        z_ref[:, c:c + cs] = project(c, c + cs)

    split = o_dt + 2 * cs
    tail = [project(o_dt, split)]
    _scan_prep(tail[0][:, 0:V7X_LANES], dtb_ref, alog_ref, t_ref, pk_ref, sk_ref, ac_ref)
    plain = [functools.partial(z_chunk, c) for c in range(0, nz, cs)]
    plain += [lambda: tail.append(project(split, w_end))]

    half = SSD_CONV // 2
    for k, (o_ref, c0, wc, cc) in enumerate(conv_outs):
        ext_ref = ext_refs[k % len(ext_refs)]
        ext_ref[...] = jnp.dot(hn_ref[...], w_ref[:, wc:wc + cs], preferred_element_type=F32)
        if k < len(plain):
            plain[k]()
        acc = cw_ref[0:1, cc:cc + cs] * ext_ref[H - half:H - half + tm, :]
        for j in range(1, SSD_CONV):
            acc = acc + cw_ref[j:j + 1, cc:cc + cs] * ext_ref[H - half + j:H - half + j + tm, :]
        acc = acc + cb_ref[:, cc:cc + cs]
        o_ref[:, c0:c0 + cs] = (acc * jax.nn.sigmoid(acc)).astype(o_ref.dtype)
    for fn in plain[len(conv_outs):]:
        fn()
    u_ref[...] = jnp.concatenate([tail[0][:, n_dt:], tail[1]], axis=1)


def _inproj(x, g, w, cw, cb, dtb, alog, tcat, out_cols, w_out, w_gate, *, tm):
    L, D = x.shape
    H = BF16_ROWS
    Q = SSD_CHUNK
    n = L // tm
    dts = (BF16, BF16, F32, F32)
    row = lambda c: pl.BlockSpec((tm, c), lambda i: (i, 0))
    slab = lambda a: pl.BlockSpec((a.shape[0] // n, a.shape[1]), lambda i: (i, 0))
    casts = (w_out, w_gate)
    consts = (g, w, cw, cb, dtb, alog, tcat)
    prep_shapes = [((L // Q, 2, 2 * Q, V7X_LANES), BF16), ((L // Q, 2 * SSD_HEADS, Q), F32), ((L, V7X_LANES), F32)]
    prep_specs = [pl.BlockSpec((tm // Q,) + s[1:], lambda i, nd=len(s): (i,) + (0,) * (nd - 1))
                  for s, _ in prep_shapes[:2]] + [row(V7X_LANES)]
    est = (2 * (tm + 2 * H) * D * 4 + w.size * 2 + 2 * tm * sum(out_cols) * 4
           + (tm + 2 * H) * D * 2 + 4 * (tm + 2 * H) * CONV_PIECE * 4 + 2 * sum(a.size for a in casts) // n * 6
           + 24 * tm * CONV_PIECE * 4)
    return pl.pallas_call(
        _inproj_kernel,
        grid=(n,),
        in_specs=_halo_specs(tm, H, L, D) + [_const_spec(a.shape) for a in consts] + [slab(a) for a in casts],
        out_specs=[row(c) for c in out_cols] + prep_specs + [slab(a) for a in casts],
        out_shape=[jax.ShapeDtypeStruct((L, c), dt) for c, dt in zip(out_cols, dts)]
                  + [jax.ShapeDtypeStruct(s, dt) for s, dt in prep_shapes]
                  + [jax.ShapeDtypeStruct(a.shape, BF16) for a in casts],
        scratch_shapes=[pltpu.VMEM((tm + 2 * H, D), BF16)]
                       + [pltpu.VMEM((tm + 2 * H, CONV_PIECE), F32) for _ in range(2)],
        compiler_params=pltpu.CompilerParams(dimension_semantics=("arbitrary",), vmem_limit_bytes=_vmem_limit(est)),
        name="inproj",
    )(x, x, x, *consts, *casts)


SSD_PACK = 16


LOG2E = 1.4426950408889634


def _pack3(v, head_lanes):
    hi, mid, lo = _split3(jnp.where(head_lanes, v, jnp.zeros_like(v)))
    packed = hi + pltpu.roll(mid, SSD_PACK, axis=1) + pltpu.roll(lo, 2 * SSD_PACK, axis=1)
    return packed.astype(BF16)


def _scan_both(xs_refs, bc_refs, pk_refs, sk_refs, ac_refs, nm_ref, rexp_ref, h_ref):
    Q = SSD_CHUNK
    P = SSD_HEAD_DIM
    E = SSD_HEADS_PER_GROUP
    GW = E * P
    DIRS = (0, 1)
    last = (Q - 1, 0)
    xs16 = [xs_refs[d][...] for d in DIRS]
    bc = [bc_refs[d][...] for d in DIRS]
    src_t = [sk_refs[d][...] for d in DIRS]

    exp2x = [jnp.dot(pk_refs[d][0:2 * Q, :], rexp_ref[d], preferred_element_type=F32) for d in DIRS]
    acum = [ac_refs[d][...] for d in DIRS]
    xw = [xs16[d] * exp2x[d][0:Q].astype(BF16) for d in DIRS]
    ea_x = [exp2x[d][Q:2 * Q] for d in DIRS]
    negmask = [nm_ref[d] for d in DIRS]
    first_head = lax.broadcasted_iota(jnp.int32, (Q, 2 * P), 1) < P

    y_groups = [[], []]
    for g in range(SSD_GROUPS):
        c0 = g * GW
        bm = [bc[d][:, g * SSD_STATE:(g + 1) * SSD_STATE] for d in DIRS]
        cm = [bc[d][:, (SSD_GROUPS + g) * SSD_STATE:(SSD_GROUPS + g + 1) * SSD_STATE] for d in DIRS]
        cb = [lax.dot_general(cm[d], bm[d], (((1,), (1,)), ((), ())), preferred_element_type=F32)
              for d in DIRS]
        st = [lax.dot_general(bm[d], xw[d][:, c0:c0 + GW], (((0,), (0,)), ((), ())), preferred_element_type=F32)
              for d in DIRS]
        h_in = [h_ref[d, g] for d in DIRS]
        y_off = [jnp.dot(cm[d], h_in[d].astype(BF16), preferred_element_type=F32) for d in DIRS]
        y_g = [y_off[d] * ea_x[d][:, c0:c0 + GW] for d in DIRS]
        for d in DIRS:
            h_ref[d, g] = h_in[d] * ea_x[d][last[d]:last[d] + 1, c0:c0 + GW] + st[d]
        pairs = [[], []]
        for hp in range(E // 2):
            for d in DIRS:
                h0 = g * E + hp * 2
                ms = []
                for k in range(2):
                    r = d * SSD_HEADS + h0 + k
                    seg = jnp.broadcast_to(acum[d][:, r:r + 1], (Q, Q)) - src_t[d][r:r + 1, :] + negmask[d]
                    ms.append((cb[d] * jnp.exp2(seg)).astype(BF16))
                m2 = jnp.concatenate(ms, axis=1)
                xp = xs16[d][:, (g * E + hp * 2) * P:(g * E + hp * 2 + 2) * P]
                zero = jnp.zeros_like(xp)
                rhs = jnp.concatenate([jnp.where(first_head, xp, zero), jnp.where(first_head, zero, xp)], axis=0)
                pairs[d].append(jnp.dot(m2, rhs, preferred_element_type=F32))
        for d in DIRS:
            y_groups[d].append(y_g[d] + jnp.concatenate(pairs[d], axis=1))
    return [(jnp.concatenate(y_groups[d], axis=1), xs16[d]) for d in DIRS]


def _ssd_kernel(xsf_ref, bcf_ref, pkf_ref, skf_ref, acf_ref, zf_ref, xsb_ref, bcb_ref, pkb_ref, skb_ref, acb_ref, zb_ref,
                dexp_ref, nw_ref, nm_ref, rexp_ref, wup_ref,
                lo_ref, hi_ref, wup16_ref,
                y_ref, h_ref):
    Q = SSD_CHUNK
    GW = SSD_HEADS_PER_GROUP * SSD_HEAD_DIM
    i = pl.program_id(0)
    nc = pl.num_programs(0)

    wup16_ref[...] = wup_ref[...].astype(BF16)

    @pl.when(i == 0)
    def _():
        h_ref[...] = jnp.zeros_like(h_ref)

    (y_f, xs_f), (y_b, xs_b) = _scan_both((xsf_ref, xsb_ref), (bcf_ref, bcb_ref), (pkf_ref, pkb_ref),
                                          (skf_ref, skb_ref), (acf_ref, acb_ref), nm_ref, rexp_ref, h_ref)
    row_f = pl.multiple_of(i * Q, Q)
    row_b = pl.multiple_of((nc - 1 - i) * Q, Q)

    @pl.when(i < nc // 2)
    def _():
        y_ref[pl.ds(row_f, Q), :] = y_f
        y_ref[pl.ds(row_b, Q), :] = y_b

    def finish(y, xs, z_ref, o_ref):
        yy = y + xs.astype(F32) * dexp_ref[...]
        z = z_ref[...]
        yy = yy * (z * jax.nn.sigmoid(z))
        outs = [_rms(yy[:, g * GW:(g + 1) * GW]) for g in range(SSD_GROUPS)]
        o_ref[...] = (jnp.concatenate(outs, axis=1) * nw_ref[...]).astype(o_ref.dtype)

    @pl.when(i >= nc // 2)
    def _():
        finish(y_f + y_ref[pl.ds(row_f, Q), :], xs_f, zf_ref, hi_ref)
        finish(y_b + y_ref[pl.ds(row_b, Q), :], xs_b, zb_ref, lo_ref)


def _ssd(xs, bc, pk, sk, ac, z, dexp, nw, negmask, rexp, w_up):
    L, DS = xs.shape
    Q = SSD_CHUNK
    nc = L // Q
    hc = nc // 2
    slab = pl.BlockSpec((w_up.shape[0] // nc, w_up.shape[1]), lambda i: (i, 0))
    fwd = lambda i: i
    bwd = lambda i: nc - 1 - i
    fwd_late = lambda i: jnp.maximum(i, hc)
    bwd_late = lambda i: jnp.minimum(nc - 1 - i, hc - 1)
    in_specs = [
        pl.BlockSpec((Q, DS), lambda i: (fwd(i), 0)),
        pl.BlockSpec((Q, bc.shape[1]), lambda i: (fwd(i), 0)),
        pl.BlockSpec((None, None) + pk.shape[2:], lambda i: (fwd(i), 0, 0, 0)),
        pl.BlockSpec((None,) + sk.shape[1:], lambda i: (fwd(i), 0, 0)),
        pl.BlockSpec((Q, V7X_LANES), lambda i: (fwd(i), 0)),
        pl.BlockSpec((Q, DS), lambda i: (fwd_late(i), 0)),
        pl.BlockSpec((Q, DS), lambda i: (bwd(i), 0)),
        pl.BlockSpec((Q, bc.shape[1]), lambda i: (bwd(i), 0)),
        pl.BlockSpec((None, None) + pk.shape[2:], lambda i: (bwd(i), 1, 0, 0)),
        pl.BlockSpec((None,) + sk.shape[1:], lambda i: (bwd(i), 0, 0)),
        pl.BlockSpec((Q, V7X_LANES), lambda i: (bwd(i), 0)),
        pl.BlockSpec((Q, DS), lambda i: (bwd_late(i), 0)),
    ] + [_const_spec(a.shape) for a in (dexp, nw, negmask, rexp)] + [slab]
    est = (L * DS * 4 + 2 * SSD_GROUPS * SSD_STATE * DS * 4 + 8 * Q * (DS + bc.shape[1]) * 4
           + rexp.size * 2 + 4 * Q * DS * 4 * 2 + 40 * Q * DS * 4 + 12 * w_up.size // nc)
    return pl.pallas_call(
        _ssd_kernel,
        grid=(nc,),
        in_specs=in_specs,
        out_specs=[pl.BlockSpec((Q, DS), lambda i: (bwd_late(i), 0)),
                   pl.BlockSpec((Q, DS), lambda i: (fwd_late(i) - hc, 0)), slab],
        out_shape=[jax.ShapeDtypeStruct((L // 2, DS), BF16), jax.ShapeDtypeStruct((L // 2, DS), BF16),
                   jax.ShapeDtypeStruct(w_up.shape, BF16)],
        scratch_shapes=[
            pltpu.VMEM((L, DS), F32),
            pltpu.VMEM((2, SSD_GROUPS, SSD_STATE, DS // SSD_GROUPS), F32),
        ],
        compiler_params=pltpu.CompilerParams(dimension_semantics=("arbitrary",), vmem_limit_bytes=_vmem_limit(est)),
        name="ssd",
    )(xs, bc, pk, sk, ac, z, xs, bc, pk, sk, ac, z, dexp, nw, negmask, rexp, w_up)


def _mixout_kernel(up_ref, um_ref, un_ref, ylo_ref, yhi_ref, x_ref, pw_ref, ps_ref, wo_ref, g_ref, gn_ref,
                   o_ref, on_ref, ext_ref, mix_ref, *, seq):
    i = pl.program_id(0)
    n = pl.num_programs(0)
    tm = um_ref.shape[0]
    cg = um_ref.shape[1] // len(POOL_WINDOWS)
    H = V7X_SUBLANES
    ext_ref[0:H, :] = jnp.where(i > 0, up_ref[...], 0.0)
    ext_ref[H:H + tm, :] = um_ref[...]
    ext_ref[H + tm:2 * H + tm, :] = jnp.where(i < n - 1, un_ref[...], 0.0)
    ys = jnp.where(i < n // 2, ylo_ref[...], yhi_ref[...])
    ds = ys.shape[1]
    dc = o_ref.shape[1] // len(POOL_WINDOWS)
    t = i * tm + lax.broadcasted_iota(jnp.int32, (tm, cg), 0)
    rows = tm + 2 * H

    def ahead(v, k):
        return pltpu.roll(v, (rows - k) % rows, axis=0)

    pooled = []
    for gi, k in enumerate(POOL_WINDOWS):
        cols = slice(gi * cg, (gi + 1) * cg)
        mix_ref[:, gi * dc:(gi + 1) * dc] = jnp.dot(ys, wo_ref[0:ds, gi * dc:(gi + 1) * dc],
                                                    preferred_element_type=F32)
        e = ext_ref[:, cols]
        half = k // 2
        run, length = e, 1
        while length < half:
            run = run + ahead(run, length)
            length *= 2
        before = run[0:tm] if half == H else ahead(run, rows - half)[H:H + tm]
        acc = before + run[H:H + tm]
        cnt = (jnp.minimum(t + (k - k // 2), seq) - jnp.maximum(t - k // 2, 0)).astype(F32)
        mixed = acc / cnt - um_ref[:, cols]
        yp = jnp.dot(mixed.astype(BF16), pw_ref[gi], preferred_element_type=F32) * ps_ref[:, cols]
        pooled.append(yp.astype(BF16))
    ypool = jnp.concatenate(pooled, axis=1)
    tp = tm // MIXOUT_PIECES

    def project(k):
        rows = slice(k * tp, (k + 1) * tp)
        mix_ref[rows, :] += jnp.dot(ypool[rows], wo_ref[ds:, :], preferred_element_type=F32)

    def finish(k):
        rows = slice(k * tp, (k + 1) * tp)
        h = x_ref[rows, :] + _rms(mix_ref[rows, :]) * g_ref[...]
        o_ref[rows, :] = h
        on_ref[rows, :] = (_rms(h) * gn_ref[...]).astype(on_ref.dtype)

    project(0)
    for k in range(1, MIXOUT_PIECES):
        project(k)
        finish(k - 1)
    finish(MIXOUT_PIECES - 1)


def _mixout(u, y_lo, y_hi, x, pool_w, pool_scale, w_out, g, g_next, *, tm):
    assert all(k % 2 == 0 and k // 2 <= V7X_SUBLANES and (k // 2) & (k // 2 - 1) == 0 for k in POOL_WINDOWS)
    L, DP = u.shape
    D = x.shape[1]
    DS = y_lo.shape[1]
    n = L // tm
    row = lambda n_: pl.BlockSpec((tm, n_), lambda i: (i, 0))
    in_specs = (_halo_specs(tm, V7X_SUBLANES, L, DP)
                + [pl.BlockSpec((tm, DS), lambda i: (jnp.minimum(i, n // 2 - 1), 0)),
                   pl.BlockSpec((tm, DS), lambda i: (jnp.maximum(i - n // 2, 0), 0)), row(D)]
                + [_const_spec(a.shape) for a in (pool_w, pool_scale, w_out, g, g_next)])
    est = (2 * tm * (DP * 4 + 2 * DS * 2 + D * 4 + D * 4 + D * 2) + (w_out.size + pool_w.size) * 2
           + (tm + 2 * V7X_SUBLANES) * DP * 4 + 8 * tm * D * 4)
    return pl.pallas_call(
        functools.partial(_mixout_kernel, seq=L),
        grid=(n,),
        in_specs=in_specs,
        out_specs=[row(D), row(D)],
        out_shape=[jax.ShapeDtypeStruct((L, D), F32), jax.ShapeDtypeStruct((L, D), BF16)],
        scratch_shapes=[pltpu.VMEM((tm + 2 * V7X_SUBLANES, DP), F32), pltpu.VMEM((tm, D), F32)],
        compiler_params=pltpu.CompilerParams(dimension_semantics=("arbitrary",), vmem_limit_bytes=_vmem_limit(est)),
        name="mixout",
    )(u, u, u, y_lo, y_hi, x, pool_w, pool_scale, w_out, g, g_next)


def _ffn_kernel(hp_ref, hm_ref, hx_ref, wg_ref, wv_ref, cw_ref, cb_ref, wd_ref, gpost_ref,
                o_ref, hn_ref, gate_ref):
    i = pl.program_id(0)
    f = pl.program_id(1)
    n = pl.num_programs(0)
    nf = pl.num_programs(1)
    tm = hm_ref.shape[0]
    H = BF16_ROWS

    @pl.when(f == 0)
    def _():
        hn_ref[0:H, :] = jnp.where(i > 0, hp_ref[...], jnp.zeros_like(hp_ref))
        hn_ref[H:H + tm, :] = hm_ref[...]
        hn_ref[H + tm:2 * H + tm, :] = jnp.where(i < n - 1, hx_ref[...], jnp.zeros_like(hx_ref))
        o_ref[...] = jnp.zeros_like(o_ref)

    gate_ref[...] = jnp.dot(hn_ref[...], wg_ref[...], preferred_element_type=F32)
    wd = wd_ref[...].astype(BF16)
    val = jnp.dot(hm_ref[...], wv_ref[...], preferred_element_type=F32)
    half = FFN_CONV // 2
    gc = cw_ref[0:1, :] * gate_ref[H - half:H - half + tm, :]
    for j in range(1, FFN_CONV):
        gc = gc + cw_ref[j:j + 1, :] * gate_ref[H - half + j:H - half + j + tm, :]
    gc = gc + cb_ref[...]
    act = (jax.nn.gelu(gc, approximate=True) * val).astype(BF16)
    o_ref[...] += jnp.dot(act, wd, preferred_element_type=F32)

    @pl.when(f == nf - 1)
    def _():
        o_ref[...] = _rms(o_ref[...]) * gpost_ref[...]


def _ffn(hn, w_up, cw, cb, w_down, gpost, *, tm, tf):
    L, D = hn.shape
    DF = w_down.shape[0]
    nf = DF // tf
    H = BF16_ROWS
    in_specs = _halo_specs(tm, H, L, D) + [
        pl.BlockSpec((D, tf), lambda i, f: (0, f)),
        pl.BlockSpec((D, tf), lambda i, f: (0, nf + f)),
        pl.BlockSpec((FFN_CONV, tf), lambda i, f: (0, f)),
        pl.BlockSpec((1, tf), lambda i, f: (0, f)),
        pl.BlockSpec((tf, D), lambda i, f: (f, 0)),
        pl.BlockSpec(gpost.shape, lambda i, f: (0, 0)),
    ]
    est = (2 * (tm + 2 * H) * D * 2 + 2 * tm * D * 4 + (tm + 2 * H) * D * 2 + (tm + 2 * H) * tf * 4
           + 2 * 2 * D * tf * 2 + 2 * D * tf * w_down.dtype.itemsize + 8 * tm * tf * 4 + tm * D * 4)
    return pl.pallas_call(
        _ffn_kernel,
        grid=(L // tm, nf),
        in_specs=in_specs,
        out_specs=pl.BlockSpec((tm, D), lambda i, f: (i, 0)),
        out_shape=jax.ShapeDtypeStruct((L, D), F32),
        scratch_shapes=[
            pltpu.VMEM((tm + 2 * H, D), BF16),
            pltpu.VMEM((tm + 2 * H, tf), F32),
        ],
        compiler_params=pltpu.CompilerParams(dimension_semantics=("arbitrary", "arbitrary"),
                                             vmem_limit_bytes=_vmem_limit(est)),
        name="ffn",
    )(hn, hn, hn, w_up, w_up, cw, cb, w_down, gpost)


def _ple_kernel(h_ref, r_ref, p_ref, gpre_ref, wg_ref, wp_ref, gpost_ref, o_ref, hn_ref, gate_ref, pe_ref):
    tp = h_ref.shape[0] // PLE_PIECES
    piece = lambda k: slice(k * tp, (k + 1) * tp)

    def prenorm(k):
        rows = piece(k)
        h = h_ref[rows, :] + r_ref[rows, :]
        o_ref[rows, :] = h
        hn_ref[rows, :] = (_rms(h) * gpre_ref[...]).astype(BF16)

    def project(k):
        rows = piece(k)
        gate_ref[rows, :] = jnp.dot(hn_ref[rows, :], wg_ref[...], preferred_element_type=F32)

    def finish(k):
        rows = piece(k)
        o_ref[rows, :] = o_ref[rows, :] + _rms(jax.nn.sigmoid(gate_ref[rows, :]) * pe_ref[rows, :]) * gpost_ref[...]

    pe_ref[...] = jnp.dot(p_ref[...].astype(BF16), wp_ref[...], preferred_element_type=F32)
    prenorm(0)
    for k in range(PLE_PIECES):
        project(k)
        if k + 1 < PLE_PIECES:
            prenorm(k + 1)
        if k > 0:
            finish(k - 1)
    finish(PLE_PIECES - 1)


def _ple(h, r, p, gpre, w_gate, w_ple, gpost, *, tm):
    L, D = h.shape
    row = lambda n: pl.BlockSpec((tm, n), lambda i: (i, 0))
    est = 2 * tm * (3 * D + p.shape[1]) * 4 + (w_gate.size + w_ple.size) * 2 + 8 * tm * D * 4
    return pl.pallas_call(
        _ple_kernel,
        grid=(L // tm,),
        in_specs=[row(D), row(D), row(p.shape[1])] + [_const_spec(a.shape) for a in (gpre, w_gate, w_ple, gpost)],
        out_specs=row(D),
        out_shape=jax.ShapeDtypeStruct((L, D), F32),
        scratch_shapes=[pltpu.VMEM((tm, D), BF16), pltpu.VMEM((tm, D), F32), pltpu.VMEM((tm, D), F32)],
        compiler_params=pltpu.CompilerParams(dimension_semantics=("arbitrary",), vmem_limit_bytes=_vmem_limit(est)),
        name="ple",
    )(h, r, p, gpre, w_gate, w_ple, gpost)


def _ssd_constants():
    Q = SSD_CHUNK
    r = jnp.arange(Q)
    lower = (r[:, None] >= r[None, :])
    tri = jnp.stack([lower, lower.T]).astype(BF16)
    tcat = jnp.concatenate([tri, tri, tri], axis=2)
    negmask = jnp.where(jnp.stack([lower, lower.T]), 0.0, NEG_BIG).astype(F32)
    k = jnp.arange(V7X_LANES)
    rexp = []
    for d in range(2):
        packed = jnp.logical_and(k >= d * SSD_HEADS, k < d * SSD_HEADS + 3 * SSD_PACK)
        head_of_lane = jnp.where(packed, (k - d * SSD_HEADS) % SSD_PACK, -1)
        rexp.append(head_of_lane[:, None] == (jnp.arange(SSD_HEADS * SSD_HEAD_DIM) // SSD_HEAD_DIM)[None, :])
    return tcat, negmask, jnp.stack(rexp).astype(BF16)


def _dir_lanes(v, fill):
    out = jnp.full((1, V7X_LANES), fill, F32)
    return out.at[0, :v.size].set(v.astype(F32).reshape(-1))


def kernel(x, p, mix_norm_pre, mix_norm_post, w_in, ssd_conv_w, ssd_conv_b, ssd_dt_bias, ssd_a_log, ssd_d,
           ssd_norm, pool_w, pool_scale, w_out, ffn_norm_pre, ffn_norm_post, w_ffn_up, ffn_conv_w, ffn_conv_b,
           w_ffn_down, ple_norm_pre, w_ple_gate, w_ple, ple_norm_post):
    B, L, D = x.shape
    depth = w_in.shape[0]
    d_ssd = SSD_HEADS * SSD_HEAD_DIM
    n_bc = 2 * SSD_GROUPS * SSD_STATE
    o_dt = 2 * d_ssd + n_bc
    o_u = o_dt + 2 * SSD_HEADS
    d_pool = w_in.shape[2] - o_u
    tcat, negmask, rexp = _ssd_constants()
    row = lambda v: v.reshape(1, -1).astype(F32)

    outs = []
    for b in range(B):
        h = x[b]
        for i in range(depth):
            xs, bc, z, u, pk, sk, ac, w_out16, w_gate16 = _inproj(
                h, row(mix_norm_pre[i]), _cast_transposed(w_in[i].T, cols=CONV_PIECE), ssd_conv_w[i].astype(F32),
                row(ssd_conv_b[i]), _dir_lanes(ssd_dt_bias[i], 0.0), _dir_lanes(ssd_a_log[i], NEG_BIG), tcat,
                (d_ssd, n_bc, d_ssd, d_pool), w_out[i], w_ple_gate[i], tm=INPROJ_ROWS)
            y_lo, y_hi, w_up16 = _ssd(
                xs, bc, pk, sk, ac, z, jnp.repeat(ssd_d[i].astype(F32), SSD_HEAD_DIM).reshape(1, -1),
                row(ssd_norm[i]), negmask, rexp, w_ffn_up[i])
            h, hn = _mixout(u, y_lo, y_hi, h, pool_w[i].astype(BF16), row(pool_scale[i]), w_out16,
                            row(mix_norm_post[i]), row(ffn_norm_pre[i]), tm=MIXOUT_ROWS)
            r = _ffn(hn, w_up16, ffn_conv_w[i].astype(F32), row(ffn_conv_b[i]), w_ffn_down[i], row(ffn_norm_post[i]),
                     tm=FFN_ROWS, tf=FFN_COLS)
            h = _ple(h, r, p[i, b], row(ple_norm_pre[i]), w_gate16, w_ple[i].astype(BF16),
                     row(ple_norm_post[i]), tm=PLE_ROWS)
        outs.append(h)
    return jnp.stack(outs)
```

```python
import functools

import jax
import jax.numpy as jnp
from jax import lax
from jax.experimental import pallas as pl
from jax.experimental.pallas import tpu as pltpu

F32 = jnp.float32
BF16 = jnp.bfloat16
EPS = 1e-6

V7X_VMEM_BYTES = 64 * 1024 * 1024
V7X_LANES = 128
V7X_SUBLANES = 8
BF16_ROWS = 16

SSD_HEAD_DIM = 64
SSD_HEADS = 16
SSD_GROUPS = 2
SSD_HEADS_PER_GROUP = SSD_HEADS // SSD_GROUPS
SSD_STATE = 128
SSD_CONV = 5
SSD_CHUNK = 128
POOL_WINDOWS = (2, 4, 8, 16)
FFN_CONV = 3
NEG_BIG = -1e30

INPROJ_ROWS = 512
MIXOUT_ROWS = 512
MIXOUT_PIECES = 2
FFN_ROWS = 1024
FFN_COLS = 512
FFN_ACT_ROWS = 128
PLE_ROWS = 512
PLE_INPUT_BUFFERS = 3
CONV_PIECE = 256


def _vmem_limit(nbytes):
    return int(min(nbytes, V7X_VMEM_BYTES - 6 * 1024 * 1024))


def _rms(x):
    return x * lax.rsqrt(jnp.mean(x * x, axis=-1, keepdims=True) + EPS)


def _split3(x):
    hi = x.astype(BF16).astype(F32)
    r1 = x - hi
    mid = r1.astype(BF16).astype(F32)
    lo = (r1 - mid).astype(BF16).astype(F32)
    return hi, mid, lo


def _const_spec(shape):
    nd = len(shape)
    return pl.BlockSpec(shape, lambda *_: (0,) * nd, pipeline_mode=pl.Buffered(1))


def _halo_specs(rows, halo, total_rows, ncols):
    hb = rows // halo
    nhb = total_rows // halo
    return [
        pl.BlockSpec((halo, ncols), lambda i, *_: (jnp.maximum(i * hb - 1, 0), 0)),
        pl.BlockSpec((rows, ncols), lambda i, *_: (i, 0)),
        pl.BlockSpec((halo, ncols), lambda i, *_: (jnp.minimum(i * hb + hb, nhb - 1), 0)),
    ]


def _cast_t_kernel(w_ref, o_ref, *, valid_rows):
    row = pl.program_id(0) * w_ref.shape[0] + lax.broadcasted_iota(jnp.int32, w_ref.shape, 0)
    w = jnp.where(row < valid_rows, w_ref[...], 0.0)
    o_ref[...] = w.T.astype(o_ref.dtype)


def _cast_transposed(w_t, *, cols):
    C, R = w_t.shape
    n = pl.cdiv(C, cols)
    return pl.pallas_call(
        functools.partial(_cast_t_kernel, valid_rows=C),
        grid=(n,),
        in_specs=[pl.BlockSpec((cols, R), lambda j: (j, 0))],
        out_specs=pl.BlockSpec((R, cols), lambda j: (0, j)),
        out_shape=jax.ShapeDtypeStruct((R, n * cols), BF16),
        compiler_params=pltpu.CompilerParams(dimension_semantics=("arbitrary",),
                                             vmem_limit_bytes=_vmem_limit(V7X_VMEM_BYTES)),
        name="cast_w_in",
    )(w_t)


def _scan_prep(raw, dtb_ref, alog_ref, t_ref, pk_ref, sk_ref, ac_ref):
    Q = SSD_CHUNK
    lane = lax.broadcasted_iota(jnp.int32, (Q, V7X_LANES), 1)
    head_lanes = [jnp.logical_and(lane >= d * SSD_HEADS, lane < (d + 1) * SSD_HEADS) for d in (0, 1)]
    a = -jnp.exp(alog_ref[...])
    for c in range(raw.shape[0] // Q):
        dt = jax.nn.softplus(raw[c * Q:(c + 1) * Q, :] + dtb_ref[...])
        da3 = jnp.concatenate([v.astype(BF16) for v in _split3(dt * a)], axis=0)
        acum = jnp.where(head_lanes[0], jnp.dot(t_ref[0], da3, preferred_element_type=F32),
                         jnp.dot(t_ref[1], da3, preferred_element_type=F32)) * LOG2E
        tot = jnp.where(head_lanes[0][0:1], acum[Q - 1:Q, :], acum[0:1, :])
        ac_ref[c * Q:(c + 1) * Q, :] = acum
        sk_ref[c] = (acum - jnp.log2(dt)).T[0:2 * SSD_HEADS, :]
        e_a = jnp.exp2(acum)
        dtw = dt * jnp.exp2(tot - acum)
        for d in (0, 1):
            pk_ref[c, d, 0:Q, :] = _pack3(dtw, head_lanes[d])
            pk_ref[c, d, Q:2 * Q, :] = _pack3(e_a, head_lanes[d])


def _inproj_kernel(xp_ref, xm_ref, xn_ref, g_ref, w_ref, cw_ref, cb_ref, dtb_ref, alog_ref, t_ref,
                   wo_ref, wg_ref,
                   xs_ref, bc_ref, z_ref, u_ref, pk_ref, sk_ref, ac_ref, wo16_ref, wg16_ref,
                   hn_ref, *ext_refs):
    i = pl.program_id(0)
    n = pl.num_programs(0)
    tm = xm_ref.shape[0]
    H = BF16_ROWS
    g = g_ref[...]
    hn_ref[0:H, :] = jnp.where(i > 0, _rms(xp_ref[...]) * g, 0.0).astype(BF16)
    hn_ref[H:H + tm, :] = (_rms(xm_ref[...]) * g).astype(BF16)
    hn_ref[H + tm:2 * H + tm, :] = jnp.where(i < n - 1, _rms(xn_ref[...]) * g, 0.0).astype(BF16)
    for src, dst in ((wo_ref, wo16_ref), (wg_ref, wg16_ref)):
        dst[...] = src[...].astype(BF16)

    cs = CONV_PIECE
    nz, nxs, nbc, nu = z_ref.shape[1], xs_ref.shape[1], bc_ref.shape[1], u_ref.shape[1]
    o_dt = nz + nxs + nbc
    n_dt = 2 * SSD_HEADS
    w_end = o_dt + n_dt + nu
    conv_outs = [(xs_ref, c, nz + c, c) for c in range(0, nxs, cs)]
    conv_outs += [(bc_ref, c, nz + nxs + c, nxs + c) for c in range(0, nbc, cs)]

    def project(c0, c1):
        return jnp.dot(hn_ref[H:H + tm, :], w_ref[:, c0:c1], preferred_element_type=F32)

    def z_chunk(c):
        z_ref[:, c:c + cs] = project(c, c + cs)

    split = o_dt + 2 * cs
    tail = [project(o_dt, split)]
    _scan_prep(tail[0][:, 0:V7X_LANES], dtb_ref, alog_ref, t_ref, pk_ref, sk_ref, ac_ref)
    plain = [functools.partial(z_chunk, c) for c in range(0, nz, cs)]
    plain += [lambda: tail.append(project(split, w_end))]

    half = SSD_CONV // 2
    for k, (o_ref, c0, wc, cc) in enumerate(conv_outs):
        ext_ref = ext_refs[k % len(ext_refs)]
        ext_ref[...] = jnp.dot(hn_ref[...], w_ref[:, wc:wc + cs], preferred_element_type=F32)
        if k < len(plain):
            plain[k]()
        acc = cw_ref[0:1, cc:cc + cs] * ext_ref[H - half:H - half + tm, :]
        for j in range(1, SSD_CONV):
            acc = acc + cw_ref[j:j + 1, cc:cc + cs] * ext_ref[H - half + j:H - half + j + tm, :]
        acc = acc + cb_ref[:, cc:cc + cs]
        o_ref[:, c0:c0 + cs] = (acc * jax.nn.sigmoid(acc)).astype(o_ref.dtype)
    for fn in plain[len(conv_outs):]:
        fn()
    u_ref[...] = jnp.concatenate([tail[0][:, n_dt:], tail[1]], axis=1)


def _inproj(x, g, w, cw, cb, dtb, alog, tcat, out_cols, w_out, w_gate, *, tm):
    L, D = x.shape
    H = BF16_ROWS
    Q = SSD_CHUNK
    n = L // tm
    dts = (BF16, BF16, F32, F32)
    row = lambda c: pl.BlockSpec((tm, c), lambda i: (i, 0))
    slab = lambda a: pl.BlockSpec((a.shape[0] // n, a.shape[1]), lambda i: (i, 0))
    casts = (w_out, w_gate)
    consts = (g, w, cw, cb, dtb, alog, tcat)
    prep_shapes = [((L // Q, 2, 2 * Q, V7X_LANES), BF16), ((L // Q, 2 * SSD_HEADS, Q), F32), ((L, V7X_LANES), F32)]
    prep_specs = [pl.BlockSpec((tm // Q,) + s[1:], lambda i, nd=len(s): (i,) + (0,) * (nd - 1))
                  for s, _ in prep_shapes[:2]] + [row(V7X_LANES)]
    est = (2 * (tm + 2 * H) * D * 4 + w.size * 2 + 2 * tm * sum(out_cols) * 4
           + (tm + 2 * H) * D * 2 + 4 * (tm + 2 * H) * CONV_PIECE * 4 + 2 * sum(a.size for a in casts) // n * 6
           + 24 * tm * CONV_PIECE * 4)
    return pl.pallas_call(
        _inproj_kernel,
        grid=(n,),
        in_specs=_halo_specs(tm, H, L, D) + [_const_spec(a.shape) for a in consts] + [slab(a) for a in casts],
        out_specs=[row(c) for c in out_cols] + prep_specs + [slab(a) for a in casts],
        out_shape=[jax.ShapeDtypeStruct((L, c), dt) for c, dt in zip(out_cols, dts)]
                  + [jax.ShapeDtypeStruct(s, dt) for s, dt in prep_shapes]
                  + [jax.ShapeDtypeStruct(a.shape, BF16) for a in casts],
        scratch_shapes=[pltpu.VMEM((tm + 2 * H, D), BF16)]
                       + [pltpu.VMEM((tm + 2 * H, CONV_PIECE), F32) for _ in range(2)],
        compiler_params=pltpu.CompilerParams(dimension_semantics=("arbitrary",), vmem_limit_bytes=_vmem_limit(est)),
        name="inproj",
    )(x, x, x, *consts, *casts)


SSD_PACK = 16


LOG2E = 1.4426950408889634


def _pack3(v, head_lanes):
    hi, mid, lo = _split3(jnp.where(head_lanes, v, jnp.zeros_like(v)))
    packed = hi + pltpu.roll(mid, SSD_PACK, axis=1) + pltpu.roll(lo, 2 * SSD_PACK, axis=1)
    return packed.astype(BF16)


def _scan_both(xs_refs, bc_refs, pk_refs, sk_refs, ac_refs, nm_ref, rexp_ref, h_ref):
    Q = SSD_CHUNK
    P = SSD_HEAD_DIM
    E = SSD_HEADS_PER_GROUP
    GW = E * P
    DIRS = (0, 1)
    last = (Q - 1, 0)
    xs16 = [xs_refs[d][...] for d in DIRS]
    bc = [bc_refs[d][...] for d in DIRS]
    src_t = [sk_refs[d][...] for d in DIRS]

    exp2x = [jnp.dot(pk_refs[d][0:2 * Q, :], rexp_ref[d], preferred_element_type=F32) for d in DIRS]
    acum = [ac_refs[d][...] for d in DIRS]
    xw = [xs16[d] * exp2x[d][0:Q].astype(BF16) for d in DIRS]
    ea_x = [exp2x[d][Q:2 * Q] for d in DIRS]
    negmask = [nm_ref[d] for d in DIRS]
    first_head = lax.broadcasted_iota(jnp.int32, (Q, 2 * P), 1) < P

    y_groups = [[], []]
    for g in range(SSD_GROUPS):
        c0 = g * GW
        bm = [bc[d][:, g * SSD_STATE:(g + 1) * SSD_STATE] for d in DIRS]
        cm = [bc[d][:, (SSD_GROUPS + g) * SSD_STATE:(SSD_GROUPS + g + 1) * SSD_STATE] for d in DIRS]
        cb = [lax.dot_general(cm[d], bm[d], (((1,), (1,)), ((), ())), preferred_element_type=F32)
              for d in DIRS]
        st = [lax.dot_general(bm[d], xw[d][:, c0:c0 + GW], (((0,), (0,)), ((), ())), preferred_element_type=F32)
              for d in DIRS]
        h_in = [h_ref[d, g] for d in DIRS]
        y_off = [jnp.dot(cm[d], h_in[d].astype(BF16), preferred_element_type=F32) for d in DIRS]
        y_g = [y_off[d] * ea_x[d][:, c0:c0 + GW] for d in DIRS]
        for d in DIRS:
            h_ref[d, g] = h_in[d] * ea_x[d][last[d]:last[d] + 1, c0:c0 + GW] + st[d]
        pairs = [[], []]
        for hp in range(E // 2):
            for d in DIRS:
                h0 = g * E + hp * 2
                ms = []
                for k in range(2):
                    r = d * SSD_HEADS + h0 + k
                    seg = jnp.broadcast_to(acum[d][:, r:r + 1], (Q, Q)) - src_t[d][r:r + 1, :] + negmask[d]
                    ms.append((cb[d] * jnp.exp2(seg)).astype(BF16))
                m2 = jnp.concatenate(ms, axis=1)
                xp = xs16[d][:, (g * E + hp * 2) * P:(g * E + hp * 2 + 2) * P]
                zero = jnp.zeros_like(xp)
                rhs = jnp.concatenate([jnp.where(first_head, xp, zero), jnp.where(first_head, zero, xp)], axis=0)
                pairs[d].append(jnp.dot(m2, rhs, preferred_element_type=F32))
        for d in DIRS:
            y_groups[d].append(y_g[d] + jnp.concatenate(pairs[d], axis=1))
    return [(jnp.concatenate(y_groups[d], axis=1), xs16[d]) for d in DIRS]


def _ssd_kernel(xsf_ref, bcf_ref, pkf_ref, skf_ref, acf_ref, zf_ref, xsb_ref, bcb_ref, pkb_ref, skb_ref, acb_ref, zb_ref,
                dexp_ref, nw_ref, nm_ref, rexp_ref, wup_ref,
                lo_ref, hi_ref, wup16_ref,
                y_ref, h_ref):
    Q = SSD_CHUNK
    GW = SSD_HEADS_PER_GROUP * SSD_HEAD_DIM
    i = pl.program_id(0)
    nc = pl.num_programs(0)

    wup16_ref[...] = wup_ref[...].astype(BF16)

    @pl.when(i == 0)
    def _():
        h_ref[...] = jnp.zeros_like(h_ref)

    (y_f, xs_f), (y_b, xs_b) = _scan_both((xsf_ref, xsb_ref), (bcf_ref, bcb_ref), (pkf_ref, pkb_ref),
                                          (skf_ref, skb_ref), (acf_ref, acb_ref), nm_ref, rexp_ref, h_ref)
    row_f = pl.multiple_of(i * Q, Q)
    row_b = pl.multiple_of((nc - 1 - i) * Q, Q)

    @pl.when(i < nc // 2)
    def _():
        y_ref[pl.ds(row_f, Q), :] = y_f
        y_ref[pl.ds(row_b, Q), :] = y_b

    def finish(y, xs, z_ref, o_ref):
        yy = y + xs.astype(F32) * dexp_ref[...]
        z = z_ref[...]
        yy = yy * (z * jax.nn.sigmoid(z))
        outs = [_rms(yy[:, g * GW:(g + 1) * GW]) for g in range(SSD_GROUPS)]
        o_ref[...] = (jnp.concatenate(outs, axis=1) * nw_ref[...]).astype(o_ref.dtype)

    @pl.when(i >= nc // 2)
    def _():
        finish(y_f + y_ref[pl.ds(row_f, Q), :], xs_f, zf_ref, hi_ref)
        finish(y_b + y_ref[pl.ds(row_b, Q), :], xs_b, zb_ref, lo_ref)


def _ssd(xs, bc, pk, sk, ac, z, dexp, nw, negmask, rexp, w_up):
    L, DS = xs.shape
    Q = SSD_CHUNK
    nc = L // Q
    hc = nc // 2
    slab = pl.BlockSpec((w_up.shape[0] // nc, w_up.shape[1]), lambda i: (i, 0))
    fwd = lambda i: i
    bwd = lambda i: nc - 1 - i
    fwd_late = lambda i: jnp.maximum(i, hc)
    bwd_late = lambda i: jnp.minimum(nc - 1 - i, hc - 1)
    in_specs = [
        pl.BlockSpec((Q, DS), lambda i: (fwd(i), 0)),
        pl.BlockSpec((Q, bc.shape[1]), lambda i: (fwd(i), 0)),
        pl.BlockSpec((None, None) + pk.shape[2:], lambda i: (fwd(i), 0, 0, 0)),
        pl.BlockSpec((None,) + sk.shape[1:], lambda i: (fwd(i), 0, 0)),
        pl.BlockSpec((Q, V7X_LANES), lambda i: (fwd(i), 0)),
        pl.BlockSpec((Q, DS), lambda i: (fwd_late(i), 0)),
        pl.BlockSpec((Q, DS), lambda i: (bwd(i), 0)),
        pl.BlockSpec((Q, bc.shape[1]), lambda i: (bwd(i), 0)),
        pl.BlockSpec((None, None) + pk.shape[2:], lambda i: (bwd(i), 1, 0, 0)),
        pl.BlockSpec((None,) + sk.shape[1:], lambda i: (bwd(i), 0, 0)),
        pl.BlockSpec((Q, V7X_LANES), lambda i: (bwd(i), 0)),
        pl.BlockSpec((Q, DS), lambda i: (bwd_late(i), 0)),
    ] + [_const_spec(a.shape) for a in (dexp, nw, negmask, rexp)] + [slab]
    est = (L * DS * 4 + 2 * SSD_GROUPS * SSD_STATE * DS * 4 + 8 * Q * (DS + bc.shape[1]) * 4
           + rexp.size * 2 + 4 * Q * DS * 4 * 2 + 40 * Q * DS * 4 + 12 * w_up.size // nc)
    return pl.pallas_call(
        _ssd_kernel,
        grid=(nc,),
        in_specs=in_specs,
        out_specs=[pl.BlockSpec((Q, DS), lambda i: (bwd_late(i), 0)),
                   pl.BlockSpec((Q, DS), lambda i: (fwd_late(i) - hc, 0)), slab],
        out_shape=[jax.ShapeDtypeStruct((L // 2, DS), BF16), jax.ShapeDtypeStruct((L // 2, DS), BF16),
                   jax.ShapeDtypeStruct(w_up.shape, BF16)],
        scratch_shapes=[
            pltpu.VMEM((L, DS), F32),
            pltpu.VMEM((2, SSD_GROUPS, SSD_STATE, DS // SSD_GROUPS), F32),
        ],
        compiler_params=pltpu.CompilerParams(dimension_semantics=("arbitrary",), vmem_limit_bytes=_vmem_limit(est)),
        name="ssd",
    )(xs, bc, pk, sk, ac, z, xs, bc, pk, sk, ac, z, dexp, nw, negmask, rexp, w_up)


def _mixout_kernel(up_ref, um_ref, un_ref, ylo_ref, yhi_ref, x_ref, pw_ref, ps_ref, wo_ref, g_ref, gn_ref,
                   o_ref, on_ref, ext_ref, mix_ref, *, seq):
    i = pl.program_id(0)
    n = pl.num_programs(0)
    tm = um_ref.shape[0]
    cg = um_ref.shape[1] // len(POOL_WINDOWS)
    H = V7X_SUBLANES
    ext_ref[0:H, :] = jnp.where(i > 0, up_ref[...], 0.0)
    ext_ref[H:H + tm, :] = um_ref[...]
    ext_ref[H + tm:2 * H + tm, :] = jnp.where(i < n - 1, un_ref[...], 0.0)
    ys = jnp.where(i < n // 2, ylo_ref[...], yhi_ref[...])
    ds = ys.shape[1]
    dc = o_ref.shape[1] // len(POOL_WINDOWS)
    t = i * tm + lax.broadcasted_iota(jnp.int32, (tm, cg), 0)
    rows = tm + 2 * H

    def ahead(v, k):
        return pltpu.roll(v, (rows - k) % rows, axis=0)

    pooled = []
    for gi, k in enumerate(POOL_WINDOWS):
        cols = slice(gi * cg, (gi + 1) * cg)
        mix_ref[:, gi * dc:(gi + 1) * dc] = jnp.dot(ys, wo_ref[0:ds, gi * dc:(gi + 1) * dc],
                                                    preferred_element_type=F32)
        e = ext_ref[:, cols]
        half = k // 2
        run, length = e, 1
        while length < half:
            run = run + ahead(run, length)
            length *= 2
        before = run[0:tm] if half == H else ahead(run, rows - half)[H:H + tm]
        acc = before + run[H:H + tm]
        cnt = (jnp.minimum(t + (k - k // 2), seq) - jnp.maximum(t - k // 2, 0)).astype(F32)
        mixed = acc / cnt - um_ref[:, cols]
        yp = jnp.dot(mixed.astype(BF16), pw_ref[gi], preferred_element_type=F32) * ps_ref[:, cols]
        pooled.append(yp.astype(BF16))
    ypool = jnp.concatenate(pooled, axis=1)
    tp = tm // MIXOUT_PIECES

    def project(k):
        rows = slice(k * tp, (k + 1) * tp)
        mix_ref[rows, :] += jnp.dot(ypool[rows], wo_ref[ds:, :], preferred_element_type=F32)

    def finish(k):
        rows = slice(k * tp, (k + 1) * tp)
        h = x_ref[rows, :] + _rms(mix_ref[rows, :]) * g_ref[...]
        o_ref[rows, :] = h
        on_ref[rows, :] = (_rms(h) * gn_ref[...]).astype(on_ref.dtype)

    project(0)
    for k in range(1, MIXOUT_PIECES):
        project(k)
        finish(k - 1)
    finish(MIXOUT_PIECES - 1)


def _mixout(u, y_lo, y_hi, x, pool_w, pool_scale, w_out, g, g_next, *, tm):
    assert all(k % 2 == 0 and k // 2 <= V7X_SUBLANES and (k // 2) & (k // 2 - 1) == 0 for k in POOL_WINDOWS)
    L, DP = u.shape
    D = x.shape[1]
    DS = y_lo.shape[1]
    n = L // tm
    row = lambda n_: pl.BlockSpec((tm, n_), lambda i: (i, 0))
    in_specs = (_halo_specs(tm, V7X_SUBLANES, L, DP)
                + [pl.BlockSpec((tm, DS), lambda i: (jnp.minimum(i, n // 2 - 1), 0)),
                   pl.BlockSpec((tm, DS), lambda i: (jnp.maximum(i - n // 2, 0), 0)), row(D)]
                + [_const_spec(a.shape) for a in (pool_w, pool_scale, w_out, g, g_next)])
    est = (2 * tm * (DP * 4 + 2 * DS * 2 + D * 4 + D * 4 + D * 2) + (w_out.size + pool_w.size) * 2
           + (tm + 2 * V7X_SUBLANES) * DP * 4 + 8 * tm * D * 4)
    return pl.pallas_call(
        functools.partial(_mixout_kernel, seq=L),
        grid=(n,),
        in_specs=in_specs,
        out_specs=[row(D), row(D)],
        out_shape=[jax.ShapeDtypeStruct((L, D), F32), jax.ShapeDtypeStruct((L, D), BF16)],
        scratch_shapes=[pltpu.VMEM((tm + 2 * V7X_SUBLANES, DP), F32), pltpu.VMEM((tm, D), F32)],
        compiler_params=pltpu.CompilerParams(dimension_semantics=("arbitrary",), vmem_limit_bytes=_vmem_limit(est)),
        name="mixout",
    )(u, u, u, y_lo, y_hi, x, pool_w, pool_scale, w_out, g, g_next)


def _ffn_kernel(hp_ref, hm_ref, hx_ref, wg_ref, wv_ref, cw_ref, cb_ref, wd_ref, gpost_ref,
                o_ref, hn_ref, gate_ref, val_ref, act_ref):
    i = pl.program_id(0)
    f = pl.program_id(1)
    n = pl.num_programs(0)
    nf = pl.num_programs(1)
    tm = hm_ref.shape[0]
    H = BF16_ROWS

    @pl.when(f == 0)
    def _():
        hn_ref[0:H, :] = jnp.where(i > 0, hp_ref[...], jnp.zeros_like(hp_ref))
        hn_ref[H:H + tm, :] = hm_ref[...]
        hn_ref[H + tm:2 * H + tm, :] = jnp.where(i < n - 1, hx_ref[...], jnp.zeros_like(hx_ref))
        o_ref[...] = jnp.zeros_like(o_ref)

    gate_ref[...] = jnp.dot(hn_ref[...], wg_ref[...], preferred_element_type=F32)
    wd = wd_ref[...].astype(BF16)
    val_ref[...] = jnp.dot(hm_ref[...], wv_ref[...], preferred_element_type=F32)
    half = FFN_CONV // 2
    for r in range(0, tm, FFN_ACT_ROWS):
        gc = cw_ref[0:1, :] * gate_ref[H - half + r:H - half + r + FFN_ACT_ROWS, :]
        for j in range(1, FFN_CONV):
            gc = gc + cw_ref[j:j + 1, :] * gate_ref[H - half + j + r:H - half + j + r + FFN_ACT_ROWS, :]
        gc = gc + cb_ref[...]
        act_ref[r:r + FFN_ACT_ROWS, :] = (jax.nn.gelu(gc, approximate=True)
                                          * val_ref[r:r + FFN_ACT_ROWS, :]).astype(BF16)
    o_ref[...] += jnp.dot(act_ref[...], wd, preferred_element_type=F32)

    @pl.when(f == nf - 1)
    def _():
        o_ref[...] = _rms(o_ref[...]) * gpost_ref[...]


def _ffn(hn, w_up, cw, cb, w_down, gpost, *, tm, tf):
    L, D = hn.shape
    DF = w_down.shape[0]
    nf = DF // tf
    H = BF16_ROWS
    in_specs = _halo_specs(tm, H, L, D) + [
        pl.BlockSpec((D, tf), lambda i, f: (0, f)),
        pl.BlockSpec((D, tf), lambda i, f: (0, nf + f)),
        pl.BlockSpec((FFN_CONV, tf), lambda i, f: (0, f)),
        pl.BlockSpec((1, tf), lambda i, f: (0, f)),
        pl.BlockSpec((tf, D), lambda i, f: (f, 0)),
        pl.BlockSpec(gpost.shape, lambda i, f: (0, 0)),
    ]
    est = (2 * (tm + 2 * H) * D * 2 + 2 * tm * D * 4 + (tm + 2 * H) * D * 2 + (tm + 2 * H) * tf * 4
           + 2 * 2 * D * tf * 2 + 2 * D * tf * w_down.dtype.itemsize + 8 * tm * tf * 4 + tm * D * 4)
    return pl.pallas_call(
        _ffn_kernel,
        grid=(L // tm, nf),
        in_specs=in_specs,
        out_specs=pl.BlockSpec((tm, D), lambda i, f: (i, 0)),
        out_shape=jax.ShapeDtypeStruct((L, D), F32),
        scratch_shapes=[
            pltpu.VMEM((tm + 2 * H, D), BF16),
            pltpu.VMEM((tm + 2 * H, tf), F32),
            pltpu.VMEM((tm, tf), F32),
            pltpu.VMEM((tm, tf), BF16),
        ],
        compiler_params=pltpu.CompilerParams(dimension_semantics=("arbitrary", "arbitrary"),
                                             vmem_limit_bytes=_vmem_limit(est)),
        name="ffn",
    )(hn, hn, hn, w_up, w_up, cw, cb, w_down, gpost)


def _ple_kernel(h_ref, r_ref, p_ref, gpre_ref, wg_ref, wp_ref, gpost_ref, o_ref):
    h = h_ref[...] + r_ref[...]
    hn = (_rms(h) * gpre_ref[...]).astype(BF16)
    gate = jax.nn.sigmoid(jnp.dot(hn, wg_ref[...], preferred_element_type=F32))
    pe = jnp.dot(p_ref[...].astype(BF16), wp_ref[...], preferred_element_type=F32)
    o_ref[...] = h + _rms(gate * pe) * gpost_ref[...]


def _ple(h, r, p, gpre, w_gate, w_ple, gpost, *, tm):
    L, D = h.shape
    row = lambda n, **kw: pl.BlockSpec((tm, n), lambda i: (i, 0), **kw)
    deep = dict(pipeline_mode=pl.Buffered(PLE_INPUT_BUFFERS))
    est = ((2 * PLE_INPUT_BUFFERS + 2) * tm * D * 4 + 2 * tm * p.shape[1] * 4 + (w_gate.size + w_ple.size) * 2
           + 8 * tm * D * 4)

    def outer(h_hbm, r_hbm, p_hbm, gpre_ref, wg_ref, wp_ref, gpost_ref, o_hbm):
        def step(h_ref, r_ref, p_ref, o_ref):
            _ple_kernel(h_ref, r_ref, p_ref, gpre_ref, wg_ref, wp_ref, gpost_ref, o_ref)

        pltpu.emit_pipeline(step, grid=(L // tm,), in_specs=[row(D, **deep), row(D, **deep), row(p.shape[1])],
                            out_specs=[row(D)])(h_hbm, r_hbm, p_hbm, o_hbm)

    hbm = pl.BlockSpec(memory_space=pl.ANY)
    vmem = pl.BlockSpec(memory_space=pltpu.VMEM)
    return pl.pallas_call(
        outer,
        in_specs=[hbm, hbm, hbm, vmem, vmem, vmem, vmem],
        out_specs=hbm,
        out_shape=jax.ShapeDtypeStruct((L, D), F32),
        compiler_params=pltpu.CompilerParams(vmem_limit_bytes=_vmem_limit(est)),
        name="ple",
    )(h, r, p, gpre, w_gate, w_ple, gpost)


def _ssd_constants():
    Q = SSD_CHUNK
    r = jnp.arange(Q)
    lower = (r[:, None] >= r[None, :])
    tri = jnp.stack([lower, lower.T]).astype(BF16)
    tcat = jnp.concatenate([tri, tri, tri], axis=2)
    negmask = jnp.where(jnp.stack([lower, lower.T]), 0.0, NEG_BIG).astype(F32)
    k = jnp.arange(V7X_LANES)
    rexp = []
    for d in range(2):
        packed = jnp.logical_and(k >= d * SSD_HEADS, k < d * SSD_HEADS + 3 * SSD_PACK)
        head_of_lane = jnp.where(packed, (k - d * SSD_HEADS) % SSD_PACK, -1)
        rexp.append(head_of_lane[:, None] == (jnp.arange(SSD_HEADS * SSD_HEAD_DIM) // SSD_HEAD_DIM)[None, :])
    return tcat, negmask, jnp.stack(rexp).astype(BF16)


def _dir_lanes(v, fill):
    out = jnp.full((1, V7X_LANES), fill, F32)
    return out.at[0, :v.size].set(v.astype(F32).reshape(-1))


def kernel(x, p, mix_norm_pre, mix_norm_post, w_in, ssd_conv_w, ssd_conv_b, ssd_dt_bias, ssd_a_log, ssd_d,
           ssd_norm, pool_w, pool_scale, w_out, ffn_norm_pre, ffn_norm_post, w_ffn_up, ffn_conv_w, ffn_conv_b,
           w_ffn_down, ple_norm_pre, w_ple_gate, w_ple, ple_norm_post):
    B, L, D = x.shape
    depth = w_in.shape[0]
    d_ssd = SSD_HEADS * SSD_HEAD_DIM
    n_bc = 2 * SSD_GROUPS * SSD_STATE
    o_dt = 2 * d_ssd + n_bc
    o_u = o_dt + 2 * SSD_HEADS
    d_pool = w_in.shape[2] - o_u
    tcat, negmask, rexp = _ssd_constants()
    row = lambda v: v.reshape(1, -1).astype(F32)

    outs = []
    for b in range(B):
        h = x[b]
        for i in range(depth):
            xs, bc, z, u, pk, sk, ac, w_out16, w_gate16 = _inproj(
                h, row(mix_norm_pre[i]), _cast_transposed(w_in[i].T, cols=CONV_PIECE), ssd_conv_w[i].astype(F32),
                row(ssd_conv_b[i]), _dir_lanes(ssd_dt_bias[i], 0.0), _dir_lanes(ssd_a_log[i], NEG_BIG), tcat,
                (d_ssd, n_bc, d_ssd, d_pool), w_out[i], w_ple_gate[i], tm=INPROJ_ROWS)
            y_lo, y_hi, w_up16 = _ssd(
                xs, bc, pk, sk, ac, z, jnp.repeat(ssd_d[i].astype(F32), SSD_HEAD_DIM).reshape(1, -1),
                row(ssd_norm[i]), negmask, rexp, w_ffn_up[i])
            h, hn = _mixout(u, y_lo, y_hi, h, pool_w[i].astype(BF16), row(pool_scale[i]), w_out16,
                            row(mix_norm_post[i]), row(ffn_norm_pre[i]), tm=MIXOUT_ROWS)
            r = _ffn(hn, w_up16, ffn_conv_w[i].astype(F32), row(ffn_conv_b[i]), w_ffn_down[i], row(ffn_norm_post[i]),
                     tm=FFN_ROWS, tf=FFN_COLS)
            h = _ple(h, r, p[i, b], row(ple_norm_pre[i]), w_gate16, w_ple[i].astype(BF16),
                     row(ple_norm_post[i]), tm=PLE_ROWS)
        outs.append(h)
    return jnp.stack(outs)
```
